```python
import math
import jax
import jax.numpy as jnp
from jax import lax
import numpy as np

D_MODEL = 2048
BATCH = 2
SEQ = 16384
DEPTH = 2

N_MIXERS = 2
N_META = 16
CONV_WIDTH = 31
HEAD_DIM = 64
N_HEADS = D_MODEL // HEAD_DIM
N_KV_HEADS = N_HEADS // 8
GROUP = N_HEADS // N_KV_HEADS
WINDOW = 128
BLOCK = 128
N_BUCKETS = 32
MAX_DISTANCE = 128
D_FF = ((8 * D_MODEL + 3 * 256 - 1) // (3 * 256)) * 256
QKV_WIDTH = (N_HEADS + 2 * N_KV_HEADS) * HEAD_DIM
RMS_EPS = 1e-6
LN_EPS = 1e-5
NEG_INF = -1e30

kernel_name = 'hybrid_conformer_conv_swa_sink_block'


def rms_norm(x, g):
    xf = x.astype(jnp.float32)
    y = xf * lax.rsqrt(jnp.mean(xf * xf, axis=-1, keepdims=True) + RMS_EPS)
    return (y * g.astype(jnp.float32)).astype(x.dtype)


def layer_norm(x, g, b):
    xf = x.astype(jnp.float32)
    mu = jnp.mean(xf, axis=-1, keepdims=True)
    var = jnp.mean(jnp.square(xf - mu), axis=-1, keepdims=True)
    y = (xf - mu) * lax.rsqrt(var + LN_EPS)
    return (y * g.astype(jnp.float32) + b.astype(jnp.float32)).astype(x.dtype)


def t5_bucket(dist):
    max_exact = N_BUCKETS // 2
    d = jnp.maximum(dist, max_exact).astype(jnp.float32)
    large = max_exact + (jnp.log(d / max_exact) / math.log(MAX_DISTANCE / max_exact)
                         * (N_BUCKETS - max_exact)).astype(jnp.int32)
    return jnp.where(dist < max_exact, dist, jnp.minimum(large, N_BUCKETS - 1))


def rel_bias_lookup(rel_bias, dist):
    b = rel_bias.astype(jnp.float32)[t5_bucket(dist)]
    b = jnp.moveaxis(b, -1, -3)
    return b.reshape(b.shape[:-3] + (N_KV_HEADS, GROUP) + b.shape[-2:])


def sink_softmax(scores, sinks):
    sink = jnp.broadcast_to(sinks[None, :, :, None, None], scores.shape[:-1] + (1,))
    p = jax.nn.softmax(jnp.concatenate([scores, sink], axis=-1), axis=-1)
    return p[..., :-1]


def conformer_conv(h, w_in, b_in, w_dw, b_dw, ln_g, ln_b, w_out, b_out):
    u = h @ w_in + b_in
    a, g = jnp.split(u, 2, axis=-1)
    u = a * jax.nn.sigmoid(g)
    u = jnp.pad(u, ((0, 0), (CONV_WIDTH - 1, 0), (0, 0)))
    u = lax.conv_general_dilated(
        u, w_dw[:, None, :].astype(u.dtype), window_strides=(1,), padding='VALID',
        dimension_numbers=('NWC', 'WIO', 'NWC'), feature_group_count=D_MODEL) + b_dw
    u = layer_norm(u, ln_g, ln_b)
    u = jax.nn.silu(u)
    return u @ w_out + b_out


def sliding_window_sink_attention(h, w_qkv, b_qkv, sinks, w_o, b_o, rel_bias):
    B, L, _ = h.shape
    S = L - N_META
    n_blk = S // BLOCK
    qkv = h @ w_qkv + b_qkv
    q, k, v = jnp.split(qkv, [N_HEADS * HEAD_DIM, (N_HEADS + N_KV_HEADS) * HEAD_DIM], axis=-1)
    q = q.reshape(B, L, N_KV_HEADS, GROUP, HEAD_DIM) * (HEAD_DIM ** -0.5)
    k = k.reshape(B, L, N_KV_HEADS, HEAD_DIM)
    v = v.reshape(B, L, N_KV_HEADS, HEAD_DIM)
    sink_logits = sinks.astype(jnp.float32).reshape(N_KV_HEADS, GROUP)
    q_m, q_r = q[:, :N_META], q[:, N_META:]
    k_m, k_r = k[:, :N_META], k[:, N_META:]
    v_m, v_r = v[:, :N_META], v[:, N_META:]

    d_mm = jnp.arange(N_META)[:, None] - jnp.arange(N_META)[None, :]
    s_mm = (jnp.einsum('bqhgd,bkhd->bhgqk', q_m, k_m).astype(jnp.float32)
            + rel_bias_lookup(rel_bias, jnp.maximum(d_mm, 0)))
    s_mm = jnp.where(d_mm >= 0, s_mm, NEG_INF)
    p_mm = sink_softmax(s_mm, sink_logits).astype(v.dtype)
    o_m = jnp.einsum('bhgqk,bkhd->bqhgd', p_mm, v_m)

    pad = ((0, 0), (BLOCK, 0), (0, 0), (0, 0))
    k_p = jnp.pad(k_r, pad).reshape(B, n_blk + 1, BLOCK, N_KV_HEADS, HEAD_DIM)
    v_p = jnp.pad(v_r, pad).reshape(B, n_blk + 1, BLOCK, N_KV_HEADS, HEAD_DIM)
    k_band = jnp.concatenate([k_p[:, :-1], k_p[:, 1:]], axis=2)
    v_band = jnp.concatenate([v_p[:, :-1], v_p[:, 1:]], axis=2)
    q_blk = q_r.reshape(B, n_blk, BLOCK, N_KV_HEADS, GROUP, HEAD_DIM)

    qpos = jnp.arange(BLOCK)[:, None]
    kpos = jnp.arange(2 * BLOCK)[None, :]
    mpos = jnp.arange(N_META)[None, :]
    d_band = BLOCK + qpos - kpos
    in_window = (d_band >= 0) & (d_band < WINDOW)
    bias_band = rel_bias_lookup(rel_bias, jnp.maximum(d_band, 0))

    def block_fn(args):
        n, q_n, k_n, v_n = args
        valid = in_window & (n * BLOCK + kpos >= BLOCK)
        d_meta = N_META + n * BLOCK + qpos - mpos
        s_meta = (jnp.einsum('bqhgd,bmhd->bhgqm', q_n, k_m).astype(jnp.float32)
                  + rel_bias_lookup(rel_bias, d_meta))
        s_band = jnp.einsum('bqhgd,bjhd->bhgqj', q_n, k_n).astype(jnp.float32) + bias_band
        s_band = jnp.where(valid, s_band, NEG_INF)
        p = sink_softmax(jnp.concatenate([s_meta, s_band], axis=-1), sink_logits).astype(v_n.dtype)
        return (jnp.einsum('bhgqm,bmhd->bqhgd', p[..., :N_META], v_m)
                + jnp.einsum('bhgqj,bjhd->bqhgd', p[..., N_META:], v_n))

    xs = (jnp.arange(n_blk, dtype=jnp.int32), jnp.moveaxis(q_blk, 1, 0),
          jnp.moveaxis(k_band, 1, 0), jnp.moveaxis(v_band, 1, 0))
    o_r = lax.map(block_fn, xs)
    o_r = jnp.moveaxis(o_r, 0, 1).reshape(B, S, N_HEADS * HEAD_DIM)
    o = jnp.concatenate([o_m.reshape(B, N_META, N_HEADS * HEAD_DIM), o_r], axis=1)
    return o @ w_o + b_o


def swiglu(h, w_gate, w_up, w_down):
    return (jax.nn.silu(h @ w_gate) * (h @ w_up)) @ w_down


def setup_inputs(seed: int = 0) -> dict:
    key = jax.random.key(seed)
    ks = list(jax.random.split(key, 24))
    n_conv = len(range(0, DEPTH, N_MIXERS))
    n_attn = len(range(1, DEPTH, N_MIXERS))
    D, F, H = D_MODEL, D_FF, N_HEADS * HEAD_DIM

    def nrm(k, shape, scale):
        return jax.random.normal(k, shape, jnp.float32) * scale

    def gain(k, shape):
        return 1.0 + nrm(k, shape, 0.02)

    return {
        'x': nrm(ks[0], (BATCH, SEQ, D), 1.0),
        'meta_tokens': nrm(ks[1], (N_META, D), 1.0),
        'rel_bias': nrm(ks[2], (N_BUCKETS, N_HEADS), 0.5),
        'conv_w_in': nrm(ks[3], (n_conv, D, 2 * D), D ** -0.5),
        'conv_b_in': nrm(ks[4], (n_conv, 2 * D), 0.02),
        'conv_w_dw': nrm(ks[5], (n_conv, CONV_WIDTH, D), CONV_WIDTH ** -0.5),
        'conv_b_dw': nrm(ks[6], (n_conv, D), 0.02),
        'conv_ln_g': gain(ks[7], (n_conv, D)),
        'conv_ln_b': nrm(ks[8], (n_conv, D), 0.02),
        'conv_w_out': nrm(ks[9], (n_conv, D, D), D ** -0.5),
        'conv_b_out': nrm(ks[10], (n_conv, D), 0.02),
        'attn_w_qkv': nrm(ks[11], (n_attn, D, QKV_WIDTH), D ** -0.5),
        'attn_b_qkv': nrm(ks[12], (n_attn, QKV_WIDTH), 0.02),
        'attn_sinks': nrm(ks[13], (n_attn, N_HEADS), 0.5),
        'attn_w_o': nrm(ks[14], (n_attn, H, D), H ** -0.5),
        'attn_b_o': nrm(ks[15], (n_attn, D), 0.02),
        'norm_mix_pre': gain(ks[16], (DEPTH, D)),
        'norm_mix_post': gain(ks[17], (DEPTH, D)),
        'norm_ffn_pre': gain(ks[18], (DEPTH, D)),
        'norm_ffn_post': gain(ks[19], (DEPTH, D)),
        'ffn_w_gate': nrm(ks[20], (DEPTH, D, F), D ** -0.5),
        'ffn_w_up': nrm(ks[21], (DEPTH, D, F), D ** -0.5),
        'ffn_w_down': nrm(ks[22], (DEPTH, F, D), F ** -0.5),
    }


def reference(x, meta_tokens, rel_bias, conv_w_in, conv_b_in, conv_w_dw, conv_b_dw,
              conv_ln_g, conv_ln_b, conv_w_out, conv_b_out, attn_w_qkv, attn_b_qkv,
              attn_sinks, attn_w_o, attn_b_o, norm_mix_pre, norm_mix_post,
              norm_ffn_pre, norm_ffn_post, ffn_w_gate, ffn_w_up, ffn_w_down):
    B = x.shape[0]
    meta = jnp.broadcast_to(meta_tokens.astype(x.dtype)[None], (B, N_META, D_MODEL))
    h = jnp.concatenate([meta, x], axis=1)
    for i in range(DEPTH):
        j = i // N_MIXERS
        u = rms_norm(h, norm_mix_pre[i])
        if i % N_MIXERS == 0:
            u = conformer_conv(u, conv_w_in[j], conv_b_in[j], conv_w_dw[j], conv_b_dw[j],
                               conv_ln_g[j], conv_ln_b[j], conv_w_out[j], conv_b_out[j])
        else:
            u = sliding_window_sink_attention(u, attn_w_qkv[j], attn_b_qkv[j], attn_sinks[j],
                                              attn_w_o[j], attn_b_o[j], rel_bias)
        h = h + rms_norm(u, norm_mix_post[i])
        u = swiglu(rms_norm(h, norm_ffn_pre[i]), ffn_w_gate[i], ffn_w_up[i], ffn_w_down[i])
        h = h + rms_norm(u, norm_ffn_post[i])
    return h[:, N_META:]
```

```python
import functools
import math

import jax
import jax.numpy as jnp
from jax import lax
from jax.experimental import pallas as pl
from jax.experimental.pallas import tpu as pltpu

N_META = 16
CONV_WIDTH = 31
HEAD_DIM = 64
GROUP = 8
WINDOW = 128
BLOCK = 128
N_BUCKETS = 32
MAX_DISTANCE = 128
RMS_EPS = 1e-6
LN_EPS = 1e-5
NEG_INF = -1e30

V7X_LANES = 128
V7X_BF16_SUBLANE_ROWS = 16
V7X_VMEM_BYTES = 64 * 1024 * 1024
V7X_VMEM_REQUEST_CAP = V7X_VMEM_BYTES - 6 * 1024 * 1024

HALO_ROWS = 2 * V7X_BF16_SUBLANE_ROWS
SLOTS = 3 * V7X_LANES
BF16 = jnp.bfloat16
F32 = jnp.float32


def _vmem_limit(pipelined_bytes, resident_bytes):
    need = 2 * pipelined_bytes + resident_bytes
    return int(min(V7X_VMEM_REQUEST_CAP, need + need // 4 + (2 << 20)))


def _nbytes(shape, dtype):
    return math.prod(shape) * jnp.dtype(dtype).itemsize


def _rms(x, g):
    return x * lax.rsqrt(jnp.mean(x * x, axis=-1, keepdims=True) + RMS_EPS) * g


def _sigmoid(x):
    return 1.0 / (1.0 + jnp.exp(-x))


def _mm(a, b):
    return jnp.dot(a, b, preferred_element_type=F32)


def _conv_in_body(h_ref, gpre_ref, wa_ref, wg_ref, ba_ref, bg_ref, o_ref, xn_ref):
    @pl.when(pl.program_id(1) == 0)
    def _():
        xn_ref[...] = _rms(h_ref[...], gpre_ref[...]).astype(BF16)

    xn = xn_ref[...]
    a = _mm(xn, wa_ref[...]) + ba_ref[...]
    g = _mm(xn, wg_ref[...]) + bg_ref[...]
    o_ref[...] = (a * _sigmoid(g)).astype(o_ref.dtype)


def _conv_in(h, gpre, w_in, b_in, *, tm, tn):
    rows, d = h.shape
    nj = d // tn
    pipelined = (_nbytes((tm, d), F32) + 2 * _nbytes((d, tn), BF16) + _nbytes((tm, tn), BF16))
    resident = _nbytes((tm, d), BF16) + 3 * _nbytes((tm, tn), F32)
    return pl.pallas_call(
        _conv_in_body,
        grid=(rows // tm, nj),
        in_specs=[
            pl.BlockSpec((tm, d), lambda i, j: (i, 0)),
            pl.BlockSpec((1, d), lambda i, j: (0, 0)),
            pl.BlockSpec((d, tn), lambda i, j: (0, j)),
            pl.BlockSpec((d, tn), lambda i, j: (0, j + nj)),
            pl.BlockSpec((1, tn), lambda i, j: (0, j)),
            pl.BlockSpec((1, tn), lambda i, j: (0, j + nj)),
        ],
        out_specs=pl.BlockSpec((tm, tn), lambda i, j: (i, j)),
        out_shape=jax.ShapeDtypeStruct((rows, d), BF16),
        scratch_shapes=[pltpu.VMEM((tm, d), BF16)],
        compiler_params=pltpu.CompilerParams(
            dimension_semantics=("parallel", "arbitrary"),
            vmem_limit_bytes=_vmem_limit(pipelined, resident)),
        name="conv_in",
    )(h, gpre, w_in, w_in, b_in, b_in)


def _conv_out_body(v_ref, vprev_ref, halo0_ref, wdw_ref, bdw_ref, lng_ref, lnb_ref, wout_ref,
                   bout_ref, gpost_ref, h_ref, o_ref, vbuf_ref, u_ref, *, tm, blocks_per_seq, cw):
    first = pl.program_id(0) % blocks_per_seq == 0

    @pl.when(first)
    def _():
        vbuf_ref[0:HALO_ROWS, :] = halo0_ref[...].astype(F32)

    @pl.when(jnp.logical_not(first))
    def _():
        vbuf_ref[0:HALO_ROWS, :] = vprev_ref[...].astype(F32)

    vbuf_ref[HALO_ROWS:HALO_ROWS + tm, :] = v_ref[...].astype(F32)

    base = HALO_ROWS - (CONV_WIDTH - 1)

    def chunk(c, carry):
        ls = pl.ds(pl.multiple_of(c * cw, cw), cw)
        acc = jnp.broadcast_to(bdw_ref[:, ls], (tm, cw))
        for k in range(CONV_WIDTH):
            acc = acc + wdw_ref[k:k + 1, ls] * vbuf_ref[pl.ds(base + k, tm), ls]
        u_ref[:, ls] = acc
        return carry

    lax.fori_loop(0, u_ref.shape[1] // cw, chunk, 0)

    u = u_ref[...]
    mu = jnp.mean(u, axis=-1, keepdims=True)
    uc = u - mu
    var = jnp.mean(uc * uc, axis=-1, keepdims=True)
    y = uc * lax.rsqrt(var + LN_EPS) * lng_ref[...] + lnb_ref[...]
    y = y * _sigmoid(y)
    z = _mm(y.astype(BF16), wout_ref[...]) + bout_ref[...]
    o_ref[...] = h_ref[...] + _rms(z, gpost_ref[...])


def _conv_out(v, vprev_src, halo0, w_dw, b_dw, ln_g, ln_b, w_out, b_out, gpost, h, *, tm,
              rows_per_seq, cw=V7X_LANES):
    rows, d = h.shape
    body = functools.partial(_conv_out_body, tm=tm, blocks_per_seq=rows_per_seq // tm, cw=cw)
    halo_blocks_per_tile = tm // HALO_ROWS
    vec = lambda: pl.BlockSpec((1, d), lambda i: (0, 0))
    pipelined = (_nbytes((tm, d), BF16) + 2 * _nbytes((HALO_ROWS, d), BF16) + _nbytes((d, d), BF16)
                 + _nbytes((CONV_WIDTH, d), F32) + 2 * _nbytes((tm, d), F32))
    resident = _nbytes((tm + HALO_ROWS, d), F32) + 4 * _nbytes((tm, d), F32)
    return pl.pallas_call(
        body,
        grid=(rows // tm,),
        in_specs=[
            pl.BlockSpec((tm, d), lambda i: (i, 0)),
            pl.BlockSpec((HALO_ROWS, d), lambda i: (jnp.maximum(i * halo_blocks_per_tile - 1, 0), 0)),
            pl.BlockSpec((HALO_ROWS, d), lambda i: (0, 0)),
            pl.BlockSpec((CONV_WIDTH, d), lambda i: (0, 0)),
            vec(), vec(), vec(),
            pl.BlockSpec((d, d), lambda i: (0, 0)),
            vec(), vec(),
            pl.BlockSpec((tm, d), lambda i: (i, 0)),
        ],
        out_specs=pl.BlockSpec((tm, d), lambda i: (i, 0)),
        out_shape=jax.ShapeDtypeStruct((rows, d), F32),
        scratch_shapes=[pltpu.VMEM((tm + HALO_ROWS, d), F32), pltpu.VMEM((tm, d), F32)],
        compiler_params=pltpu.CompilerParams(
            dimension_semantics=("parallel",),
            vmem_limit_bytes=_vmem_limit(pipelined, resident)),
        name="conv_out",
    )(v, vprev_src, halo0, w_dw, b_dw, ln_g, ln_b, w_out, b_out, gpost, h)


def _ffn_body(h_ref, gpre_ref, wg_ref, wu_ref, wd_ref, gpost_ref, o_ref, xn_ref, acc_ref):
    f = pl.program_id(1)

    @pl.when(f == 0)
    def _():
        xn_ref[...] = _rms(h_ref[...], gpre_ref[...]).astype(BF16)

    xn = xn_ref[...]
    g = _mm(xn, wg_ref[...])
    u = _mm(xn, wu_ref[...])
    a = (g * _sigmoid(g) * u).astype(BF16)
    part = _mm(a, wd_ref[...])

    @pl.when(f == 0)
    def _():
        acc_ref[...] = part

    @pl.when(f > 0)
    def _():
        acc_ref[...] += part

    @pl.when(f == pl.num_programs(1) - 1)
    def _():
        o_ref[...] = h_ref[...] + _rms(acc_ref[...], gpost_ref[...])


def _ffn(h, gpre, w_gate, w_up, w_down, gpost, *, tm, tf):
    rows, d = h.shape
    ff = w_gate.shape[1]
    pipelined = 2 * _nbytes((tm, d), F32) + 3 * _nbytes((d, tf), BF16)
    resident = (_nbytes((tm, d), BF16) + 2 * _nbytes((tm, d), F32) + 3 * _nbytes((tm, tf), F32))
    return pl.pallas_call(
        _ffn_body,
        grid=(rows // tm, ff // tf),
        in_specs=[
            pl.BlockSpec((tm, d), lambda i, f: (i, 0)),
            pl.BlockSpec((1, d), lambda i, f: (0, 0)),
            pl.BlockSpec((d, tf), lambda i, f: (0, f)),
            pl.BlockSpec((d, tf), lambda i, f: (0, f)),
            pl.BlockSpec((tf, d), lambda i, f: (f, 0)),
            pl.BlockSpec((1, d), lambda i, f: (0, 0)),
        ],
        out_specs=pl.BlockSpec((tm, d), lambda i, f: (i, 0)),
        out_shape=jax.ShapeDtypeStruct((rows, d), F32),
        scratch_shapes=[pltpu.VMEM((tm, d), BF16), pltpu.VMEM((tm, d), F32)],
        compiler_params=pltpu.CompilerParams(
            dimension_semantics=("parallel", "arbitrary"),
            vmem_limit_bytes=_vmem_limit(pipelined, resident)),
        name="ffn",
    )(h, gpre, w_gate, w_up, w_down, gpost)


def _qkv_body(h_ref, gpre_ref, w_ref, b_ref, q_ref, k_ref, v_ref, xn_ref, *, n_q_tiles, n_kv):
    j = pl.program_id(1)

    @pl.when(j == 0)
    def _():
        xn_ref[...] = _rms(h_ref[...], gpre_ref[...]).astype(BF16)

    y = _mm(xn_ref[...], w_ref[...]) + b_ref[...]

    @pl.when(j < n_q_tiles)
    def _():
        q_ref[...] = (y * (HEAD_DIM ** -0.5)).astype(BF16)

    @pl.when(j == n_q_tiles)
    def _():
        for hh in range(n_kv):
            k_ref[hh] = y[:, hh * HEAD_DIM:(hh + 1) * HEAD_DIM].astype(BF16)
            v_ref[hh] = y[:, (n_kv + hh) * HEAD_DIM:(n_kv + hh + 1) * HEAD_DIM].astype(BF16)


def _qkv(h, gpre, w_qkv, b_qkv, *, tm, n_kv):
    rows, d = h.shape
    tn = 2 * n_kv * HEAD_DIM
    n_q_tiles = d // tn
    body = functools.partial(_qkv_body, n_q_tiles=n_q_tiles, n_kv=n_kv)
    kv_shape = jax.ShapeDtypeStruct((n_kv, rows, HEAD_DIM), BF16)
    pipelined = (_nbytes((tm, d), F32) + _nbytes((d, tn), BF16) + _nbytes((tm, tn), BF16)
                 + 2 * _nbytes((n_kv, tm, V7X_LANES), BF16))
    resident = _nbytes((tm, d), BF16) + 2 * _nbytes((tm, tn), F32)
    return pl.pallas_call(
        body,
        grid=(rows // tm, n_q_tiles + 1),
        in_specs=[
            pl.BlockSpec((tm, d), lambda i, j: (i, 0)),
            pl.BlockSpec((1, d), lambda i, j: (0, 0)),
            pl.BlockSpec((d, tn), lambda i, j: (0, j)),
            pl.BlockSpec((1, tn), lambda i, j: (0, j)),
        ],
        out_specs=[
            pl.BlockSpec((tm, tn), lambda i, j: (i, jnp.minimum(j, n_q_tiles - 1))),
            pl.BlockSpec((n_kv, tm, HEAD_DIM), lambda i, j: (0, i, 0)),
            pl.BlockSpec((n_kv, tm, HEAD_DIM), lambda i, j: (0, i, 0)),
        ],
        out_shape=[jax.ShapeDtypeStruct((rows, d), BF16), kv_shape, kv_shape],
        scratch_shapes=[pltpu.VMEM((tm, d), BF16)],
        compiler_params=pltpu.CompilerParams(
            dimension_semantics=("parallel", "arbitrary"),
            vmem_limit_bytes=_vmem_limit(pipelined, resident)),
        name="qkv",
    )(h, gpre, w_qkv, b_qkv)


def _attn_body(q_ref, kp_ref, kc_ref, vp_ref, vc_ref, km_ref, vm_ref, tbl_ref, o_ref):
    q = q_ref[...]
    qs = jnp.concatenate([q[:, g * HEAD_DIM:(g + 1) * HEAD_DIM] for g in range(GROUP)], axis=0)
    zpad = jnp.zeros((SLOTS - 2 * BLOCK - N_META, HEAD_DIM), BF16)
    kf = jnp.concatenate([kp_ref[0], kc_ref[0], km_ref[0], zpad], axis=0)
    vf = jnp.concatenate([vp_ref[0], vc_ref[0], vm_ref[0], zpad], axis=0)
    s = lax.dot_general(qs, kf, (((1,), (1,)), ((), ())), preferred_element_type=F32)
    s = s + tbl_ref[0, 0]
    m = jnp.max(s, axis=-1, keepdims=True)
    p = jnp.exp(s - m)
    l = jnp.sum(p, axis=-1, keepdims=True)
    o = _mm(p.astype(BF16), vf) / l
    o_ref[...] = jnp.concatenate(
        [o[g * BLOCK:(g + 1) * BLOCK] for g in range(GROUP)], axis=1).astype(o_ref.dtype)


def _attention(q, k, v, k_meta, v_meta, tbl, *, batch, seq):
    rows, d = q.shape
    n_kv = k.shape[0]
    n_blk = seq // BLOCK
    qw = GROUP * HEAD_DIM
    row_blk = lambda h, b, n: b * n_blk + n
    prev_blk = lambda h, b, n: b * n_blk + jnp.maximum(n - 1, 0)
    kv_spec = lambda fn: pl.BlockSpec((1, BLOCK, HEAD_DIM), lambda h, b, n: (h, fn(h, b, n), 0))
    meta_spec = pl.BlockSpec((1, N_META, HEAD_DIM), lambda h, b, n: (h, 0, 0))
    pipelined = (2 * _nbytes((BLOCK, qw), BF16) + 4 * _nbytes((BLOCK, V7X_LANES), BF16)
                 + _nbytes((GROUP * BLOCK, SLOTS), F32))
    resident = 4 * _nbytes((GROUP * BLOCK, SLOTS), F32)
    return pl.pallas_call(
        _attn_body,
        grid=(n_kv, batch, n_blk),
        in_specs=[
            pl.BlockSpec((BLOCK, qw), lambda h, b, n: (row_blk(h, b, n), h)),
            kv_spec(prev_blk), kv_spec(row_blk), kv_spec(prev_blk), kv_spec(row_blk),
            meta_spec, meta_spec,
            pl.BlockSpec((1, 1, GROUP * BLOCK, SLOTS), lambda h, b, n: (jnp.minimum(n, 1), h, 0, 0)),
        ],
        out_specs=pl.BlockSpec((BLOCK, qw), lambda h, b, n: (row_blk(h, b, n), h)),
        out_shape=jax.ShapeDtypeStruct((rows, d), BF16),
        compiler_params=pltpu.CompilerParams(
            dimension_semantics=("parallel", "parallel", "parallel"),
            vmem_limit_bytes=_vmem_limit(pipelined, resident)),
        name="swa_attention",
    )(q, k, k, v, v, k_meta, v_meta, tbl)


def _oproj_body(a_ref, w_ref, b_ref, gpost_ref, h_ref, o_ref):
    z = _mm(a_ref[...], w_ref[...]) + b_ref[...]
    o_ref[...] = h_ref[...] + _rms(z, gpost_ref[...])


def _oproj(a, w_o, b_o, gpost, h, *, tm):
    rows, d = h.shape
    vec = lambda: pl.BlockSpec((1, d), lambda i: (0, 0))
    pipelined = _nbytes((tm, d), BF16) + _nbytes((d, d), BF16) + 2 * _nbytes((tm, d), F32)
    resident = 3 * _nbytes((tm, d), F32)
    return pl.pallas_call(
        _oproj_body,
        grid=(rows // tm,),
        in_specs=[
            pl.BlockSpec((tm, d), lambda i: (i, 0)),
            pl.BlockSpec((d, d), lambda i: (0, 0)),
            vec(), vec(),
            pl.BlockSpec((tm, d), lambda i: (i, 0)),
        ],
        out_specs=pl.BlockSpec((tm, d), lambda i: (i, 0)),
        out_shape=jax.ShapeDtypeStruct((rows, d), F32),
        compiler_params=pltpu.CompilerParams(
            dimension_semantics=("parallel",),
            vmem_limit_bytes=_vmem_limit(pipelined, resident)),
        name="attn_out_proj",
    )(a, w_o, b_o, gpost, h)


def _t5_bucket(dist):
    max_exact = N_BUCKETS // 2
    dd = jnp.maximum(dist, max_exact).astype(F32)
    large = max_exact + (jnp.log(dd / max_exact) / math.log(MAX_DISTANCE / max_exact)
                         * (N_BUCKETS - max_exact)).astype(jnp.int32)
    return jnp.where(dist < max_exact, dist, jnp.minimum(large, N_BUCKETS - 1))


def _slot_table(rel_bias, sinks, n_kv):
    n_heads = rel_bias.shape[1]
    rb = rel_bias.astype(F32)
    qpos = jnp.arange(BLOCK)[:, None]
    kpos = jnp.arange(2 * BLOCK)[None, :]
    mpos = jnp.arange(N_META)[None, :]
    d_band = BLOCK + qpos - kpos
    in_window = (d_band >= 0) & (d_band < WINDOW)
    bias_band = rb[_t5_bucket(jnp.maximum(d_band, 0))]
    sink = jnp.broadcast_to(sinks.astype(F32)[None, None, :], (BLOCK, 1, n_heads))
    pad = jnp.full((BLOCK, SLOTS - 2 * BLOCK - N_META - 1, n_heads), NEG_INF, F32)
    tables = []
    for n in (0, 1):
        valid = in_window & (n * BLOCK + kpos >= BLOCK)
        band = jnp.where(valid[:, :, None], bias_band, NEG_INF)
        meta = rb[_t5_bucket(N_META + n * BLOCK + qpos - mpos)]
        tables.append(jnp.concatenate([band, meta, sink, pad], axis=1))
    tbl = jnp.transpose(jnp.stack(tables), (0, 3, 1, 2))
    return tbl.reshape(2, n_kv, GROUP * BLOCK, SLOTS)


def kernel(x, meta_tokens, rel_bias, conv_w_in, conv_b_in, conv_w_dw, conv_b_dw, conv_ln_g, conv_ln_b,
           conv_w_out, conv_b_out, attn_w_qkv, attn_b_qkv, attn_sinks, attn_w_o, attn_b_o,
           norm_mix_pre, norm_mix_post, norm_ffn_pre, norm_ffn_post, ffn_w_gate, ffn_w_up, ffn_w_down):
    batch, seq, d = x.shape
    n_kv = (attn_w_qkv.shape[2] - d) // (2 * HEAD_DIM)
    rows = batch * seq
    row = lambda a: a.reshape(1, -1).astype(F32)
    bf = lambda a: a.astype(BF16)

    tm_mm, tn_mm, tm_conv, tm_ffn, tf = 512, 512, 256, 512, 512
    tm_meta = N_META

    h0 = x.reshape(rows, d)
    hm0 = meta_tokens.astype(x.dtype)

    w_in, b_in = bf(conv_w_in[0]), row(conv_b_in[0])
    conv_args = (conv_w_dw[0].astype(F32), row(conv_b_dw[0]), row(conv_ln_g[0]), row(conv_ln_b[0]),
                 bf(conv_w_out[0]), row(conv_b_out[0]), row(norm_mix_post[0]))
    ffn0 = (row(norm_ffn_pre[0]), bf(ffn_w_gate[0]), bf(ffn_w_up[0]), bf(ffn_w_down[0]),
            row(norm_ffn_post[0]))
    gpre0 = row(norm_mix_pre[0])

    vm = _conv_in(hm0, gpre0, w_in, b_in, tm=tm_meta, tn=tn_mm)
    zero_halo = jnp.zeros((HALO_ROWS, d), BF16)
    hm1 = _conv_out(vm, zero_halo, zero_halo, *conv_args, hm0, tm=tm_meta, rows_per_seq=N_META)
    hm2 = _ffn(hm1, *ffn0, tm=tm_meta, tf=tf)

    v = _conv_in(h0, gpre0, w_in, b_in, tm=tm_mm, tn=tn_mm)
    halo0 = jnp.concatenate([jnp.zeros((HALO_ROWS - N_META, d), BF16), vm], axis=0)
    h1 = _conv_out(v, v, halo0, *conv_args, h0, tm=tm_conv, rows_per_seq=seq)
    h2 = _ffn(h1, *ffn0, tm=tm_ffn, tf=tf)

    w_qkv, b_qkv = bf(attn_w_qkv[0]), row(attn_b_qkv[0])
    gpre1 = row(norm_mix_pre[1])
    _, k_meta, v_meta = _qkv(hm2, gpre1, w_qkv, b_qkv, tm=tm_meta, n_kv=n_kv)
    q, k, vv = _qkv(h2, gpre1, w_qkv, b_qkv, tm=tm_mm, n_kv=n_kv)
    tbl = _slot_table(rel_bias, attn_sinks[0], n_kv)
    a = _attention(q, k, vv, k_meta, v_meta, tbl, batch=batch, seq=seq)
    h3 = _oproj(a, bf(attn_w_o[0]), row(attn_b_o[0]), row(norm_mix_post[1]), h2, tm=tm_mm)
    h4 = _ffn(h3, row(norm_ffn_pre[1]), bf(ffn_w_gate[1]), bf(ffn_w_up[1]), bf(ffn_w_down[1]),
              row(norm_ffn_post[1]), tm=tm_ffn, tf=tf)
    return h4.reshape(batch, seq, d)
```

```python
import functools
import math

import jax
import jax.numpy as jnp
from jax import lax
from jax.experimental import pallas as pl
from jax.experimental.pallas import tpu as pltpu

N_META = 16
CONV_WIDTH = 31
HEAD_DIM = 64
GROUP = 8
WINDOW = 128
BLOCK = 128
N_BUCKETS = 32
MAX_DISTANCE = 128
RMS_EPS = 1e-6
LN_EPS = 1e-5
NEG_INF = -1e30

V7X_LANES = 128
V7X_SUBLANES = 8
V7X_BF16_SUBLANE_ROWS = 16
CONV_ROW_CHUNK = 128
V7X_VMEM_BYTES = 64 * 1024 * 1024
V7X_VMEM_REQUEST_CAP = V7X_VMEM_BYTES - 6 * 1024 * 1024

HALO_ROWS = 2 * V7X_BF16_SUBLANE_ROWS
SLOTS = 3 * V7X_LANES
BF16 = jnp.bfloat16
F32 = jnp.float32


def _vmem_limit(pipelined_bytes, resident_bytes):
    need = 2 * pipelined_bytes + resident_bytes
    return int(min(V7X_VMEM_REQUEST_CAP, need + need // 4 + (2 << 20)))


def _nbytes(shape, dtype):
    return math.prod(shape) * jnp.dtype(dtype).itemsize


def _rms(x, g):
    return x * lax.rsqrt(jnp.mean(x * x, axis=-1, keepdims=True) + RMS_EPS) * g


def _sigmoid(x):
    return 1.0 / (1.0 + jnp.exp(-x))


def _mm(a, b):
    return jnp.dot(a, b, preferred_element_type=F32)


def _conv_in_body(h_ref, gpre_ref, wa_ref, wg_ref, ba_ref, bg_ref, o_ref, xn_ref):
    @pl.when(pl.program_id(1) == 0)
    def _():
        xn_ref[...] = _rms(h_ref[...], gpre_ref[...]).astype(BF16)

    xn = xn_ref[...]
    a = _mm(xn, wa_ref[...]) + ba_ref[...]
    g = _mm(xn, wg_ref[...]) + bg_ref[...]
    o_ref[...] = (a * _sigmoid(g)).astype(o_ref.dtype)


def _conv_in(h, gpre, w_in, b_in, *, tm, tn):
    rows, d = h.shape
    nj = d // tn
    pipelined = (_nbytes((tm, d), F32) + 2 * _nbytes((d, tn), BF16) + _nbytes((tm, tn), BF16))
    resident = _nbytes((tm, d), BF16) + 3 * _nbytes((tm, tn), F32)
    return pl.pallas_call(
        _conv_in_body,
        grid=(rows // tm, nj),
        in_specs=[
            pl.BlockSpec((tm, d), lambda i, j: (i, 0)),
            pl.BlockSpec((1, d), lambda i, j: (0, 0)),
            pl.BlockSpec((d, tn), lambda i, j: (0, j)),
            pl.BlockSpec((d, tn), lambda i, j: (0, j + nj)),
            pl.BlockSpec((1, tn), lambda i, j: (0, j)),
            pl.BlockSpec((1, tn), lambda i, j: (0, j + nj)),
        ],
        out_specs=pl.BlockSpec((tm, tn), lambda i, j: (i, j)),
        out_shape=jax.ShapeDtypeStruct((rows, d), BF16),
        scratch_shapes=[pltpu.VMEM((tm, d), BF16)],
        compiler_params=pltpu.CompilerParams(
            dimension_semantics=("parallel", "arbitrary"),
            vmem_limit_bytes=_vmem_limit(pipelined, resident)),
        name="conv_in",
    )(h, gpre, w_in, w_in, b_in, b_in)


def _conv_out_body(v_ref, vprev_ref, halo0_ref, wdw_ref, bdw_ref, lng_ref, lnb_ref, wout_ref,
                   bout_ref, gpost_ref, h_ref, o_ref, vbuf_ref, u_ref, *, tm, rt, blocks_per_seq, cw):
    first = pl.program_id(0) % blocks_per_seq == 0

    @pl.when(first)
    def _():
        vbuf_ref[0:HALO_ROWS, :] = halo0_ref[...].astype(F32)

    @pl.when(jnp.logical_not(first))
    def _():
        vbuf_ref[0:HALO_ROWS, :] = vprev_ref[...].astype(F32)

    vbuf_ref[HALO_ROWS:HALO_ROWS + tm, :] = v_ref[...].astype(F32)

    n_a = -(-CONV_WIDTH // V7X_SUBLANES)

    def chunk(c, carry):
        ls = pl.ds(pl.multiple_of(c * cw, cw), cw)
        for r0 in range(0, tm, rt):
            x = vbuf_ref[r0:r0 + HALO_ROWS + rt, ls]
            acc = jnp.broadcast_to(bdw_ref[:, ls], (rt, cw))
            for b in range(V7X_SUBLANES):
                y = pltpu.roll(x, b, axis=0) if b else x
                for a in range(n_a):
                    m = V7X_SUBLANES * a + b
                    if m < CONV_WIDTH:
                        lo = HALO_ROWS - V7X_SUBLANES * a
                        acc = acc + wdw_ref[CONV_WIDTH - 1 - m:CONV_WIDTH - m, ls] * y[lo:lo + rt]
            u_ref[r0:r0 + rt, ls] = acc
        return carry

    lax.fori_loop(0, u_ref.shape[1] // cw, chunk, 0)

    u = u_ref[...]
    mu = jnp.mean(u, axis=-1, keepdims=True)
    uc = u - mu
    var = jnp.mean(uc * uc, axis=-1, keepdims=True)
    y = uc * lax.rsqrt(var + LN_EPS) * lng_ref[...] + lnb_ref[...]
    y = y * _sigmoid(y)
    z = _mm(y.astype(BF16), wout_ref[...]) + bout_ref[...]
    o_ref[...] = h_ref[...] + _rms(z, gpost_ref[...])


def _conv_out(v, vprev_src, halo0, w_dw, b_dw, ln_g, ln_b, w_out, b_out, gpost, h, *, tm,
              rows_per_seq, cw=V7X_LANES):
    rows, d = h.shape
    body = functools.partial(_conv_out_body, tm=tm, rt=min(tm, CONV_ROW_CHUNK),
                             blocks_per_seq=rows_per_seq // tm, cw=cw)
    halo_blocks_per_tile = tm // HALO_ROWS
    vec = lambda: pl.BlockSpec((1, d), lambda i: (0, 0))
    pipelined = (_nbytes((tm, d), BF16) + 2 * _nbytes((HALO_ROWS, d), BF16) + _nbytes((d, d), BF16)
                 + _nbytes((CONV_WIDTH, d), F32) + 2 * _nbytes((tm, d), F32))
    resident = _nbytes((tm + HALO_ROWS, d), F32) + 4 * _nbytes((tm, d), F32)
    return pl.pallas_call(
        body,
        grid=(rows // tm,),
        in_specs=[
            pl.BlockSpec((tm, d), lambda i: (i, 0)),
            pl.BlockSpec((HALO_ROWS, d), lambda i: (jnp.maximum(i * halo_blocks_per_tile - 1, 0), 0)),
            pl.BlockSpec((HALO_ROWS, d), lambda i: (0, 0)),
            pl.BlockSpec((CONV_WIDTH, d), lambda i: (0, 0)),
            vec(), vec(), vec(),
            pl.BlockSpec((d, d), lambda i: (0, 0)),
            vec(), vec(),
            pl.BlockSpec((tm, d), lambda i: (i, 0)),
        ],
        out_specs=pl.BlockSpec((tm, d), lambda i: (i, 0)),
        out_shape=jax.ShapeDtypeStruct((rows, d), F32),
        scratch_shapes=[pltpu.VMEM((tm + HALO_ROWS, d), F32), pltpu.VMEM((tm, d), F32)],
        compiler_params=pltpu.CompilerParams(
            dimension_semantics=("parallel",),
            vmem_limit_bytes=_vmem_limit(pipelined, resident)),
        name="conv_out",
    )(v, vprev_src, halo0, w_dw, b_dw, ln_g, ln_b, w_out, b_out, gpost, h)


def _ffn_body(h_ref, gpre_ref, wg_ref, wu_ref, wd_ref, gpost_ref, o_ref, xn_ref, acc_ref):
    f = pl.program_id(1)

    @pl.when(f == 0)
    def _():
        xn_ref[...] = _rms(h_ref[...], gpre_ref[...]).astype(BF16)
        acc_ref[...] = jnp.zeros_like(acc_ref)

    xn = xn_ref[...]
    g = _mm(xn, wg_ref[...])
    u = _mm(xn, wu_ref[...])
    a = (g * _sigmoid(g) * u).astype(BF16)
    acc_ref[...] += _mm(a, wd_ref[...])

    @pl.when(f == pl.num_programs(1) - 1)
    def _():
        o_ref[...] = h_ref[...] + _rms(acc_ref[...], gpost_ref[...])


def _ffn(h, gpre, w_gate, w_up, w_down, gpost, *, tm, tf):
    rows, d = h.shape
    ff = w_gate.shape[1]
    pipelined = 2 * _nbytes((tm, d), F32) + 3 * _nbytes((d, tf), BF16)
    resident = (_nbytes((tm, d), BF16) + 2 * _nbytes((tm, d), F32) + 3 * _nbytes((tm, tf), F32))
    return pl.pallas_call(
        _ffn_body,
        grid=(rows // tm, ff // tf),
        in_specs=[
            pl.BlockSpec((tm, d), lambda i, f: (i, 0)),
            pl.BlockSpec((1, d), lambda i, f: (0, 0)),
            pl.BlockSpec((d, tf), lambda i, f: (0, f)),
            pl.BlockSpec((d, tf), lambda i, f: (0, f)),
            pl.BlockSpec((tf, d), lambda i, f: (f, 0)),
            pl.BlockSpec((1, d), lambda i, f: (0, 0)),
        ],
        out_specs=pl.BlockSpec((tm, d), lambda i, f: (i, 0)),
        out_shape=jax.ShapeDtypeStruct((rows, d), F32),
        scratch_shapes=[pltpu.VMEM((tm, d), BF16), pltpu.VMEM((tm, d), F32)],
        compiler_params=pltpu.CompilerParams(
            dimension_semantics=("parallel", "arbitrary"),
            vmem_limit_bytes=_vmem_limit(pipelined, resident)),
        name="ffn",
    )(h, gpre, w_gate, w_up, w_down, gpost)


def _qkv_body(h_ref, gpre_ref, w_ref, b_ref, q_ref, k_ref, v_ref, xn_ref, *, n_q_tiles, n_kv):
    j = pl.program_id(1)

    @pl.when(j == 0)
    def _():
        xn_ref[...] = _rms(h_ref[...], gpre_ref[...]).astype(BF16)

    y = _mm(xn_ref[...], w_ref[...]) + b_ref[...]

    @pl.when(j < n_q_tiles)
    def _():
        q_ref[...] = (y * (HEAD_DIM ** -0.5)).astype(BF16)

    @pl.when(j == n_q_tiles)
    def _():
        for hh in range(n_kv):
            k_ref[hh] = y[:, hh * HEAD_DIM:(hh + 1) * HEAD_DIM].astype(BF16)
            v_ref[hh] = y[:, (n_kv + hh) * HEAD_DIM:(n_kv + hh + 1) * HEAD_DIM].astype(BF16)


def _qkv(h, gpre, w_qkv, b_qkv, *, tm, n_kv):
    rows, d = h.shape
    tn = 2 * n_kv * HEAD_DIM
    n_q_tiles = d // tn
    body = functools.partial(_qkv_body, n_q_tiles=n_q_tiles, n_kv=n_kv)
    kv_shape = jax.ShapeDtypeStruct((n_kv, rows, HEAD_DIM), BF16)
    pipelined = (_nbytes((tm, d), F32) + _nbytes((d, tn), BF16) + _nbytes((tm, tn), BF16)
                 + 2 * _nbytes((n_kv, tm, V7X_LANES), BF16))
    resident = _nbytes((tm, d), BF16) + 2 * _nbytes((tm, tn), F32)
    return pl.pallas_call(
        body,
        grid=(rows // tm, n_q_tiles + 1),
        in_specs=[
            pl.BlockSpec((tm, d), lambda i, j: (i, 0)),
            pl.BlockSpec((1, d), lambda i, j: (0, 0)),
            pl.BlockSpec((d, tn), lambda i, j: (0, j)),
            pl.BlockSpec((1, tn), lambda i, j: (0, j)),
        ],
        out_specs=[
            pl.BlockSpec((tm, tn), lambda i, j: (i, jnp.minimum(j, n_q_tiles - 1))),
            pl.BlockSpec((n_kv, tm, HEAD_DIM), lambda i, j: (0, i, 0)),
            pl.BlockSpec((n_kv, tm, HEAD_DIM), lambda i, j: (0, i, 0)),
        ],
        out_shape=[jax.ShapeDtypeStruct((rows, d), BF16), kv_shape, kv_shape],
        scratch_shapes=[pltpu.VMEM((tm, d), BF16)],
        compiler_params=pltpu.CompilerParams(
            dimension_semantics=("parallel", "arbitrary"),
            vmem_limit_bytes=_vmem_limit(pipelined, resident)),
        name="qkv",
    )(h, gpre, w_qkv, b_qkv)


def _attn_body(q_ref, kp_ref, kc_ref, vp_ref, vc_ref, km_ref, vm_ref, tbl_ref, o_ref):
    q = q_ref[...]
    qs = jnp.concatenate([q[:, g * HEAD_DIM:(g + 1) * HEAD_DIM] for g in range(GROUP)], axis=0)
    zpad = jnp.zeros((SLOTS - 2 * BLOCK - N_META, HEAD_DIM), BF16)
    kf = jnp.concatenate([kp_ref[0], kc_ref[0], km_ref[0], zpad], axis=0)
    vf = jnp.concatenate([vp_ref[0], vc_ref[0], vm_ref[0], zpad], axis=0)
    s = lax.dot_general(qs, kf, (((1,), (1,)), ((), ())), preferred_element_type=F32)
    s = s + tbl_ref[0, 0]
    m = jnp.max(s, axis=-1, keepdims=True)
    p = jnp.exp(s - m)
    l = jnp.sum(p, axis=-1, keepdims=True)
    o = _mm(p.astype(BF16), vf) / l
    o_ref[...] = jnp.concatenate(
        [o[g * BLOCK:(g + 1) * BLOCK] for g in range(GROUP)], axis=1).astype(o_ref.dtype)


def _attention(q, k, v, k_meta, v_meta, tbl, *, batch, seq):
    rows, d = q.shape
    n_kv = k.shape[0]
    n_blk = seq // BLOCK
    qw = GROUP * HEAD_DIM
    row_blk = lambda h, b, n: b * n_blk + n
    prev_blk = lambda h, b, n: b * n_blk + jnp.maximum(n - 1, 0)
    kv_spec = lambda fn: pl.BlockSpec((1, BLOCK, HEAD_DIM), lambda h, b, n: (h, fn(h, b, n), 0))
    meta_spec = pl.BlockSpec((1, N_META, HEAD_DIM), lambda h, b, n: (h, 0, 0))
    pipelined = (2 * _nbytes((BLOCK, qw), BF16) + 4 * _nbytes((BLOCK, V7X_LANES), BF16)
                 + _nbytes((GROUP * BLOCK, SLOTS), F32))
    resident = 4 * _nbytes((GROUP * BLOCK, SLOTS), F32)
    return pl.pallas_call(
        _attn_body,
        grid=(n_kv, batch, n_blk),
        in_specs=[
            pl.BlockSpec((BLOCK, qw), lambda h, b, n: (row_blk(h, b, n), h)),
            kv_spec(prev_blk), kv_spec(row_blk), kv_spec(prev_blk), kv_spec(row_blk),
            meta_spec, meta_spec,
            pl.BlockSpec((1, 1, GROUP * BLOCK, SLOTS), lambda h, b, n: (jnp.minimum(n, 1), h, 0, 0)),
        ],
        out_specs=pl.BlockSpec((BLOCK, qw), lambda h, b, n: (row_blk(h, b, n), h)),
        out_shape=jax.ShapeDtypeStruct((rows, d), BF16),
        compiler_params=pltpu.CompilerParams(
            dimension_semantics=("parallel", "parallel", "parallel"),
            vmem_limit_bytes=_vmem_limit(pipelined, resident)),
        name="swa_attention",
    )(q, k, k, v, v, k_meta, v_meta, tbl)


def _oproj_body(a_ref, w_ref, b_ref, gpost_ref, h_ref, o_ref):
    z = _mm(a_ref[...], w_ref[...]) + b_ref[...]
    o_ref[...] = h_ref[...] + _rms(z, gpost_ref[...])


def _oproj(a, w_o, b_o, gpost, h, *, tm):
    rows, d = h.shape
    vec = lambda: pl.BlockSpec((1, d), lambda i: (0, 0))
    pipelined = _nbytes((tm, d), BF16) + _nbytes((d, d), BF16) + 2 * _nbytes((tm, d), F32)
    resident = 3 * _nbytes((tm, d), F32)
    return pl.pallas_call(
        _oproj_body,
        grid=(rows // tm,),
        in_specs=[
            pl.BlockSpec((tm, d), lambda i: (i, 0)),
            pl.BlockSpec((d, d), lambda i: (0, 0)),
            vec(), vec(),
            pl.BlockSpec((tm, d), lambda i: (i, 0)),
        ],
        out_specs=pl.BlockSpec((tm, d), lambda i: (i, 0)),
        out_shape=jax.ShapeDtypeStruct((rows, d), F32),
        compiler_params=pltpu.CompilerParams(
            dimension_semantics=("parallel",),
            vmem_limit_bytes=_vmem_limit(pipelined, resident)),
        name="attn_out_proj",
    )(a, w_o, b_o, gpost, h)


def _t5_bucket(dist):
    max_exact = N_BUCKETS // 2
    dd = jnp.maximum(dist, max_exact).astype(F32)
    large = max_exact + (jnp.log(dd / max_exact) / math.log(MAX_DISTANCE / max_exact)
                         * (N_BUCKETS - max_exact)).astype(jnp.int32)
    return jnp.where(dist < max_exact, dist, jnp.minimum(large, N_BUCKETS - 1))


def _slot_table(rel_bias, sinks, n_kv):
    n_heads = rel_bias.shape[1]
    rb = rel_bias.astype(F32)
    qpos = jnp.arange(BLOCK)[:, None]
    kpos = jnp.arange(2 * BLOCK)[None, :]
    mpos = jnp.arange(N_META)[None, :]
    d_band = BLOCK + qpos - kpos
    in_window = (d_band >= 0) & (d_band < WINDOW)
    bias_band = rb[_t5_bucket(jnp.maximum(d_band, 0))]
    sink = jnp.broadcast_to(sinks.astype(F32)[None, None, :], (BLOCK, 1, n_heads))
    pad = jnp.full((BLOCK, SLOTS - 2 * BLOCK - N_META - 1, n_heads), NEG_INF, F32)
    tables = []
    for n in (0, 1):
        valid = in_window & (n * BLOCK + kpos >= BLOCK)
        band = jnp.where(valid[:, :, None], bias_band, NEG_INF)
        meta = rb[_t5_bucket(N_META + n * BLOCK + qpos - mpos)]
        tables.append(jnp.concatenate([band, meta, sink, pad], axis=1))
    tbl = jnp.transpose(jnp.stack(tables), (0, 3, 1, 2))
    return tbl.reshape(2, n_kv, GROUP * BLOCK, SLOTS)


def kernel(x, meta_tokens, rel_bias, conv_w_in, conv_b_in, conv_w_dw, conv_b_dw, conv_ln_g, conv_ln_b,
           conv_w_out, conv_b_out, attn_w_qkv, attn_b_qkv, attn_sinks, attn_w_o, attn_b_o,
           norm_mix_pre, norm_mix_post, norm_ffn_pre, norm_ffn_post, ffn_w_gate, ffn_w_up, ffn_w_down):
    batch, seq, d = x.shape
    n_kv = (attn_w_qkv.shape[2] - d) // (2 * HEAD_DIM)
    rows = batch * seq
    row = lambda a: a.reshape(1, -1).astype(F32)
    bf = lambda a: a.astype(BF16)

    tm_mm, tn_mm, tm_conv, tm_ffn, tf = 512, 512, 256, 512, 512
    tm_meta = N_META

    h0 = x.reshape(rows, d)
    hm0 = meta_tokens.astype(x.dtype)

    w_in, b_in = bf(conv_w_in[0]), row(conv_b_in[0])
    conv_args = (conv_w_dw[0].astype(F32), row(conv_b_dw[0]), row(conv_ln_g[0]), row(conv_ln_b[0]),
                 bf(conv_w_out[0]), row(conv_b_out[0]), row(norm_mix_post[0]))
    ffn0 = (row(norm_ffn_pre[0]), bf(ffn_w_gate[0]), bf(ffn_w_up[0]), bf(ffn_w_down[0]),
            row(norm_ffn_post[0]))
    gpre0 = row(norm_mix_pre[0])

    vm = _conv_in(hm0, gpre0, w_in, b_in, tm=tm_meta, tn=tn_mm)
    zero_halo = jnp.zeros((HALO_ROWS, d), BF16)
    hm1 = _conv_out(vm, zero_halo, zero_halo, *conv_args, hm0, tm=tm_meta, rows_per_seq=N_META)
    hm2 = _ffn(hm1, *ffn0, tm=tm_meta, tf=tf)

    v = _conv_in(h0, gpre0, w_in, b_in, tm=tm_mm, tn=tn_mm)
    halo0 = jnp.concatenate([jnp.zeros((HALO_ROWS - N_META, d), BF16), vm], axis=0)
    h1 = _conv_out(v, v, halo0, *conv_args, h0, tm=tm_conv, rows_per_seq=seq)
    h2 = _ffn(h1, *ffn0, tm=tm_ffn, tf=tf)

    w_qkv, b_qkv = bf(attn_w_qkv[0]), row(attn_b_qkv[0])
    gpre1 = row(norm_mix_pre[1])
    _, k_meta, v_meta = _qkv(hm2, gpre1, w_qkv, b_qkv, tm=tm_meta, n_kv=n_kv)
    q, k, vv = _qkv(h2, gpre1, w_qkv, b_qkv, tm=tm_mm, n_kv=n_kv)
    tbl = _slot_table(rel_bias, attn_sinks[0], n_kv)
    a = _attention(q, k, vv, k_meta, v_meta, tbl, batch=batch, seq=seq)
    h3 = _oproj(a, bf(attn_w_o[0]), row(attn_b_o[0]), row(norm_mix_post[1]), h2, tm=tm_mm)
    h4 = _ffn(h3, row(norm_ffn_pre[1]), bf(ffn_w_gate[1]), bf(ffn_w_up[1]), bf(ffn_w_down[1]),
              row(norm_ffn_post[1]), tm=tm_ffn, tf=tf)
    return h4.reshape(batch, seq, d)
```

```python
import functools
import math

import jax
import jax.numpy as jnp
from jax import lax
from jax.experimental import pallas as pl
from jax.experimental.pallas import tpu as pltpu

N_META = 16
CONV_WIDTH = 31
HEAD_DIM = 64
GROUP = 8
WINDOW = 128
BLOCK = 128
N_BUCKETS = 32
MAX_DISTANCE = 128
RMS_EPS = 1e-6
LN_EPS = 1e-5
NEG_INF = -1e30

V7X_LANES = 128
V7X_SUBLANES = 8
V7X_BF16_SUBLANE_ROWS = 16
CAST_ROWS = 256
CONV_ROW_CHUNK = 128
V7X_VMEM_BYTES = 64 * 1024 * 1024
V7X_VMEM_REQUEST_CAP = V7X_VMEM_BYTES - 6 * 1024 * 1024

HALO_ROWS = 2 * V7X_BF16_SUBLANE_ROWS
SLOTS = 3 * V7X_LANES
BF16 = jnp.bfloat16
F32 = jnp.float32


def _vmem_limit(pipelined_bytes, resident_bytes):
    need = 2 * pipelined_bytes + resident_bytes
    return int(min(V7X_VMEM_REQUEST_CAP, need + need // 4 + (2 << 20)))


def _nbytes(shape, dtype):
    return math.prod(shape) * jnp.dtype(dtype).itemsize


def _rms(x, g):
    return x * lax.rsqrt(jnp.mean(x * x, axis=-1, keepdims=True) + RMS_EPS) * g


def _sigmoid(x):
    return 1.0 / (1.0 + jnp.exp(-x))


def _mm(a, b):
    return jnp.dot(a, b, preferred_element_type=F32)


def _cast_body(x_ref, o_ref):
    o_ref[...] = x_ref[...].astype(o_ref.dtype)


def _to_bf16(w, *, kb):
    n_l, k, n = w.shape
    blk = pl.BlockSpec((None, kb, n), lambda l, i: (l, i, 0))
    return pl.pallas_call(
        _cast_body,
        grid=(n_l, k // kb),
        in_specs=[blk],
        out_specs=blk,
        out_shape=jax.ShapeDtypeStruct(w.shape, BF16),
        compiler_params=pltpu.CompilerParams(
            dimension_semantics=("parallel", "parallel"),
            vmem_limit_bytes=_vmem_limit(_nbytes((kb, n), F32) + _nbytes((kb, n), BF16), 0)),
        name="cast_bf16",
    )(w)


def _resident(shape):
    return pl.BlockSpec(shape, lambda i: (0,) * len(shape), pipeline_mode=pl.Buffered(1))


def _conv_in_body(h_ref, gpre_ref, w_ref, b_ref, o_ref, *, tn):
    d = h_ref.shape[1]
    xn = _rms(h_ref[...], gpre_ref[...]).astype(BF16)
    for j in range(d // tn):
        ca, cg = slice(j * tn, (j + 1) * tn), slice(d + j * tn, d + (j + 1) * tn)
        a = _mm(xn, w_ref[:, ca]) + b_ref[:, ca]
        g = _mm(xn, w_ref[:, cg]) + b_ref[:, cg]
        o_ref[:, ca] = (a * _sigmoid(g)).astype(o_ref.dtype)


def _conv_in(h, gpre, w_in, b_in, *, tm, tn):
    rows, d = h.shape
    pipelined = _nbytes((tm, d), F32) + _nbytes((tm, d), BF16)
    resident = _nbytes(w_in.shape, BF16) + _nbytes((tm, d), BF16) + 6 * _nbytes((tm, tn), F32)
    return pl.pallas_call(
        functools.partial(_conv_in_body, tn=tn),
        grid=(rows // tm,),
        in_specs=[
            pl.BlockSpec((tm, d), lambda i: (i, 0)),
            _resident((1, d)), _resident(w_in.shape), _resident(b_in.shape),
        ],
        out_specs=pl.BlockSpec((tm, d), lambda i: (i, 0)),
        out_shape=jax.ShapeDtypeStruct((rows, d), BF16),
        compiler_params=pltpu.CompilerParams(
            dimension_semantics=("parallel",),
            vmem_limit_bytes=_vmem_limit(pipelined, resident)),
        name="conv_in",
    )(h, gpre, w_in, b_in)


def _conv_out_body(v_ref, vprev_ref, halo0_ref, wdw_ref, bdw_ref, lng_ref, lnb_ref, wout_ref,
                   bout_ref, gpost_ref, h_ref, o_ref, xbuf_ref, u_ref, *, tm, rt, blocks_per_seq):
    n_slab, _, cw = xbuf_ref.shape
    first = pl.program_id(0) % blocks_per_seq == 0

    def fill_halo(src_ref):
        for c in range(n_slab):
            xbuf_ref[c, pl.ds(0, HALO_ROWS, stride=2), :] = src_ref[:, c * cw:(c + 1) * cw].astype(F32)

    pl.when(first)(functools.partial(fill_halo, halo0_ref))
    pl.when(jnp.logical_not(first))(functools.partial(fill_halo, vprev_ref))
    for c in range(n_slab):
        xbuf_ref[c, pl.ds(2 * HALO_ROWS, tm, stride=2), :] = v_ref[:, c * cw:(c + 1) * cw].astype(F32)

    def chunk(c, carry):
        ls = pl.ds(pl.multiple_of(c * cw, cw), cw)
        for r0 in range(0, tm, rt):
            acc = jnp.broadcast_to(bdw_ref[:, ls], (rt, cw)).reshape(rt // V7X_SUBLANES, V7X_SUBLANES, cw)
            for m in range(CONV_WIDTH):
                xs = xbuf_ref[c, pl.ds(2 * (HALO_ROWS + r0 - m), rt, stride=2), :]
                acc = acc + xs.reshape(acc.shape) * wdw_ref[c, CONV_WIDTH - 1 - m][None]
            u_ref[r0:r0 + rt, ls] = acc.reshape(rt, cw)
        return carry

    lax.fori_loop(0, n_slab, chunk, 0)

    u = u_ref[...]
    mu = jnp.mean(u, axis=-1, keepdims=True)
    uc = u - mu
    var = jnp.mean(uc * uc, axis=-1, keepdims=True)
    y = uc * lax.rsqrt(var + LN_EPS) * lng_ref[...] + lnb_ref[...]
    y = y * _sigmoid(y)
    z = _mm(y.astype(BF16), wout_ref[...]) + bout_ref[...]
    o_ref[...] = h_ref[...] + _rms(z, gpost_ref[...])


def _conv_out(v, vprev_src, halo0, w_dw, b_dw, ln_g, ln_b, w_out, b_out, gpost, h, *, tm,
              rows_per_seq):
    rows, d = h.shape
    cw = V7X_LANES
    n_slab = d // cw
    body = functools.partial(_conv_out_body, tm=tm, rt=min(tm, CONV_ROW_CHUNK),
                             blocks_per_seq=rows_per_seq // tm)
    halo_blocks_per_tile = tm // HALO_ROWS
    w_slabs = jnp.broadcast_to(jnp.transpose(w_dw.reshape(CONV_WIDTH, n_slab, cw), (1, 0, 2))[:, :, None, :],
                               (n_slab, CONV_WIDTH, V7X_SUBLANES, cw))
    vec = lambda: pl.BlockSpec((1, d), lambda i: (0, 0))
    pipelined = (_nbytes((tm, d), BF16) + 2 * _nbytes((HALO_ROWS, d), BF16)
                 + _nbytes((HALO_ROWS, d), F32) + 2 * _nbytes((tm, d), F32))
    resident = (_nbytes((d, d), BF16) + 2 * _nbytes((tm + HALO_ROWS, d), F32)
                + 4 * _nbytes((tm, d), F32))
    return pl.pallas_call(
        body,
        grid=(rows // tm,),
        in_specs=[
            pl.BlockSpec((tm, d), lambda i: (i, 0)),
            pl.BlockSpec((HALO_ROWS, d), lambda i: (jnp.maximum(i * halo_blocks_per_tile - 1, 0), 0)),
            pl.BlockSpec((HALO_ROWS, d), lambda i: (0, 0)),
            _resident((n_slab, CONV_WIDTH, V7X_SUBLANES, cw)),
            vec(), vec(), vec(),
            _resident((d, d)),
            vec(), vec(),
            pl.BlockSpec((tm, d), lambda i: (i, 0)),
        ],
        out_specs=pl.BlockSpec((tm, d), lambda i: (i, 0)),
        out_shape=jax.ShapeDtypeStruct((rows, d), F32),
        scratch_shapes=[pltpu.VMEM((n_slab, 2 * (tm + HALO_ROWS), cw), F32), pltpu.VMEM((tm, d), F32)],
        compiler_params=pltpu.CompilerParams(
            dimension_semantics=("parallel",),
            vmem_limit_bytes=_vmem_limit(pipelined, resident)),
        name="conv_out",
    )(v, vprev_src, halo0, w_slabs, b_dw, ln_g, ln_b, w_out, b_out, gpost, h)


def _ffn_body(h_ref, gpre_ref, wg_ref, wu_ref, wd_ref, gpost_ref, o_ref, xn_ref, acc_ref):
    f = pl.program_id(1)

    @pl.when(f == 0)
    def _():
        xn_ref[...] = _rms(h_ref[...], gpre_ref[...]).astype(BF16)
        acc_ref[...] = jnp.zeros_like(acc_ref)

    xn = xn_ref[...]
    g = _mm(xn, wg_ref[...])
    u = _mm(xn, wu_ref[...])
    a = (g * _sigmoid(g) * u).astype(BF16)
    acc_ref[...] += _mm(a, wd_ref[...])

    @pl.when(f == pl.num_programs(1) - 1)
    def _():
        o_ref[...] = h_ref[...] + _rms(acc_ref[...], gpost_ref[...])


def _ffn(h, gpre, w_gate, w_up, w_down, gpost, *, layer, tm, tf):
    rows, d = h.shape
    ff = w_gate.shape[2]
    pipelined = 2 * _nbytes((tm, d), F32) + 3 * _nbytes((d, tf), BF16)
    resident = (_nbytes((tm, d), BF16) + 2 * _nbytes((tm, d), F32) + 3 * _nbytes((tm, tf), F32))
    return pl.pallas_call(
        _ffn_body,
        grid=(rows // tm, ff // tf),
        in_specs=[
            pl.BlockSpec((tm, d), lambda i, f: (i, 0)),
            pl.BlockSpec((1, d), lambda i, f: (0, 0)),
            pl.BlockSpec((None, d, tf), lambda i, f: (layer, 0, f)),
            pl.BlockSpec((None, d, tf), lambda i, f: (layer, 0, f)),
            pl.BlockSpec((None, tf, d), lambda i, f: (layer, f, 0)),
            pl.BlockSpec((1, d), lambda i, f: (0, 0)),
        ],
        out_specs=pl.BlockSpec((tm, d), lambda i, f: (i, 0)),
        out_shape=jax.ShapeDtypeStruct((rows, d), F32),
        scratch_shapes=[pltpu.VMEM((tm, d), BF16), pltpu.VMEM((tm, d), F32)],
        compiler_params=pltpu.CompilerParams(
            dimension_semantics=("parallel", "arbitrary"),
            vmem_limit_bytes=_vmem_limit(pipelined, resident)),
        name="ffn",
    )(h, gpre, w_gate, w_up, w_down, gpost)


def _qkv_body(h_ref, gpre_ref, w_ref, b_ref, q_ref, k_ref, v_ref, *, tn, n_kv):
    d = h_ref.shape[1]
    xn = _rms(h_ref[...], gpre_ref[...]).astype(BF16)
    for j in range(d // tn):
        cq = slice(j * tn, (j + 1) * tn)
        q_ref[:, cq] = ((_mm(xn, w_ref[:, cq]) + b_ref[:, cq]) * (HEAD_DIM ** -0.5)).astype(BF16)
    y = _mm(xn, w_ref[:, d:]) + b_ref[:, d:]
    for hh in range(n_kv):
        k_ref[hh] = y[:, hh * HEAD_DIM:(hh + 1) * HEAD_DIM].astype(BF16)
        v_ref[hh] = y[:, (n_kv + hh) * HEAD_DIM:(n_kv + hh + 1) * HEAD_DIM].astype(BF16)


def _qkv(h, gpre, w_qkv, b_qkv, *, tm, tn, n_kv):
    rows, d = h.shape
    kv_shape = jax.ShapeDtypeStruct((n_kv, rows, HEAD_DIM), BF16)
    pipelined = (_nbytes((tm, d), F32) + _nbytes((tm, d), BF16)
                 + 2 * _nbytes((n_kv, tm, V7X_LANES), BF16))
    resident = _nbytes(w_qkv.shape, BF16) + _nbytes((tm, d), BF16) + 4 * _nbytes((tm, tn), F32)
    return pl.pallas_call(
        functools.partial(_qkv_body, tn=tn, n_kv=n_kv),
        grid=(rows // tm,),
        in_specs=[
            pl.BlockSpec((tm, d), lambda i: (i, 0)),
            _resident((1, d)), _resident(w_qkv.shape), _resident(b_qkv.shape),
        ],
        out_specs=[
            pl.BlockSpec((tm, d), lambda i: (i, 0)),
            pl.BlockSpec((n_kv, tm, HEAD_DIM), lambda i: (0, i, 0)),
            pl.BlockSpec((n_kv, tm, HEAD_DIM), lambda i: (0, i, 0)),
        ],
        out_shape=[jax.ShapeDtypeStruct((rows, d), BF16), kv_shape, kv_shape],
        compiler_params=pltpu.CompilerParams(
            dimension_semantics=("parallel",),
            vmem_limit_bytes=_vmem_limit(pipelined, resident)),
        name="qkv",
    )(h, gpre, w_qkv, b_qkv)


def _attn_body(q_ref, kp_ref, kc_ref, vp_ref, vc_ref, km_ref, vm_ref, tbl_ref, o_ref):
    q = q_ref[...]
    qs = jnp.concatenate([q[:, g * HEAD_DIM:(g + 1) * HEAD_DIM] for g in range(GROUP)], axis=0)
    zpad = jnp.zeros((SLOTS - 2 * BLOCK - N_META, HEAD_DIM), BF16)
    kf = jnp.concatenate([kp_ref[0], kc_ref[0], km_ref[0], zpad], axis=0)
    vf = jnp.concatenate([vp_ref[0], vc_ref[0], vm_ref[0], zpad], axis=0)
    s = lax.dot_general(qs, kf, (((1,), (1,)), ((), ())), preferred_element_type=F32)
    s = s + tbl_ref[0, 0]
    m = jnp.max(s, axis=-1, keepdims=True)
    p = jnp.exp(s - m)
    l = jnp.sum(p, axis=-1, keepdims=True)
    o = _mm(p.astype(BF16), vf) / l
    o_ref[...] = jnp.concatenate(
        [o[g * BLOCK:(g + 1) * BLOCK] for g in range(GROUP)], axis=1).astype(o_ref.dtype)


def _attention(q, k, v, k_meta, v_meta, tbl, *, batch, seq):
    rows, d = q.shape
    n_kv = k.shape[0]
    n_blk = seq // BLOCK
    qw = GROUP * HEAD_DIM
    row_blk = lambda h, b, n: b * n_blk + n
    prev_blk = lambda h, b, n: b * n_blk + jnp.maximum(n - 1, 0)
    kv_spec = lambda fn: pl.BlockSpec((1, BLOCK, HEAD_DIM), lambda h, b, n: (h, fn(h, b, n), 0))
    meta_spec = pl.BlockSpec((1, N_META, HEAD_DIM), lambda h, b, n: (h, 0, 0))
    pipelined = (2 * _nbytes((BLOCK, qw), BF16) + 4 * _nbytes((BLOCK, V7X_LANES), BF16)
                 + _nbytes((GROUP * BLOCK, SLOTS), F32))
    resident = 4 * _nbytes((GROUP * BLOCK, SLOTS), F32)
    return pl.pallas_call(
        _attn_body,
        grid=(n_kv, batch, n_blk),
        in_specs=[
            pl.BlockSpec((BLOCK, qw), lambda h, b, n: (row_blk(h, b, n), h)),
            kv_spec(prev_blk), kv_spec(row_blk), kv_spec(prev_blk), kv_spec(row_blk),
            meta_spec, meta_spec,
            pl.BlockSpec((1, 1, GROUP * BLOCK, SLOTS), lambda h, b, n: (jnp.minimum(n, 1), h, 0, 0)),
        ],
        out_specs=pl.BlockSpec((BLOCK, qw), lambda h, b, n: (row_blk(h, b, n), h)),
        out_shape=jax.ShapeDtypeStruct((rows, d), BF16),
        compiler_params=pltpu.CompilerParams(
            dimension_semantics=("parallel", "parallel", "parallel"),
            vmem_limit_bytes=_vmem_limit(pipelined, resident)),
        name="swa_attention",
    )(q, k, k, v, v, k_meta, v_meta, tbl)


def _oproj_body(a_ref, w_ref, b_ref, gpost_ref, h_ref, o_ref):
    z = _mm(a_ref[...], w_ref[...]) + b_ref[...]
    o_ref[...] = h_ref[...] + _rms(z, gpost_ref[...])


def _oproj(a, w_o, b_o, gpost, h, *, tm):
    rows, d = h.shape
    vec = lambda: pl.BlockSpec((1, d), lambda i: (0, 0))
    pipelined = _nbytes((tm, d), BF16) + 2 * _nbytes((tm, d), F32)
    resident = _nbytes((d, d), BF16) + 3 * _nbytes((tm, d), F32)
    return pl.pallas_call(
        _oproj_body,
        grid=(rows // tm,),
        in_specs=[
            pl.BlockSpec((tm, d), lambda i: (i, 0)),
            _resident((d, d)),
            vec(), vec(),
            pl.BlockSpec((tm, d), lambda i: (i, 0)),
        ],
        out_specs=pl.BlockSpec((tm, d), lambda i: (i, 0)),
        out_shape=jax.ShapeDtypeStruct((rows, d), F32),
        compiler_params=pltpu.CompilerParams(
            dimension_semantics=("parallel",),
            vmem_limit_bytes=_vmem_limit(pipelined, resident)),
        name="attn_out_proj",
    )(a, w_o, b_o, gpost, h)


def _t5_bucket(dist):
    max_exact = N_BUCKETS // 2
    dd = jnp.maximum(dist, max_exact).astype(F32)
    large = max_exact + (jnp.log(dd / max_exact) / math.log(MAX_DISTANCE / max_exact)
                         * (N_BUCKETS - max_exact)).astype(jnp.int32)
    return jnp.where(dist < max_exact, dist, jnp.minimum(large, N_BUCKETS - 1))


def _slot_table(rel_bias, sinks, n_kv):
    n_heads = rel_bias.shape[1]
    rb = rel_bias.astype(F32)

    def lookup(bucket):
        onehot = (bucket[..., None] == jnp.arange(N_BUCKETS)).astype(F32)
        return jnp.einsum("...b,bh->h...", onehot, rb, precision=lax.Precision.HIGHEST)

    qpos = jnp.arange(BLOCK)[:, None]
    kpos = jnp.arange(2 * BLOCK)[None, :]
    mpos = jnp.arange(N_META)[None, :]
    d_band = BLOCK + qpos - kpos
    in_window = (d_band >= 0) & (d_band < WINDOW)
    bias_band = lookup(_t5_bucket(jnp.maximum(d_band, 0)))
    sink = jnp.broadcast_to(sinks.astype(F32)[:, None, None], (n_heads, BLOCK, 1))
    pad = jnp.full((n_heads, BLOCK, SLOTS - 2 * BLOCK - N_META - 1), NEG_INF, F32)
    tables = []
    for n in (0, 1):
        valid = in_window & (n * BLOCK + kpos >= BLOCK)
        band = jnp.where(valid[None], bias_band, NEG_INF)
        meta = lookup(_t5_bucket(N_META + n * BLOCK + qpos - mpos))
        tables.append(jnp.concatenate([band, meta, sink, pad], axis=2))
    return jnp.stack(tables).reshape(2, n_kv, GROUP * BLOCK, SLOTS)


def kernel(x, meta_tokens, rel_bias, conv_w_in, conv_b_in, conv_w_dw, conv_b_dw, conv_ln_g, conv_ln_b,
           conv_w_out, conv_b_out, attn_w_qkv, attn_b_qkv, attn_sinks, attn_w_o, attn_b_o,
           norm_mix_pre, norm_mix_post, norm_ffn_pre, norm_ffn_post, ffn_w_gate, ffn_w_up, ffn_w_down):
    batch, seq, d = x.shape
    n_kv = (attn_w_qkv.shape[2] - d) // (2 * HEAD_DIM)
    rows = batch * seq
    row = lambda a: a.reshape(1, -1).astype(F32)
    bf = functools.partial(_to_bf16, kb=CAST_ROWS)

    tm_mm, tn_mm, tm_conv, tm_ffn, tf = 512, 512, 256, 512, 512
    tm_meta = N_META

    h0 = x.reshape(rows, d)
    hm0 = meta_tokens.astype(x.dtype)
    w_gate, w_up, w_down = bf(ffn_w_gate), bf(ffn_w_up), bf(ffn_w_down)
    ffn = lambda h, layer, tm: _ffn(h, row(norm_ffn_pre[layer]), w_gate, w_up, w_down,
                                    row(norm_ffn_post[layer]), layer=layer, tm=tm, tf=tf)

    w_in, b_in = bf(conv_w_in)[0], row(conv_b_in[0])
    conv_args = (conv_w_dw[0].astype(F32), row(conv_b_dw[0]), row(conv_ln_g[0]), row(conv_ln_b[0]),
                 bf(conv_w_out)[0], row(conv_b_out[0]), row(norm_mix_post[0]))
    gpre0 = row(norm_mix_pre[0])

    vm = _conv_in(hm0, gpre0, w_in, b_in, tm=tm_meta, tn=tn_mm)
    zero_halo = jnp.zeros((HALO_ROWS, d), BF16)
    hm1 = _conv_out(vm, zero_halo, zero_halo, *conv_args, hm0, tm=tm_meta, rows_per_seq=N_META)
    hm2 = ffn(hm1, 0, tm_meta)

    v = _conv_in(h0, gpre0, w_in, b_in, tm=tm_mm, tn=tn_mm)
    halo0 = jnp.concatenate([jnp.zeros((HALO_ROWS - N_META, d), BF16), vm], axis=0)
    h1 = _conv_out(v, v, halo0, *conv_args, h0, tm=tm_conv, rows_per_seq=seq)
    h2 = ffn(h1, 0, tm_ffn)

    w_qkv, b_qkv = bf(attn_w_qkv)[0], row(attn_b_qkv[0])
    gpre1 = row(norm_mix_pre[1])
    _, k_meta, v_meta = _qkv(hm2, gpre1, w_qkv, b_qkv, tm=tm_meta, tn=tn_mm, n_kv=n_kv)
    q, k, vv = _qkv(h2, gpre1, w_qkv, b_qkv, tm=tm_mm, tn=tn_mm, n_kv=n_kv)
    tbl = _slot_table(rel_bias, attn_sinks[0], n_kv)
    a = _attention(q, k, vv, k_meta, v_meta, tbl, batch=batch, seq=seq)
    h3 = _oproj(a, bf(attn_w_o)[0], row(attn_b_o[0]), row(norm_mix_post[1]), h2, tm=tm_mm)
    h4 = ffn(h3, 1, tm_ffn)
    return h4.reshape(batch, seq, d)
```

```python
import functools
import math

import jax
import jax.numpy as jnp
from jax import lax
from jax.experimental import pallas as pl
from jax.experimental.pallas import tpu as pltpu

N_META = 16
CONV_WIDTH = 31
HEAD_DIM = 64
GROUP = 8
WINDOW = 128
BLOCK = 128
N_BUCKETS = 32
MAX_DISTANCE = 128
RMS_EPS = 1e-6
LN_EPS = 1e-5
NEG_INF = -1e30

V7X_LANES = 128
V7X_SUBLANES = 8
V7X_BF16_SUBLANE_ROWS = 16
ATTN_KV_HEADS_PER_STEP = 2
CAST_ROWS = 256
CONV_ROW_CHUNK = 128
V7X_VMEM_BYTES = 64 * 1024 * 1024
V7X_VMEM_REQUEST_CAP = V7X_VMEM_BYTES - 6 * 1024 * 1024

HALO_ROWS = 2 * V7X_BF16_SUBLANE_ROWS
SUB = BLOCK // 2
N_SUB = BLOCK // SUB
BAND_SLOTS = BLOCK + SUB
SLOTS = 2 * V7X_LANES
BF16 = jnp.bfloat16
F32 = jnp.float32


def _vmem_limit(pipelined_bytes, resident_bytes):
    need = 2 * pipelined_bytes + resident_bytes
    return int(min(V7X_VMEM_REQUEST_CAP, need + need // 4 + (2 << 20)))


def _nbytes(shape, dtype):
    return math.prod(shape) * jnp.dtype(dtype).itemsize


def _rms(x, g):
    return x * lax.rsqrt(jnp.mean(x * x, axis=-1, keepdims=True) + RMS_EPS) * g


def _sigmoid(x):
    return 1.0 / (1.0 + jnp.exp(-x))


def _mm(a, b):
    return jnp.dot(a, b, preferred_element_type=F32)


def _cast_body(x_ref, o_ref):
    o_ref[...] = x_ref[...].astype(o_ref.dtype)


def _to_bf16(w, *, kb):
    n_l, k, n = w.shape
    blk = pl.BlockSpec((None, kb, n), lambda l, i: (l, i, 0))
    return pl.pallas_call(
        _cast_body,
        grid=(n_l, k // kb),
        in_specs=[blk],
        out_specs=blk,
        out_shape=jax.ShapeDtypeStruct(w.shape, BF16),
        compiler_params=pltpu.CompilerParams(
            dimension_semantics=("parallel", "parallel"),
            vmem_limit_bytes=_vmem_limit(_nbytes((kb, n), F32) + _nbytes((kb, n), BF16), 0)),
        name="cast_bf16",
    )(w)


def _resident(shape):
    return pl.BlockSpec(shape, lambda i: (0,) * len(shape), pipeline_mode=pl.Buffered(1))


def _conv_in_body(h_ref, gpre_ref, w_ref, b_ref, o_ref, *, tn):
    d = h_ref.shape[1]
    xn = _rms(h_ref[...], gpre_ref[...]).astype(BF16)
    for j in range(d // tn):
        ca, cg = slice(j * tn, (j + 1) * tn), slice(d + j * tn, d + (j + 1) * tn)
        a = _mm(xn, w_ref[:, ca]) + b_ref[:, ca]
        g = _mm(xn, w_ref[:, cg]) + b_ref[:, cg]
        o_ref[:, ca] = (a * _sigmoid(g)).astype(o_ref.dtype)


def _conv_in(h, gpre, w_in, b_in, *, tm, tn):
    rows, d = h.shape
    pipelined = _nbytes((tm, d), F32) + _nbytes((tm, d), BF16)
    resident = _nbytes(w_in.shape, BF16) + _nbytes((tm, d), BF16) + 6 * _nbytes((tm, tn), F32)
    return pl.pallas_call(
        functools.partial(_conv_in_body, tn=tn),
        grid=(rows // tm,),
        in_specs=[
            pl.BlockSpec((tm, d), lambda i: (i, 0)),
            _resident((1, d)), _resident(w_in.shape), _resident(b_in.shape),
        ],
        out_specs=pl.BlockSpec((tm, d), lambda i: (i, 0)),
        out_shape=jax.ShapeDtypeStruct((rows, d), BF16),
        compiler_params=pltpu.CompilerParams(
            dimension_semantics=("parallel",),
            vmem_limit_bytes=_vmem_limit(pipelined, resident)),
        name="conv_in",
    )(h, gpre, w_in, b_in)


def _conv_out_body(v_ref, vprev_ref, halo0_ref, wdw_ref, bdw_ref, lng_ref, lnb_ref, wout_ref,
                   bout_ref, gpost_ref, h_ref, o_ref, xbuf_ref, u_ref, *, tm, rt, blocks_per_seq):
    n_slab, _, cw = xbuf_ref.shape
    first = pl.program_id(0) % blocks_per_seq == 0

    def fill_halo(src_ref):
        for c in range(n_slab):
            xbuf_ref[c, pl.ds(0, HALO_ROWS, stride=2), :] = src_ref[:, c * cw:(c + 1) * cw].astype(F32)

    pl.when(first)(functools.partial(fill_halo, halo0_ref))
    pl.when(jnp.logical_not(first))(functools.partial(fill_halo, vprev_ref))
    for c in range(n_slab):
        xbuf_ref[c, pl.ds(2 * HALO_ROWS, tm, stride=2), :] = v_ref[:, c * cw:(c + 1) * cw].astype(F32)

    def chunk(c, carry):
        ls = pl.ds(pl.multiple_of(c * cw, cw), cw)
        for r0 in range(0, tm, rt):
            acc = jnp.broadcast_to(bdw_ref[:, ls], (rt, cw)).reshape(rt // V7X_SUBLANES, V7X_SUBLANES, cw)
            for m in range(CONV_WIDTH):
                xs = xbuf_ref[c, pl.ds(2 * (HALO_ROWS + r0 - m), rt, stride=2), :]
                acc = acc + xs.reshape(acc.shape) * wdw_ref[c, CONV_WIDTH - 1 - m][None]
            u_ref[r0:r0 + rt, ls] = acc.reshape(rt, cw)
        return carry

    lax.fori_loop(0, n_slab, chunk, 0)

    u = u_ref[...]
    mu = jnp.mean(u, axis=-1, keepdims=True)
    uc = u - mu
    var = jnp.mean(uc * uc, axis=-1, keepdims=True)
    y = uc * lax.rsqrt(var + LN_EPS) * lng_ref[...] + lnb_ref[...]
    y = y * _sigmoid(y)
    z = _mm(y.astype(BF16), wout_ref[...]) + bout_ref[...]
    o_ref[...] = h_ref[...] + _rms(z, gpost_ref[...])


def _conv_out(v, vprev_src, halo0, w_dw, b_dw, ln_g, ln_b, w_out, b_out, gpost, h, *, tm,
              rows_per_seq):
    rows, d = h.shape
    cw = V7X_LANES
    n_slab = d // cw
    body = functools.partial(_conv_out_body, tm=tm, rt=min(tm, CONV_ROW_CHUNK),
                             blocks_per_seq=rows_per_seq // tm)
    halo_blocks_per_tile = tm // HALO_ROWS
    w_slabs = jnp.broadcast_to(jnp.transpose(w_dw.reshape(CONV_WIDTH, n_slab, cw), (1, 0, 2))[:, :, None, :],
                               (n_slab, CONV_WIDTH, V7X_SUBLANES, cw))
    vec = lambda: pl.BlockSpec((1, d), lambda i: (0, 0))
    pipelined = (_nbytes((tm, d), BF16) + 2 * _nbytes((HALO_ROWS, d), BF16)
                 + _nbytes((HALO_ROWS, d), F32) + 2 * _nbytes((tm, d), F32))
    resident = (_nbytes((d, d), BF16) + 2 * _nbytes((tm + HALO_ROWS, d), F32)
                + 4 * _nbytes((tm, d), F32))
    return pl.pallas_call(
        body,
        grid=(rows // tm,),
        in_specs=[
            pl.BlockSpec((tm, d), lambda i: (i, 0)),
            pl.BlockSpec((HALO_ROWS, d), lambda i: (jnp.maximum(i * halo_blocks_per_tile - 1, 0), 0)),
            pl.BlockSpec((HALO_ROWS, d), lambda i: (0, 0)),
            _resident((n_slab, CONV_WIDTH, V7X_SUBLANES, cw)),
            vec(), vec(), vec(),
            _resident((d, d)),
            vec(), vec(),
            pl.BlockSpec((tm, d), lambda i: (i, 0)),
        ],
        out_specs=pl.BlockSpec((tm, d), lambda i: (i, 0)),
        out_shape=jax.ShapeDtypeStruct((rows, d), F32),
        scratch_shapes=[pltpu.VMEM((n_slab, 2 * (tm + HALO_ROWS), cw), F32), pltpu.VMEM((tm, d), F32)],
        compiler_params=pltpu.CompilerParams(
            dimension_semantics=("parallel",),
            vmem_limit_bytes=_vmem_limit(pipelined, resident)),
        name="conv_out",
    )(v, vprev_src, halo0, w_slabs, b_dw, ln_g, ln_b, w_out, b_out, gpost, h)


def _ffn_body(h_ref, gpre_ref, wg_ref, wu_ref, wd_ref, gpost_ref, o_ref, xn_ref):
    f = pl.program_id(1)

    @pl.when(f == 0)
    def _():
        xn_ref[...] = _rms(h_ref[...], gpre_ref[...]).astype(BF16)
        o_ref[...] = jnp.zeros_like(o_ref)

    xn = xn_ref[...]
    g = _mm(xn, wg_ref[...])
    u = _mm(xn, wu_ref[...])
    a = (g * _sigmoid(g) * u).astype(BF16)
    o_ref[...] += _mm(a, wd_ref[...])

    @pl.when(f == pl.num_programs(1) - 1)
    def _():
        o_ref[...] = h_ref[...] + _rms(o_ref[...], gpost_ref[...])


def _ffn(h, gpre, w_gate, w_up, w_down, gpost, *, layer, tm, tf):
    rows, d = h.shape
    ff = w_gate.shape[2]
    pipelined = 2 * _nbytes((tm, d), F32) + 3 * _nbytes((d, tf), BF16)
    resident = _nbytes((tm, d), BF16) + 4 * _nbytes((tm, tf), F32)
    return pl.pallas_call(
        _ffn_body,
        grid=(rows // tm, ff // tf),
        in_specs=[
            pl.BlockSpec((tm, d), lambda i, f: (i, 0)),
            pl.BlockSpec((1, d), lambda i, f: (0, 0)),
            pl.BlockSpec((None, d, tf), lambda i, f: (layer, 0, f)),
            pl.BlockSpec((None, d, tf), lambda i, f: (layer, 0, f)),
            pl.BlockSpec((None, tf, d), lambda i, f: (layer, f, 0)),
            pl.BlockSpec((1, d), lambda i, f: (0, 0)),
        ],
        out_specs=pl.BlockSpec((tm, d), lambda i, f: (i, 0)),
        out_shape=jax.ShapeDtypeStruct((rows, d), F32),
        scratch_shapes=[pltpu.VMEM((tm, d), BF16)],
        compiler_params=pltpu.CompilerParams(
            dimension_semantics=("parallel", "arbitrary"),
            vmem_limit_bytes=_vmem_limit(pipelined, resident)),
        name="ffn",
    )(h, gpre, w_gate, w_up, w_down, gpost)


def _qkv_body(h_ref, gpre_ref, w_ref, b_ref, q_ref, k_ref, v_ref, *, tn, n_kv):
    d = h_ref.shape[1]
    xn = _rms(h_ref[...], gpre_ref[...]).astype(BF16)
    for j in range(d // tn):
        cq = slice(j * tn, (j + 1) * tn)
        q_ref[:, cq] = ((_mm(xn, w_ref[:, cq]) + b_ref[:, cq]) * (HEAD_DIM ** -0.5)).astype(BF16)
    y = _mm(xn, w_ref[:, d:]) + b_ref[:, d:]
    for hh in range(n_kv):
        k_ref[hh] = y[:, hh * HEAD_DIM:(hh + 1) * HEAD_DIM].astype(BF16)
        v_ref[hh] = y[:, (n_kv + hh) * HEAD_DIM:(n_kv + hh + 1) * HEAD_DIM].astype(BF16)


def _qkv(h, gpre, w_qkv, b_qkv, *, tm, tn, n_kv):
    rows, d = h.shape
    kv_shape = jax.ShapeDtypeStruct((n_kv, rows, HEAD_DIM), BF16)
    pipelined = (_nbytes((tm, d), F32) + _nbytes((tm, d), BF16)
                 + 2 * _nbytes((n_kv, tm, V7X_LANES), BF16))
    resident = _nbytes(w_qkv.shape, BF16) + _nbytes((tm, d), BF16) + 4 * _nbytes((tm, tn), F32)
    return pl.pallas_call(
        functools.partial(_qkv_body, tn=tn, n_kv=n_kv),
        grid=(rows // tm,),
        in_specs=[
            pl.BlockSpec((tm, d), lambda i: (i, 0)),
            _resident((1, d)), _resident(w_qkv.shape), _resident(b_qkv.shape),
        ],
        out_specs=[
            pl.BlockSpec((tm, d), lambda i: (i, 0)),
            pl.BlockSpec((n_kv, tm, HEAD_DIM), lambda i: (0, i, 0)),
            pl.BlockSpec((n_kv, tm, HEAD_DIM), lambda i: (0, i, 0)),
        ],
        out_shape=[jax.ShapeDtypeStruct((rows, d), BF16), kv_shape, kv_shape],
        compiler_params=pltpu.CompilerParams(
            dimension_semantics=("parallel",),
            vmem_limit_bytes=_vmem_limit(pipelined, resident)),
        name="qkv",
    )(h, gpre, w_qkv, b_qkv)


def _attn_body(q_ref, kp_ref, kc_ref, vp_ref, vc_ref, km_ref, vm_ref, tbl_ref, o_ref):
    qw = GROUP * HEAD_DIM
    zpad = jnp.zeros((SLOTS - BAND_SLOTS - N_META, HEAD_DIM), BF16)
    for hh in range(kp_ref.shape[0]):
        outs = []
        for sb in range(N_SUB):
            lo = sb * SUB
            band = lambda p_ref, c_ref: jnp.concatenate(
                [p_ref[hh, lo:BLOCK], c_ref[hh, 0:lo + SUB]], axis=0)
            kf = jnp.concatenate([band(kp_ref, kc_ref), km_ref[hh], zpad], axis=0)
            vf = jnp.concatenate([band(vp_ref, vc_ref), vm_ref[hh], zpad], axis=0)
            q = q_ref[lo:lo + SUB, hh * qw:(hh + 1) * qw]
            qs = jnp.concatenate([q[:, g * HEAD_DIM:(g + 1) * HEAD_DIM] for g in range(GROUP)], axis=0)
            s = lax.dot_general(qs, kf, (((1,), (1,)), ((), ())), preferred_element_type=F32)
            s = s + tbl_ref[0, hh, sb]
            m = jnp.max(s, axis=-1, keepdims=True)
            p = jnp.exp(s - m)
            l = jnp.sum(p, axis=-1, keepdims=True)
            outs.append(_mm(p.astype(BF16), vf) / l)
        o_ref[:, hh * qw:(hh + 1) * qw] = jnp.concatenate(
            [jnp.concatenate([o[g * SUB:(g + 1) * SUB] for o in outs], axis=0) for g in range(GROUP)],
            axis=1).astype(o_ref.dtype)


def _attention(q, k, v, k_meta, v_meta, tbl, *, batch, seq, hps):
    rows, d = q.shape
    n_kv = k.shape[0]
    n_blk = seq // BLOCK
    qw = hps * GROUP * HEAD_DIM
    row_blk = lambda h, b, n: b * n_blk + n
    prev_blk = lambda h, b, n: b * n_blk + jnp.maximum(n - 1, 0)
    kv_spec = lambda fn: pl.BlockSpec((hps, BLOCK, HEAD_DIM), lambda h, b, n: (h, fn(h, b, n), 0))
    meta_spec = pl.BlockSpec((hps, N_META, HEAD_DIM), lambda h, b, n: (h, 0, 0))
    pipelined = (2 * _nbytes((BLOCK, qw), BF16) + 4 * _nbytes((hps, BLOCK, V7X_LANES), BF16)
                 + _nbytes((hps, GROUP * BLOCK, SLOTS), F32))
    resident = 6 * hps * _nbytes((GROUP * BLOCK, SLOTS), F32)
    return pl.pallas_call(
        _attn_body,
        grid=(n_kv // hps, batch, n_blk),
        in_specs=[
            pl.BlockSpec((BLOCK, qw), lambda h, b, n: (row_blk(h, b, n), h)),
            kv_spec(prev_blk), kv_spec(row_blk), kv_spec(prev_blk), kv_spec(row_blk),
            meta_spec, meta_spec,
            pl.BlockSpec((1, hps, N_SUB, GROUP * SUB, SLOTS),
                         lambda h, b, n: (jnp.minimum(n, 1), h, 0, 0, 0)),
        ],
        out_specs=pl.BlockSpec((BLOCK, qw), lambda h, b, n: (row_blk(h, b, n), h)),
        out_shape=jax.ShapeDtypeStruct((rows, d), BF16),
        compiler_params=pltpu.CompilerParams(
            dimension_semantics=("parallel", "parallel", "parallel"),
            vmem_limit_bytes=_vmem_limit(pipelined, resident)),
        name="swa_attention",
    )(q, k, k, v, v, k_meta, v_meta, tbl)


def _oproj_body(a_ref, w_ref, b_ref, gpost_ref, h_ref, o_ref):
    z = _mm(a_ref[...], w_ref[...]) + b_ref[...]
    o_ref[...] = h_ref[...] + _rms(z, gpost_ref[...])


def _oproj(a, w_o, b_o, gpost, h, *, tm):
    rows, d = h.shape
    vec = lambda: pl.BlockSpec((1, d), lambda i: (0, 0))
    pipelined = _nbytes((tm, d), BF16) + 2 * _nbytes((tm, d), F32)
    resident = _nbytes((d, d), BF16) + 3 * _nbytes((tm, d), F32)
    return pl.pallas_call(
        _oproj_body,
        grid=(rows // tm,),
        in_specs=[
            pl.BlockSpec((tm, d), lambda i: (i, 0)),
            _resident((d, d)),
            vec(), vec(),
            pl.BlockSpec((tm, d), lambda i: (i, 0)),
        ],
        out_specs=pl.BlockSpec((tm, d), lambda i: (i, 0)),
        out_shape=jax.ShapeDtypeStruct((rows, d), F32),
        compiler_params=pltpu.CompilerParams(
            dimension_semantics=("parallel",),
            vmem_limit_bytes=_vmem_limit(pipelined, resident)),
        name="attn_out_proj",
    )(a, w_o, b_o, gpost, h)


def _t5_bucket(dist):
    max_exact = N_BUCKETS // 2
    dd = jnp.maximum(dist, max_exact).astype(F32)
    large = max_exact + (jnp.log(dd / max_exact) / math.log(MAX_DISTANCE / max_exact)
                         * (N_BUCKETS - max_exact)).astype(jnp.int32)
    return jnp.where(dist < max_exact, dist, jnp.minimum(large, N_BUCKETS - 1))


def _slot_table(rel_bias, sinks, n_kv):
    n_heads = rel_bias.shape[1]
    rb = rel_bias.astype(F32)

    def lookup(bucket):
        onehot = (bucket[..., None] == jnp.arange(N_BUCKETS)).astype(F32)
        return jnp.einsum("...b,bh->h...", onehot, rb, precision=lax.Precision.HIGHEST)

    qpos = jnp.arange(BLOCK)[:, None]
    kpos = jnp.arange(2 * BLOCK)[None, :]
    mpos = jnp.arange(N_META)[None, :]
    d_band = BLOCK + qpos - kpos
    in_window = (d_band >= 0) & (d_band < WINDOW)
    bias_band = lookup(_t5_bucket(jnp.maximum(d_band, 0)))
    sink = jnp.broadcast_to(sinks.astype(F32)[:, None, None], (n_heads, SUB, 1))
    pad = jnp.full((n_heads, SUB, SLOTS - BAND_SLOTS - N_META - 1), NEG_INF, F32)
    tables = []
    for n in (0, 1):
        valid = in_window & (n * BLOCK + kpos >= BLOCK)
        band = jnp.where(valid[None], bias_band, NEG_INF)
        meta = lookup(_t5_bucket(N_META + n * BLOCK + qpos - mpos))
        subs = []
        for sb in range(N_SUB):
            lo = sb * SUB
            subs.append(jnp.concatenate([band[:, lo:lo + SUB, lo:lo + BAND_SLOTS],
                                         meta[:, lo:lo + SUB], sink, pad], axis=2))
        t = jnp.stack(subs, axis=1).reshape(n_kv, GROUP, N_SUB, SUB, SLOTS)
        tables.append(jnp.transpose(t, (0, 2, 1, 3, 4)).reshape(n_kv, N_SUB, GROUP * SUB, SLOTS))
    return jnp.stack(tables)


def kernel(x, meta_tokens, rel_bias, conv_w_in, conv_b_in, conv_w_dw, conv_b_dw, conv_ln_g, conv_ln_b,
           conv_w_out, conv_b_out, attn_w_qkv, attn_b_qkv, attn_sinks, attn_w_o, attn_b_o,
           norm_mix_pre, norm_mix_post, norm_ffn_pre, norm_ffn_post, ffn_w_gate, ffn_w_up, ffn_w_down):
    batch, seq, d = x.shape
    n_kv = (attn_w_qkv.shape[2] - d) // (2 * HEAD_DIM)
    rows = batch * seq
    row = lambda a: a.reshape(1, -1).astype(F32)
    bf = functools.partial(_to_bf16, kb=CAST_ROWS)

    tm_mm, tn_mm, tm_conv, tm_ffn, tf = 512, 512, 256, 512, 512
    tm_meta = N_META

    h0 = x.reshape(rows, d)
    hm0 = meta_tokens.astype(x.dtype)
    w_gate, w_up, w_down = bf(ffn_w_gate), bf(ffn_w_up), bf(ffn_w_down)
    ffn = lambda h, layer, tm: _ffn(h, row(norm_ffn_pre[layer]), w_gate, w_up, w_down,
                                    row(norm_ffn_post[layer]), layer=layer, tm=tm, tf=tf)

    w_in, b_in = bf(conv_w_in)[0], row(conv_b_in[0])
    conv_args = (conv_w_dw[0].astype(F32), row(conv_b_dw[0]), row(conv_ln_g[0]), row(conv_ln_b[0]),
                 bf(conv_w_out)[0], row(conv_b_out[0]), row(norm_mix_post[0]))
    gpre0 = row(norm_mix_pre[0])

    vm = _conv_in(hm0, gpre0, w_in, b_in, tm=tm_meta, tn=tn_mm)
    zero_halo = jnp.zeros((HALO_ROWS, d), BF16)
    hm1 = _conv_out(vm, zero_halo, zero_halo, *conv_args, hm0, tm=tm_meta, rows_per_seq=N_META)
    hm2 = ffn(hm1, 0, tm_meta)

    v = _conv_in(h0, gpre0, w_in, b_in, tm=tm_mm, tn=tn_mm)
    halo0 = jnp.concatenate([jnp.zeros((HALO_ROWS - N_META, d), BF16), vm], axis=0)
    h1 = _conv_out(v, v, halo0, *conv_args, h0, tm=tm_conv, rows_per_seq=seq)
    h2 = ffn(h1, 0, tm_ffn)

    w_qkv, b_qkv = bf(attn_w_qkv)[0], row(attn_b_qkv[0])
    gpre1 = row(norm_mix_pre[1])
    _, k_meta, v_meta = _qkv(hm2, gpre1, w_qkv, b_qkv, tm=tm_meta, tn=tn_mm, n_kv=n_kv)
    q, k, vv = _qkv(h2, gpre1, w_qkv, b_qkv, tm=tm_mm, tn=tn_mm, n_kv=n_kv)
    tbl = _slot_table(rel_bias, attn_sinks[0], n_kv)
    a = _attention(q, k, vv, k_meta, v_meta, tbl, batch=batch, seq=seq, hps=ATTN_KV_HEADS_PER_STEP)
    h3 = _oproj(a, bf(attn_w_o)[0], row(attn_b_o[0]), row(norm_mix_post[1]), h2, tm=tm_mm)
    h4 = ffn(h3, 1, tm_ffn)
    return h4.reshape(batch, seq, d)
```

```python
import functools
import math

import jax
import jax.numpy as jnp
from jax import lax
from jax.experimental import pallas as pl
from jax.experimental.pallas import tpu as pltpu

N_META = 16
CONV_WIDTH = 31
HEAD_DIM = 64
GROUP = 8
WINDOW = 128
BLOCK = 128
N_BUCKETS = 32
MAX_DISTANCE = 128
RMS_EPS = 1e-6
LN_EPS = 1e-5
NEG_INF = -1e30

V7X_LANES = 128
V7X_SUBLANES = 8
V7X_BF16_SUBLANE_ROWS = 16
ATTN_KV_HEADS_PER_STEP = 2
CAST_ROWS = 256
CONV_ROW_CHUNK = 128
V7X_VMEM_BYTES = 64 * 1024 * 1024
V7X_VMEM_REQUEST_CAP = V7X_VMEM_BYTES - 6 * 1024 * 1024

HALO_ROWS = 2 * V7X_BF16_SUBLANE_ROWS
SUB = BLOCK // 2
N_SUB = BLOCK // SUB
BAND_SLOTS = BLOCK + SUB
SLOTS = 2 * V7X_LANES
BF16 = jnp.bfloat16
F32 = jnp.float32


def _vmem_limit(pipelined_bytes, resident_bytes):
    need = 2 * pipelined_bytes + resident_bytes
    return int(min(V7X_VMEM_REQUEST_CAP, need + need // 4 + (2 << 20)))


def _nbytes(shape, dtype):
    return math.prod(shape) * jnp.dtype(dtype).itemsize


def _rms(x, g):
    return x * lax.rsqrt(jnp.mean(x * x, axis=-1, keepdims=True) + RMS_EPS) * g


def _sigmoid(x):
    return 1.0 / (1.0 + jnp.exp(-x))


def _mm(a, b):
    return jnp.dot(a, b, preferred_element_type=F32)


def _cast_body(x_ref, o_ref):
    o_ref[...] = x_ref[...].astype(o_ref.dtype)


def _to_bf16(w, *, kb):
    n_l, k, n = w.shape
    blk = pl.BlockSpec((None, kb, n), lambda l, i: (l, i, 0))
    return pl.pallas_call(
        _cast_body,
        grid=(n_l, k // kb),
        in_specs=[blk],
        out_specs=blk,
        out_shape=jax.ShapeDtypeStruct(w.shape, BF16),
        compiler_params=pltpu.CompilerParams(
            dimension_semantics=("parallel", "parallel"),
            vmem_limit_bytes=_vmem_limit(_nbytes((kb, n), F32) + _nbytes((kb, n), BF16), 0)),
        name="cast_bf16",
    )(w)


def _resident(shape):
    return pl.BlockSpec(shape, lambda i: (0,) * len(shape), pipeline_mode=pl.Buffered(1))


def _conv_in_body(h_ref, gpre_ref, w_ref, b_ref, o_ref, *, tn):
    d = h_ref.shape[1]
    xn = _rms(h_ref[...], gpre_ref[...]).astype(BF16)
    for j in range(d // tn):
        ca, cg = slice(j * tn, (j + 1) * tn), slice(d + j * tn, d + (j + 1) * tn)
        a = _mm(xn, w_ref[:, ca]) + b_ref[:, ca]
        g = _mm(xn, w_ref[:, cg]) + b_ref[:, cg]
        o_ref[:, ca] = (a * _sigmoid(g)).astype(o_ref.dtype)


def _conv_in(h, gpre, w_in, b_in, *, tm, tn):
    rows, d = h.shape
    pipelined = _nbytes((tm, d), F32) + _nbytes((tm, d), BF16)
    resident = _nbytes(w_in.shape, BF16) + _nbytes((tm, d), BF16) + 6 * _nbytes((tm, tn), F32)
    return pl.pallas_call(
        functools.partial(_conv_in_body, tn=tn),
        grid=(rows // tm,),
        in_specs=[
            pl.BlockSpec((tm, d), lambda i: (i, 0)),
            _resident((1, d)), _resident(w_in.shape), _resident(b_in.shape),
        ],
        out_specs=pl.BlockSpec((tm, d), lambda i: (i, 0)),
        out_shape=jax.ShapeDtypeStruct((rows, d), BF16),
        compiler_params=pltpu.CompilerParams(
            dimension_semantics=("parallel",),
            vmem_limit_bytes=_vmem_limit(pipelined, resident)),
        name="conv_in",
    )(h, gpre, w_in, b_in)


def _conv_out_body(v_ref, vprev_ref, halo0_ref, wdw_ref, bdw_ref, lng_ref, lnb_ref, wout_ref,
                   bout_ref, gpost_ref, h_ref, o_ref, xbuf_ref, u_ref, *, tm, rt, blocks_per_seq):
    n_slab, _, cw = xbuf_ref.shape
    first = pl.program_id(0) % blocks_per_seq == 0

    def fill_halo(src_ref):
        for c in range(n_slab):
            xbuf_ref[c, pl.ds(0, HALO_ROWS, stride=2), :] = src_ref[:, c * cw:(c + 1) * cw].astype(F32)

    pl.when(first)(functools.partial(fill_halo, halo0_ref))
    pl.when(jnp.logical_not(first))(functools.partial(fill_halo, vprev_ref))
    for c in range(n_slab):
        xbuf_ref[c, pl.ds(2 * HALO_ROWS, tm, stride=2), :] = v_ref[:, c * cw:(c + 1) * cw].astype(F32)

    def chunk(c, carry):
        ls = pl.ds(pl.multiple_of(c * cw, cw), cw)
        for r0 in range(0, tm, rt):
            acc = jnp.broadcast_to(bdw_ref[:, ls], (rt, cw)).reshape(rt // V7X_SUBLANES, V7X_SUBLANES, cw)
            for m in range(CONV_WIDTH):
                xs = xbuf_ref[c, pl.ds(2 * (HALO_ROWS + r0 - m), rt, stride=2), :]
                acc = acc + xs.reshape(acc.shape) * wdw_ref[c, CONV_WIDTH - 1 - m][None]
            u_ref[r0:r0 + rt, ls] = acc.reshape(rt, cw)
        return carry

    lax.fori_loop(0, n_slab, chunk, 0)

    u = u_ref[...]
    mu = jnp.mean(u, axis=-1, keepdims=True)
    uc = u - mu
    var = jnp.mean(uc * uc, axis=-1, keepdims=True)
    y = uc * lax.rsqrt(var + LN_EPS) * lng_ref[...] + lnb_ref[...]
    y = y * _sigmoid(y)
    z = _mm(y.astype(BF16), wout_ref[...]) + bout_ref[...]
    o_ref[...] = h_ref[...] + _rms(z, gpost_ref[...])


def _conv_out(v, vprev_src, halo0, w_dw, b_dw, ln_g, ln_b, w_out, b_out, gpost, h, *, tm,
              rows_per_seq):
    rows, d = h.shape
    cw = V7X_LANES
    n_slab = d // cw
    body = functools.partial(_conv_out_body, tm=tm, rt=min(tm, CONV_ROW_CHUNK),
                             blocks_per_seq=rows_per_seq // tm)
    halo_blocks_per_tile = tm // HALO_ROWS
    w_slabs = jnp.broadcast_to(jnp.transpose(w_dw.reshape(CONV_WIDTH, n_slab, cw), (1, 0, 2))[:, :, None, :],
                               (n_slab, CONV_WIDTH, V7X_SUBLANES, cw))
    vec = lambda: pl.BlockSpec((1, d), lambda i: (0, 0))
    pipelined = (_nbytes((tm, d), BF16) + 2 * _nbytes((HALO_ROWS, d), BF16)
                 + _nbytes((HALO_ROWS, d), F32) + 2 * _nbytes((tm, d), F32))
    resident = (_nbytes((d, d), BF16) + 2 * _nbytes((tm + HALO_ROWS, d), F32)
                + 4 * _nbytes((tm, d), F32))
    return pl.pallas_call(
        body,
        grid=(rows // tm,),
        in_specs=[
            pl.BlockSpec((tm, d), lambda i: (i, 0)),
            pl.BlockSpec((HALO_ROWS, d), lambda i: (jnp.maximum(i * halo_blocks_per_tile - 1, 0), 0)),
            pl.BlockSpec((HALO_ROWS, d), lambda i: (0, 0)),
            _resident((n_slab, CONV_WIDTH, V7X_SUBLANES, cw)),
            vec(), vec(), vec(),
            _resident((d, d)),
            vec(), vec(),
            pl.BlockSpec((tm, d), lambda i: (i, 0)),
        ],
        out_specs=pl.BlockSpec((tm, d), lambda i: (i, 0)),
        out_shape=jax.ShapeDtypeStruct((rows, d), F32),
        scratch_shapes=[pltpu.VMEM((n_slab, 2 * (tm + HALO_ROWS), cw), F32), pltpu.VMEM((tm, d), F32)],
        compiler_params=pltpu.CompilerParams(
            dimension_semantics=("parallel",),
            vmem_limit_bytes=_vmem_limit(pipelined, resident)),
        name="conv_out",
    )(v, vprev_src, halo0, w_slabs, b_dw, ln_g, ln_b, w_out, b_out, gpost, h)


def _ffn_body(h_ref, hprev_ref, gpre_ref, wg_ref, wu_ref, wd_ref, gpost_ref, o_ref, xn_ref, acc_ref,
              *, n_blocks):
    i, f = pl.program_id(0), pl.program_id(1)

    def down_partial(xn):
        g = _mm(xn, wg_ref[...])
        u = _mm(xn, wu_ref[...])
        return _mm((g * _sigmoid(g) * u).astype(BF16), wd_ref[...])

    @pl.when((i == 0) & (f == 0))
    def _():
        acc_ref[...] = jnp.zeros_like(acc_ref)

    @pl.when(f == 0)
    def _():
        o_ref[...] = hprev_ref[...] + _rms(acc_ref[...], gpost_ref[...])
        xn = _rms(h_ref[...], gpre_ref[...]).astype(BF16)
        xn_ref[...] = xn
        acc_ref[...] = down_partial(xn)

    @pl.when((f > 0) & (i < n_blocks))
    def _():
        acc_ref[...] += down_partial(xn_ref[...])


def _ffn(h, gpre, w_gate, w_up, w_down, gpost, *, layer, tm, tf):
    rows, d = h.shape
    ff = w_gate.shape[2]
    n_blocks = rows // tm
    cur = lambda i, f: (jnp.minimum(i, n_blocks - 1), 0)
    prev = lambda i, f: (jnp.maximum(i - 1, 0), 0)
    tile = lambda i, f: jnp.where(i < n_blocks, f, 0)
    pipelined = 3 * _nbytes((tm, d), F32) + 3 * _nbytes((d, tf), BF16)
    resident = _nbytes((tm, d), BF16) + _nbytes((tm, d), F32) + 4 * _nbytes((tm, tf), F32)
    return pl.pallas_call(
        functools.partial(_ffn_body, n_blocks=n_blocks),
        grid=(n_blocks + 1, ff // tf),
        in_specs=[
            pl.BlockSpec((tm, d), cur),
            pl.BlockSpec((tm, d), prev),
            pl.BlockSpec((1, d), lambda i, f: (0, 0)),
            pl.BlockSpec((None, d, tf), lambda i, f: (layer, 0, tile(i, f))),
            pl.BlockSpec((None, d, tf), lambda i, f: (layer, 0, tile(i, f))),
            pl.BlockSpec((None, tf, d), lambda i, f: (layer, tile(i, f), 0)),
            pl.BlockSpec((1, d), lambda i, f: (0, 0)),
        ],
        out_specs=pl.BlockSpec((tm, d), prev),
        out_shape=jax.ShapeDtypeStruct((rows, d), F32),
        scratch_shapes=[pltpu.VMEM((tm, d), BF16), pltpu.VMEM((tm, d), F32)],
        compiler_params=pltpu.CompilerParams(
            dimension_semantics=("arbitrary", "arbitrary"),
            vmem_limit_bytes=_vmem_limit(pipelined, resident)),
        name="ffn",
    )(h, h, gpre, w_gate, w_up, w_down, gpost)


def _qkv_body(h_ref, gpre_ref, w_ref, b_ref, q_ref, k_ref, v_ref, *, tn, n_kv):
    d = h_ref.shape[1]
    xn = _rms(h_ref[...], gpre_ref[...]).astype(BF16)
    for j in range(d // tn):
        cq = slice(j * tn, (j + 1) * tn)
        q_ref[:, cq] = ((_mm(xn, w_ref[:, cq]) + b_ref[:, cq]) * (HEAD_DIM ** -0.5)).astype(BF16)
    y = _mm(xn, w_ref[:, d:]) + b_ref[:, d:]
    for hh in range(n_kv):
        k_ref[hh] = y[:, hh * HEAD_DIM:(hh + 1) * HEAD_DIM].astype(BF16)
        v_ref[hh] = y[:, (n_kv + hh) * HEAD_DIM:(n_kv + hh + 1) * HEAD_DIM].astype(BF16)


def _qkv(h, gpre, w_qkv, b_qkv, *, tm, tn, n_kv):
    rows, d = h.shape
    kv_shape = jax.ShapeDtypeStruct((n_kv, rows, HEAD_DIM), BF16)
    pipelined = (_nbytes((tm, d), F32) + _nbytes((tm, d), BF16)
                 + 2 * _nbytes((n_kv, tm, V7X_LANES), BF16))
    resident = _nbytes(w_qkv.shape, BF16) + _nbytes((tm, d), BF16) + 4 * _nbytes((tm, tn), F32)
    return pl.pallas_call(
        functools.partial(_qkv_body, tn=tn, n_kv=n_kv),
        grid=(rows // tm,),
        in_specs=[
            pl.BlockSpec((tm, d), lambda i: (i, 0)),
            _resident((1, d)), _resident(w_qkv.shape), _resident(b_qkv.shape),
        ],
        out_specs=[
            pl.BlockSpec((tm, d), lambda i: (i, 0)),
            pl.BlockSpec((n_kv, tm, HEAD_DIM), lambda i: (0, i, 0)),
            pl.BlockSpec((n_kv, tm, HEAD_DIM), lambda i: (0, i, 0)),
        ],
        out_shape=[jax.ShapeDtypeStruct((rows, d), BF16), kv_shape, kv_shape],
        compiler_params=pltpu.CompilerParams(
            dimension_semantics=("parallel",),
            vmem_limit_bytes=_vmem_limit(pipelined, resident)),
        name="qkv",
    )(h, gpre, w_qkv, b_qkv)


def _attn_body(q_ref, kp_ref, kc_ref, vp_ref, vc_ref, km_ref, vm_ref, tbl_ref, o_ref):
    qw = GROUP * HEAD_DIM
    zpad = jnp.zeros((SLOTS - BAND_SLOTS - N_META, HEAD_DIM), BF16)
    for hh in range(kp_ref.shape[0]):
        outs = []
        for sb in range(N_SUB):
            lo = sb * SUB
            band = lambda p_ref, c_ref: jnp.concatenate(
                [p_ref[hh, lo:BLOCK], c_ref[hh, 0:lo + SUB]], axis=0)
            kf = jnp.concatenate([band(kp_ref, kc_ref), km_ref[hh], zpad], axis=0)
            vf = jnp.concatenate([band(vp_ref, vc_ref), vm_ref[hh], zpad], axis=0)
            q = q_ref[lo:lo + SUB, hh * qw:(hh + 1) * qw]
            qs = jnp.concatenate([q[:, g * HEAD_DIM:(g + 1) * HEAD_DIM] for g in range(GROUP)], axis=0)
            s = lax.dot_general(qs, kf, (((1,), (1,)), ((), ())), preferred_element_type=F32)
            s = s + tbl_ref[0, hh, sb]
            m = jnp.max(s, axis=-1, keepdims=True)
            p = jnp.exp(s - m)
            l = jnp.sum(p, axis=-1, keepdims=True)
            outs.append(_mm(p.astype(BF16), vf) / l)
        o_ref[:, hh * qw:(hh + 1) * qw] = jnp.concatenate(
            [jnp.concatenate([o[g * SUB:(g + 1) * SUB] for o in outs], axis=0) for g in range(GROUP)],
            axis=1).astype(o_ref.dtype)


def _attention(q, k, v, k_meta, v_meta, tbl, *, batch, seq, hps):
    rows, d = q.shape
    n_kv = k.shape[0]
    n_blk = seq // BLOCK
    qw = hps * GROUP * HEAD_DIM
    row_blk = lambda h, b, n: b * n_blk + n
    prev_blk = lambda h, b, n: b * n_blk + jnp.maximum(n - 1, 0)
    kv_spec = lambda fn: pl.BlockSpec((hps, BLOCK, HEAD_DIM), lambda h, b, n: (h, fn(h, b, n), 0))
    meta_spec = pl.BlockSpec((hps, N_META, HEAD_DIM), lambda h, b, n: (h, 0, 0))
    pipelined = (2 * _nbytes((BLOCK, qw), BF16) + 4 * _nbytes((hps, BLOCK, V7X_LANES), BF16)
                 + _nbytes((hps, GROUP * BLOCK, SLOTS), F32))
    resident = 6 * hps * _nbytes((GROUP * BLOCK, SLOTS), F32)
    return pl.pallas_call(
        _attn_body,
        grid=(n_kv // hps, batch, n_blk),
        in_specs=[
            pl.BlockSpec((BLOCK, qw), lambda h, b, n: (row_blk(h, b, n), h)),
            kv_spec(prev_blk), kv_spec(row_blk), kv_spec(prev_blk), kv_spec(row_blk),
            meta_spec, meta_spec,
            pl.BlockSpec((1, hps, N_SUB, GROUP * SUB, SLOTS),
                         lambda h, b, n: (jnp.minimum(n, 1), h, 0, 0, 0)),
        ],
        out_specs=pl.BlockSpec((BLOCK, qw), lambda h, b, n: (row_blk(h, b, n), h)),
        out_shape=jax.ShapeDtypeStruct((rows, d), BF16),
        compiler_params=pltpu.CompilerParams(
            dimension_semantics=("parallel", "parallel", "parallel"),
            vmem_limit_bytes=_vmem_limit(pipelined, resident)),
        name="swa_attention",
    )(q, k, k, v, v, k_meta, v_meta, tbl)


def _oproj_body(a_ref, w_ref, b_ref, gpost_ref, h_ref, o_ref):
    z = _mm(a_ref[...], w_ref[...]) + b_ref[...]
    o_ref[...] = h_ref[...] + _rms(z, gpost_ref[...])


def _oproj(a, w_o, b_o, gpost, h, *, tm):
    rows, d = h.shape
    vec = lambda: pl.BlockSpec((1, d), lambda i: (0, 0))
    pipelined = _nbytes((tm, d), BF16) + 2 * _nbytes((tm, d), F32)
    resident = _nbytes((d, d), BF16) + 3 * _nbytes((tm, d), F32)
    return pl.pallas_call(
        _oproj_body,
        grid=(rows // tm,),
        in_specs=[
            pl.BlockSpec((tm, d), lambda i: (i, 0)),
            _resident((d, d)),
            vec(), vec(),
            pl.BlockSpec((tm, d), lambda i: (i, 0)),
        ],
        out_specs=pl.BlockSpec((tm, d), lambda i: (i, 0)),
        out_shape=jax.ShapeDtypeStruct((rows, d), F32),
        compiler_params=pltpu.CompilerParams(
            dimension_semantics=("parallel",),
            vmem_limit_bytes=_vmem_limit(pipelined, resident)),
        name="attn_out_proj",
    )(a, w_o, b_o, gpost, h)


def _t5_bucket(dist):
    max_exact = N_BUCKETS // 2
    dd = jnp.maximum(dist, max_exact).astype(F32)
    large = max_exact + (jnp.log(dd / max_exact) / math.log(MAX_DISTANCE / max_exact)
                         * (N_BUCKETS - max_exact)).astype(jnp.int32)
    return jnp.where(dist < max_exact, dist, jnp.minimum(large, N_BUCKETS - 1))


def _slot_table(rel_bias, sinks, n_kv):
    n_heads = rel_bias.shape[1]
    rb = rel_bias.astype(F32)

    def lookup(bucket):
        onehot = (bucket[..., None] == jnp.arange(N_BUCKETS)).astype(F32)
        return jnp.einsum("...b,bh->h...", onehot, rb, precision=lax.Precision.HIGHEST)

    qpos = jnp.arange(BLOCK)[:, None]
    kpos = jnp.arange(2 * BLOCK)[None, :]
    mpos = jnp.arange(N_META)[None, :]
    d_band = BLOCK + qpos - kpos
    in_window = (d_band >= 0) & (d_band < WINDOW)
    bias_band = lookup(_t5_bucket(jnp.maximum(d_band, 0)))
    sink = jnp.broadcast_to(sinks.astype(F32)[:, None, None], (n_heads, SUB, 1))
    pad = jnp.full((n_heads, SUB, SLOTS - BAND_SLOTS - N_META - 1), NEG_INF, F32)
    tables = []
    for n in (0, 1):
        valid = in_window & (n * BLOCK + kpos >= BLOCK)
        band = jnp.where(valid[None], bias_band, NEG_INF)
        meta = lookup(_t5_bucket(N_META + n * BLOCK + qpos - mpos))
        subs = []
        for sb in range(N_SUB):
            lo = sb * SUB
            subs.append(jnp.concatenate([band[:, lo:lo + SUB, lo:lo + BAND_SLOTS],
                                         meta[:, lo:lo + SUB], sink, pad], axis=2))
        t = jnp.stack(subs, axis=1).reshape(n_kv, GROUP, N_SUB, SUB, SLOTS)
        tables.append(jnp.transpose(t, (0, 2, 1, 3, 4)).reshape(n_kv, N_SUB, GROUP * SUB, SLOTS))
    return jnp.stack(tables)


def kernel(x, meta_tokens, rel_bias, conv_w_in, conv_b_in, conv_w_dw, conv_b_dw, conv_ln_g, conv_ln_b,
           conv_w_out, conv_b_out, attn_w_qkv, attn_b_qkv, attn_sinks, attn_w_o, attn_b_o,
           norm_mix_pre, norm_mix_post, norm_ffn_pre, norm_ffn_post, ffn_w_gate, ffn_w_up, ffn_w_down):
    batch, seq, d = x.shape
    n_kv = (attn_w_qkv.shape[2] - d) // (2 * HEAD_DIM)
    rows = batch * seq
    row = lambda a: a.reshape(1, -1).astype(F32)
    bf = functools.partial(_to_bf16, kb=CAST_ROWS)

    tm_mm, tn_mm, tm_conv, tm_ffn, tf = 512, 512, 256, 512, 512
    tm_meta = N_META

    h0 = x.reshape(rows, d)
    hm0 = meta_tokens.astype(x.dtype)
    w_gate, w_up, w_down = bf(ffn_w_gate), bf(ffn_w_up), bf(ffn_w_down)
    ffn = lambda h, layer, tm: _ffn(h, row(norm_ffn_pre[layer]), w_gate, w_up, w_down,
                                    row(norm_ffn_post[layer]), layer=layer, tm=tm, tf=tf)

    w_in, b_in = bf(conv_w_in)[0], row(conv_b_in[0])
    conv_args = (conv_w_dw[0].astype(F32), row(conv_b_dw[0]), row(conv_ln_g[0]), row(conv_ln_b[0]),
                 bf(conv_w_out)[0], row(conv_b_out[0]), row(norm_mix_post[0]))
    gpre0 = row(norm_mix_pre[0])

    vm = _conv_in(hm0, gpre0, w_in, b_in, tm=tm_meta, tn=tn_mm)
    zero_halo = jnp.zeros((HALO_ROWS, d), BF16)
    hm1 = _conv_out(vm, zero_halo, zero_halo, *conv_args, hm0, tm=tm_meta, rows_per_seq=N_META)
    hm2 = ffn(hm1, 0, tm_meta)

    v = _conv_in(h0, gpre0, w_in, b_in, tm=tm_mm, tn=tn_mm)
    halo0 = jnp.concatenate([jnp.zeros((HALO_ROWS - N_META, d), BF16), vm], axis=0)
    h1 = _conv_out(v, v, halo0, *conv_args, h0, tm=tm_conv, rows_per_seq=seq)
    h2 = ffn(h1, 0, tm_ffn)

    w_qkv, b_qkv = bf(attn_w_qkv)[0], row(attn_b_qkv[0])
    gpre1 = row(norm_mix_pre[1])
    _, k_meta, v_meta = _qkv(hm2, gpre1, w_qkv, b_qkv, tm=tm_meta, tn=tn_mm, n_kv=n_kv)
    q, k, vv = _qkv(h2, gpre1, w_qkv, b_qkv, tm=tm_mm, tn=tn_mm, n_kv=n_kv)
    tbl = _slot_table(rel_bias, attn_sinks[0], n_kv)
    a = _attention(q, k, vv, k_meta, v_meta, tbl, batch=batch, seq=seq, hps=ATTN_KV_HEADS_PER_STEP)
    h3 = _oproj(a, bf(attn_w_o)[0], row(attn_b_o[0]), row(norm_mix_post[1]), h2, tm=tm_mm)
    h4 = ffn(h3, 1, tm_ffn)
    return h4.reshape(batch, seq, d)
```

```python
import functools
import math

import jax
import jax.numpy as jnp
from jax import lax
from jax.experimental import pallas as pl
from jax.experimental.pallas import tpu as pltpu

N_META = 16
CONV_WIDTH = 31
HEAD_DIM = 64
GROUP = 8
WINDOW = 128
BLOCK = 128
N_BUCKETS = 32
MAX_DISTANCE = 128
RMS_EPS = 1e-6
LN_EPS = 1e-5
NEG_INF = -1e30

V7X_LANES = 128
V7X_SUBLANES = 8
V7X_BF16_SUBLANE_ROWS = 16
ATTN_KV_HEADS_PER_STEP = 2
CAST_ROWS = 256
CONV_ROW_CHUNK = 128
V7X_VMEM_BYTES = 64 * 1024 * 1024
V7X_VMEM_REQUEST_CAP = V7X_VMEM_BYTES - 6 * 1024 * 1024

HALO_ROWS = 2 * V7X_BF16_SUBLANE_ROWS
SUB = BLOCK // 2
N_SUB = BLOCK // SUB
BAND_SLOTS = BLOCK + SUB
SLOTS = 2 * V7X_LANES
BF16 = jnp.bfloat16
F32 = jnp.float32


def _vmem_limit(pipelined_bytes, resident_bytes):
    need = 2 * pipelined_bytes + resident_bytes
    return int(min(V7X_VMEM_REQUEST_CAP, need + need // 4 + (2 << 20)))


def _nbytes(shape, dtype):
    return math.prod(shape) * jnp.dtype(dtype).itemsize


def _rms(x, g):
    return x * lax.rsqrt(jnp.mean(x * x, axis=-1, keepdims=True) + RMS_EPS) * g


def _sigmoid(x):
    return 1.0 / (1.0 + jnp.exp(-x))


def _mm(a, b):
    return jnp.dot(a, b, preferred_element_type=F32)


def _cast_body(x_ref, o_ref):
    o_ref[...] = x_ref[...].astype(o_ref.dtype)


def _to_bf16(w, *, kb):
    n_l, k, n = w.shape
    blk = pl.BlockSpec((None, kb, n), lambda l, i: (l, i, 0))
    return pl.pallas_call(
        _cast_body,
        grid=(n_l, k // kb),
        in_specs=[blk],
        out_specs=blk,
        out_shape=jax.ShapeDtypeStruct(w.shape, BF16),
        compiler_params=pltpu.CompilerParams(
            dimension_semantics=("parallel", "parallel"),
            vmem_limit_bytes=_vmem_limit(_nbytes((kb, n), F32) + _nbytes((kb, n), BF16), 0)),
        name="cast_bf16",
    )(w)


def _resident(shape):
    return pl.BlockSpec(shape, lambda i: (0,) * len(shape), pipeline_mode=pl.Buffered(1))


def _conv_in_body(h_ref, gpre_ref, w_ref, b_ref, o_ref, *, tn):
    d = h_ref.shape[1]
    xn = _rms(h_ref[...], gpre_ref[...]).astype(BF16)
    for j in range(d // tn):
        ca, cg = slice(j * tn, (j + 1) * tn), slice(d + j * tn, d + (j + 1) * tn)
        a = _mm(xn, w_ref[:, ca]) + b_ref[:, ca]
        g = _mm(xn, w_ref[:, cg]) + b_ref[:, cg]
        o_ref[:, ca] = (a * _sigmoid(g)).astype(o_ref.dtype)


def _conv_in(h, gpre, w_in, b_in, *, tm, tn):
    rows, d = h.shape
    pipelined = _nbytes((tm, d), F32) + _nbytes((tm, d), BF16)
    resident = _nbytes(w_in.shape, BF16) + _nbytes((tm, d), BF16) + 6 * _nbytes((tm, tn), F32)
    return pl.pallas_call(
        functools.partial(_conv_in_body, tn=tn),
        grid=(rows // tm,),
        in_specs=[
            pl.BlockSpec((tm, d), lambda i: (i, 0)),
            _resident((1, d)), _resident(w_in.shape), _resident(b_in.shape),
        ],
        out_specs=pl.BlockSpec((tm, d), lambda i: (i, 0)),
        out_shape=jax.ShapeDtypeStruct((rows, d), BF16),
        compiler_params=pltpu.CompilerParams(
            dimension_semantics=("parallel",),
            vmem_limit_bytes=_vmem_limit(pipelined, resident)),
        name="conv_in",
    )(h, gpre, w_in, b_in)


def _conv_out_body(v_ref, vprev_ref, halo0_ref, wdw_ref, bdw_ref, lng_ref, lnb_ref, wout_ref,
                   bout_ref, gpost_ref, h_ref, o_ref, xbuf_ref, u_ref, *, tm, rt, blocks_per_seq):
    n_slab, _, cw = xbuf_ref.shape
    first = pl.program_id(0) % blocks_per_seq == 0

    def fill_halo(src_ref):
        for c in range(n_slab):
            xbuf_ref[c, pl.ds(0, HALO_ROWS, stride=2), :] = src_ref[:, c * cw:(c + 1) * cw].astype(F32)

    pl.when(first)(functools.partial(fill_halo, halo0_ref))
    pl.when(jnp.logical_not(first))(functools.partial(fill_halo, vprev_ref))
    for c in range(n_slab):
        xbuf_ref[c, pl.ds(2 * HALO_ROWS, tm, stride=2), :] = v_ref[:, c * cw:(c + 1) * cw].astype(F32)

    def chunk(c, carry):
        ls = pl.ds(pl.multiple_of(c * cw, cw), cw)
        for r0 in range(0, tm, rt):
            acc = jnp.broadcast_to(bdw_ref[:, ls], (rt, cw)).reshape(rt // V7X_SUBLANES, V7X_SUBLANES, cw)
            for m in range(CONV_WIDTH):
                xs = xbuf_ref[c, pl.ds(2 * (HALO_ROWS + r0 - m), rt, stride=2), :]
                acc = acc + xs.reshape(acc.shape) * wdw_ref[c, CONV_WIDTH - 1 - m][None]
            u_ref[r0:r0 + rt, ls] = acc.reshape(rt, cw)
        return carry

    lax.fori_loop(0, n_slab, chunk, 0)

    u = u_ref[...]
    mu = jnp.mean(u, axis=-1, keepdims=True)
    uc = u - mu
    var = jnp.mean(uc * uc, axis=-1, keepdims=True)
    y = uc * lax.rsqrt(var + LN_EPS) * lng_ref[...] + lnb_ref[...]
    y = y * _sigmoid(y)
    z = _mm(y.astype(BF16), wout_ref[...]) + bout_ref[...]
    o_ref[...] = h_ref[...] + _rms(z, gpost_ref[...])


def _conv_out(v, vprev_src, halo0, w_dw, b_dw, ln_g, ln_b, w_out, b_out, gpost, h, *, tm,
              rows_per_seq):
    rows, d = h.shape
    cw = V7X_LANES
    n_slab = d // cw
    body = functools.partial(_conv_out_body, tm=tm, rt=min(tm, CONV_ROW_CHUNK),
                             blocks_per_seq=rows_per_seq // tm)
    halo_blocks_per_tile = tm // HALO_ROWS
    w_slabs = jnp.broadcast_to(jnp.transpose(w_dw.reshape(CONV_WIDTH, n_slab, cw), (1, 0, 2))[:, :, None, :],
                               (n_slab, CONV_WIDTH, V7X_SUBLANES, cw))
    vec = lambda: pl.BlockSpec((1, d), lambda i: (0, 0))
    pipelined = (_nbytes((tm, d), BF16) + 2 * _nbytes((HALO_ROWS, d), BF16)
                 + _nbytes((HALO_ROWS, d), F32) + 2 * _nbytes((tm, d), F32))
    resident = (_nbytes((d, d), BF16) + 2 * _nbytes((tm + HALO_ROWS, d), F32)
                + 4 * _nbytes((tm, d), F32))
    return pl.pallas_call(
        body,
        grid=(rows // tm,),
        in_specs=[
            pl.BlockSpec((tm, d), lambda i: (i, 0)),
            pl.BlockSpec((HALO_ROWS, d), lambda i: (jnp.maximum(i * halo_blocks_per_tile - 1, 0), 0)),
            pl.BlockSpec((HALO_ROWS, d), lambda i: (0, 0)),
            _resident((n_slab, CONV_WIDTH, V7X_SUBLANES, cw)),
            vec(), vec(), vec(),
            _resident((d, d)),
            vec(), vec(),
            pl.BlockSpec((tm, d), lambda i: (i, 0)),
        ],
        out_specs=pl.BlockSpec((tm, d), lambda i: (i, 0)),
        out_shape=jax.ShapeDtypeStruct((rows, d), F32),
        scratch_shapes=[pltpu.VMEM((n_slab, 2 * (tm + HALO_ROWS), cw), F32), pltpu.VMEM((tm, d), F32)],
        compiler_params=pltpu.CompilerParams(
            dimension_semantics=("parallel",),
            vmem_limit_bytes=_vmem_limit(pipelined, resident)),
        name="conv_out",
    )(v, vprev_src, halo0, w_slabs, b_dw, ln_g, ln_b, w_out, b_out, gpost, h)


def _ffn_body(h_ref, gpre_ref, wg_ref, wu_ref, wd_ref, gpost_ref, o_ref, xn_ref, acc_ref, hkeep_ref,
              *, n_blocks):
    i, f = pl.program_id(0), pl.program_id(1)

    def down_partial(xn):
        g = _mm(xn, wg_ref[...])
        u = _mm(xn, wu_ref[...])
        return _mm((g * _sigmoid(g) * u).astype(BF16), wd_ref[...])

    @pl.when((i == 0) & (f == 0))
    def _():
        acc_ref[...] = jnp.zeros_like(acc_ref)
        hkeep_ref[...] = jnp.zeros_like(hkeep_ref)

    @pl.when(f == 0)
    def _():
        o_ref[...] = hkeep_ref[...] + _rms(acc_ref[...], gpost_ref[...])
        h = h_ref[...]
        hkeep_ref[...] = h
        xn = _rms(h, gpre_ref[...]).astype(BF16)
        xn_ref[...] = xn
        acc_ref[...] = down_partial(xn)

    @pl.when((f > 0) & (i < n_blocks))
    def _():
        acc_ref[...] += down_partial(xn_ref[...])


def _ffn(h, gpre, w_gate, w_up, w_down, gpost, *, layer, tm, tf):
    rows, d = h.shape
    ff = w_gate.shape[2]
    n_blocks = rows // tm
    cur = lambda i, f: (jnp.minimum(i, n_blocks - 1), 0)
    prev = lambda i, f: (jnp.maximum(i - 1, 0), 0)
    tile = lambda i, f: jnp.where(i < n_blocks, f, 0)
    pipelined = 2 * _nbytes((tm, d), F32) + 3 * _nbytes((d, tf), BF16)
    resident = _nbytes((tm, d), BF16) + 2 * _nbytes((tm, d), F32) + 4 * _nbytes((tm, tf), F32)
    return pl.pallas_call(
        functools.partial(_ffn_body, n_blocks=n_blocks),
        grid=(n_blocks + 1, ff // tf),
        in_specs=[
            pl.BlockSpec((tm, d), cur),
            pl.BlockSpec((1, d), lambda i, f: (0, 0)),
            pl.BlockSpec((None, d, tf), lambda i, f: (layer, 0, tile(i, f))),
            pl.BlockSpec((None, d, tf), lambda i, f: (layer, 0, tile(i, f))),
            pl.BlockSpec((None, tf, d), lambda i, f: (layer, tile(i, f), 0)),
            pl.BlockSpec((1, d), lambda i, f: (0, 0)),
        ],
        out_specs=pl.BlockSpec((tm, d), prev),
        out_shape=jax.ShapeDtypeStruct((rows, d), F32),
        scratch_shapes=[pltpu.VMEM((tm, d), BF16), pltpu.VMEM((tm, d), F32), pltpu.VMEM((tm, d), F32)],
        compiler_params=pltpu.CompilerParams(
            dimension_semantics=("arbitrary", "arbitrary"),
            vmem_limit_bytes=_vmem_limit(pipelined, resident)),
        name="ffn",
    )(h, gpre, w_gate, w_up, w_down, gpost)


def _qkv_body(h_ref, gpre_ref, w_ref, b_ref, q_ref, k_ref, v_ref, *, tn, n_kv):
    d = h_ref.shape[1]
    xn = _rms(h_ref[...], gpre_ref[...]).astype(BF16)
    for j in range(d // tn):
        cq = slice(j * tn, (j + 1) * tn)
        q_ref[:, cq] = ((_mm(xn, w_ref[:, cq]) + b_ref[:, cq]) * (HEAD_DIM ** -0.5)).astype(BF16)
    y = _mm(xn, w_ref[:, d:]) + b_ref[:, d:]
    for hh in range(n_kv):
        k_ref[hh] = y[:, hh * HEAD_DIM:(hh + 1) * HEAD_DIM].astype(BF16)
        v_ref[hh] = y[:, (n_kv + hh) * HEAD_DIM:(n_kv + hh + 1) * HEAD_DIM].astype(BF16)


def _qkv(h, gpre, w_qkv, b_qkv, *, tm, tn, n_kv):
    rows, d = h.shape
    kv_shape = jax.ShapeDtypeStruct((n_kv, rows, HEAD_DIM), BF16)
    pipelined = (_nbytes((tm, d), F32) + _nbytes((tm, d), BF16)
                 + 2 * _nbytes((n_kv, tm, V7X_LANES), BF16))
    resident = _nbytes(w_qkv.shape, BF16) + _nbytes((tm, d), BF16) + 4 * _nbytes((tm, tn), F32)
    return pl.pallas_call(
        functools.partial(_qkv_body, tn=tn, n_kv=n_kv),
        grid=(rows // tm,),
        in_specs=[
            pl.BlockSpec((tm, d), lambda i: (i, 0)),
            _resident((1, d)), _resident(w_qkv.shape), _resident(b_qkv.shape),
        ],
        out_specs=[
            pl.BlockSpec((tm, d), lambda i: (i, 0)),
            pl.BlockSpec((n_kv, tm, HEAD_DIM), lambda i: (0, i, 0)),
            pl.BlockSpec((n_kv, tm, HEAD_DIM), lambda i: (0, i, 0)),
        ],
        out_shape=[jax.ShapeDtypeStruct((rows, d), BF16), kv_shape, kv_shape],
        compiler_params=pltpu.CompilerParams(
            dimension_semantics=("parallel",),
            vmem_limit_bytes=_vmem_limit(pipelined, resident)),
        name="qkv",
    )(h, gpre, w_qkv, b_qkv)


def _attn_body(q_ref, kp_ref, kc_ref, vp_ref, vc_ref, km_ref, vm_ref, tbl_ref, o_ref):
    qw = GROUP * HEAD_DIM
    zpad = jnp.zeros((SLOTS - BAND_SLOTS - N_META, HEAD_DIM), BF16)
    for hh in range(kp_ref.shape[0]):
        outs = []
        for sb in range(N_SUB):
            lo = sb * SUB
            band = lambda p_ref, c_ref: jnp.concatenate(
                [p_ref[hh, lo:BLOCK], c_ref[hh, 0:lo + SUB]], axis=0)
            kf = jnp.concatenate([band(kp_ref, kc_ref), km_ref[hh], zpad], axis=0)
            vf = jnp.concatenate([band(vp_ref, vc_ref), vm_ref[hh], zpad], axis=0)
            q = q_ref[lo:lo + SUB, hh * qw:(hh + 1) * qw]
            qs = jnp.concatenate([q[:, g * HEAD_DIM:(g + 1) * HEAD_DIM] for g in range(GROUP)], axis=0)
            s = lax.dot_general(qs, kf, (((1,), (1,)), ((), ())), preferred_element_type=F32)
            s = s + tbl_ref[0, hh, sb]
            m = jnp.max(s, axis=-1, keepdims=True)
            p = jnp.exp(s - m)
            l = jnp.sum(p, axis=-1, keepdims=True)
            outs.append(_mm(p.astype(BF16), vf) / l)
        o_ref[:, hh * qw:(hh + 1) * qw] = jnp.concatenate(
            [jnp.concatenate([o[g * SUB:(g + 1) * SUB] for o in outs], axis=0) for g in range(GROUP)],
            axis=1).astype(o_ref.dtype)


def _attention(q, k, v, k_meta, v_meta, tbl, *, batch, seq, hps):
    rows, d = q.shape
    n_kv = k.shape[0]
    n_blk = seq // BLOCK
    qw = hps * GROUP * HEAD_DIM
    row_blk = lambda h, b, n: b * n_blk + n
    prev_blk = lambda h, b, n: b * n_blk + jnp.maximum(n - 1, 0)
    kv_spec = lambda fn: pl.BlockSpec((hps, BLOCK, HEAD_DIM), lambda h, b, n: (h, fn(h, b, n), 0))
    meta_spec = pl.BlockSpec((hps, N_META, HEAD_DIM), lambda h, b, n: (h, 0, 0))
    pipelined = (2 * _nbytes((BLOCK, qw), BF16) + 4 * _nbytes((hps, BLOCK, V7X_LANES), BF16)
                 + _nbytes((hps, GROUP * BLOCK, SLOTS), F32))
    resident = 6 * hps * _nbytes((GROUP * BLOCK, SLOTS), F32)
    return pl.pallas_call(
        _attn_body,
        grid=(n_kv // hps, batch, n_blk),
        in_specs=[
            pl.BlockSpec((BLOCK, qw), lambda h, b, n: (row_blk(h, b, n), h)),
            kv_spec(prev_blk), kv_spec(row_blk), kv_spec(prev_blk), kv_spec(row_blk),
            meta_spec, meta_spec,
            pl.BlockSpec((1, hps, N_SUB, GROUP * SUB, SLOTS),
                         lambda h, b, n: (jnp.minimum(n, 1), h, 0, 0, 0)),
        ],
        out_specs=pl.BlockSpec((BLOCK, qw), lambda h, b, n: (row_blk(h, b, n), h)),
        out_shape=jax.ShapeDtypeStruct((rows, d), BF16),
        compiler_params=pltpu.CompilerParams(
            dimension_semantics=("parallel", "parallel", "parallel"),
            vmem_limit_bytes=_vmem_limit(pipelined, resident)),
        name="swa_attention",
    )(q, k, k, v, v, k_meta, v_meta, tbl)


def _oproj_body(a_ref, w_ref, b_ref, gpost_ref, h_ref, o_ref):
    z = _mm(a_ref[...], w_ref[...]) + b_ref[...]
    o_ref[...] = h_ref[...] + _rms(z, gpost_ref[...])


def _oproj(a, w_o, b_o, gpost, h, *, tm):
    rows, d = h.shape
    vec = lambda: pl.BlockSpec((1, d), lambda i: (0, 0))
    pipelined = _nbytes((tm, d), BF16) + 2 * _nbytes((tm, d), F32)
    resident = _nbytes((d, d), BF16) + 3 * _nbytes((tm, d), F32)
    return pl.pallas_call(
        _oproj_body,
        grid=(rows // tm,),
        in_specs=[
            pl.BlockSpec((tm, d), lambda i: (i, 0)),
            _resident((d, d)),
            vec(), vec(),
            pl.BlockSpec((tm, d), lambda i: (i, 0)),
        ],
        out_specs=pl.BlockSpec((tm, d), lambda i: (i, 0)),
        out_shape=jax.ShapeDtypeStruct((rows, d), F32),
        compiler_params=pltpu.CompilerParams(
            dimension_semantics=("parallel",),
            vmem_limit_bytes=_vmem_limit(pipelined, resident)),
        name="attn_out_proj",
    )(a, w_o, b_o, gpost, h)


def _t5_bucket(dist):
    max_exact = N_BUCKETS // 2
    dd = jnp.maximum(dist, max_exact).astype(F32)
    large = max_exact + (jnp.log(dd / max_exact) / math.log(MAX_DISTANCE / max_exact)
                         * (N_BUCKETS - max_exact)).astype(jnp.int32)
    return jnp.where(dist < max_exact, dist, jnp.minimum(large, N_BUCKETS - 1))


def _slot_table(rel_bias, sinks, n_kv):
    n_heads = rel_bias.shape[1]
    rb = rel_bias.astype(F32)

    def lookup(bucket):
        onehot = (bucket[..., None] == jnp.arange(N_BUCKETS)).astype(F32)
        return jnp.einsum("...b,bh->h...", onehot, rb, precision=lax.Precision.HIGHEST)

    qpos = jnp.arange(BLOCK)[:, None]
    kpos = jnp.arange(2 * BLOCK)[None, :]
    mpos = jnp.arange(N_META)[None, :]
    d_band = BLOCK + qpos - kpos
    in_window = (d_band >= 0) & (d_band < WINDOW)
    bias_band = lookup(_t5_bucket(jnp.maximum(d_band, 0)))
    sink = jnp.broadcast_to(sinks.astype(F32)[:, None, None], (n_heads, SUB, 1))
    pad = jnp.full((n_heads, SUB, SLOTS - BAND_SLOTS - N_META - 1), NEG_INF, F32)
    tables = []
    for n in (0, 1):
        valid = in_window & (n * BLOCK + kpos >= BLOCK)
        band = jnp.where(valid[None], bias_band, NEG_INF)
        meta = lookup(_t5_bucket(N_META + n * BLOCK + qpos - mpos))
        subs = []
        for sb in range(N_SUB):
            lo = sb * SUB
            subs.append(jnp.concatenate([band[:, lo:lo + SUB, lo:lo + BAND_SLOTS],
                                         meta[:, lo:lo + SUB], sink, pad], axis=2))
        t = jnp.stack(subs, axis=1).reshape(n_kv, GROUP, N_SUB, SUB, SLOTS)
        tables.append(jnp.transpose(t, (0, 2, 1, 3, 4)).reshape(n_kv, N_SUB, GROUP * SUB, SLOTS))
    return jnp.stack(tables)


def kernel(x, meta_tokens, rel_bias, conv_w_in, conv_b_in, conv_w_dw, conv_b_dw, conv_ln_g, conv_ln_b,
           conv_w_out, conv_b_out, attn_w_qkv, attn_b_qkv, attn_sinks, attn_w_o, attn_b_o,
           norm_mix_pre, norm_mix_post, norm_ffn_pre, norm_ffn_post, ffn_w_gate, ffn_w_up, ffn_w_down):
    batch, seq, d = x.shape
    n_kv = (attn_w_qkv.shape[2] - d) // (2 * HEAD_DIM)
    rows = batch * seq
    row = lambda a: a.reshape(1, -1).astype(F32)
    bf = functools.partial(_to_bf16, kb=CAST_ROWS)

    tm_mm, tn_mm, tm_conv, tm_ffn, tf = 512, 512, 256, 512, 512
    tm_meta = N_META

    h0 = x.reshape(rows, d)
    hm0 = meta_tokens.astype(x.dtype)
    w_gate, w_up, w_down = bf(ffn_w_gate), bf(ffn_w_up), bf(ffn_w_down)
    ffn = lambda h, layer, tm: _ffn(h, row(norm_ffn_pre[layer]), w_gate, w_up, w_down,
                                    row(norm_ffn_post[layer]), layer=layer, tm=tm, tf=tf)

    w_in, b_in = bf(conv_w_in)[0], row(conv_b_in[0])
    conv_args = (conv_w_dw[0].astype(F32), row(conv_b_dw[0]), row(conv_ln_g[0]), row(conv_ln_b[0]),
                 bf(conv_w_out)[0], row(conv_b_out[0]), row(norm_mix_post[0]))
    gpre0 = row(norm_mix_pre[0])

    vm = _conv_in(hm0, gpre0, w_in, b_in, tm=tm_meta, tn=tn_mm)
    zero_halo = jnp.zeros((HALO_ROWS, d), BF16)
    hm1 = _conv_out(vm, zero_halo, zero_halo, *conv_args, hm0, tm=tm_meta, rows_per_seq=N_META)
    hm2 = ffn(hm1, 0, tm_meta)

    v = _conv_in(h0, gpre0, w_in, b_in, tm=tm_mm, tn=tn_mm)
    halo0 = jnp.concatenate([jnp.zeros((HALO_ROWS - N_META, d), BF16), vm], axis=0)
    h1 = _conv_out(v, v, halo0, *conv_args, h0, tm=tm_conv, rows_per_seq=seq)
    h2 = ffn(h1, 0, tm_ffn)

    w_qkv, b_qkv = bf(attn_w_qkv)[0], row(attn_b_qkv[0])
    gpre1 = row(norm_mix_pre[1])
    _, k_meta, v_meta = _qkv(hm2, gpre1, w_qkv, b_qkv, tm=tm_meta, tn=tn_mm, n_kv=n_kv)
    q, k, vv = _qkv(h2, gpre1, w_qkv, b_qkv, tm=tm_mm, tn=tn_mm, n_kv=n_kv)
    tbl = _slot_table(rel_bias, attn_sinks[0], n_kv)
    a = _attention(q, k, vv, k_meta, v_meta, tbl, batch=batch, seq=seq, hps=ATTN_KV_HEADS_PER_STEP)
    h3 = _oproj(a, bf(attn_w_o)[0], row(attn_b_o[0]), row(norm_mix_post[1]), h2, tm=tm_mm)
    h4 = ffn(h3, 1, tm_ffn)
    return h4.reshape(batch, seq, d)
```

```python
import functools
import math

import jax
import jax.numpy as jnp
from jax import lax
from jax.experimental import pallas as pl
from jax.experimental.pallas import tpu as pltpu

N_META = 16
CONV_WIDTH = 31
HEAD_DIM = 64
GROUP = 8
WINDOW = 128
BLOCK = 128
N_BUCKETS = 32
MAX_DISTANCE = 128
RMS_EPS = 1e-6
LN_EPS = 1e-5
NEG_INF = -1e30

V7X_LANES = 128
V7X_SUBLANES = 8
V7X_BF16_SUBLANE_ROWS = 16
ATTN_KV_HEADS_PER_STEP = 2
CAST_ROWS = 256
CONV_ROW_CHUNK = 256
V7X_VMEM_BYTES = 64 * 1024 * 1024
V7X_VMEM_REQUEST_CAP = V7X_VMEM_BYTES - 6 * 1024 * 1024

HALO_ROWS = 2 * V7X_BF16_SUBLANE_ROWS
SUB = BLOCK // 2
N_SUB = BLOCK // SUB
BAND_SLOTS = BLOCK + SUB
SLOTS = 2 * V7X_LANES
BF16 = jnp.bfloat16
F32 = jnp.float32


def _vmem_limit(pipelined_bytes, resident_bytes):
    need = 2 * pipelined_bytes + resident_bytes
    return int(min(V7X_VMEM_REQUEST_CAP, need + need // 4 + (2 << 20)))


def _nbytes(shape, dtype):
    return math.prod(shape) * jnp.dtype(dtype).itemsize


def _rms(x, g):
    return x * lax.rsqrt(jnp.mean(x * x, axis=-1, keepdims=True) + RMS_EPS) * g


def _sigmoid(x):
    return 1.0 / (1.0 + jnp.exp(-x))


def _mm(a, b):
    return jnp.dot(a, b, preferred_element_type=F32)


def _cast_body(x_ref, o_ref):
    o_ref[...] = x_ref[...].astype(o_ref.dtype)


def _to_bf16(w, *, kb):
    n_l, k, n = w.shape
    blk = pl.BlockSpec((None, kb, n), lambda l, i: (l, i, 0))
    return pl.pallas_call(
        _cast_body,
        grid=(n_l, k // kb),
        in_specs=[blk],
        out_specs=blk,
        out_shape=jax.ShapeDtypeStruct(w.shape, BF16),
        compiler_params=pltpu.CompilerParams(
            dimension_semantics=("parallel", "parallel"),
            vmem_limit_bytes=_vmem_limit(_nbytes((kb, n), F32) + _nbytes((kb, n), BF16), 0)),
        name="cast_bf16",
    )(w)


def _resident(shape):
    return pl.BlockSpec(shape, lambda i: (0,) * len(shape), pipeline_mode=pl.Buffered(1))


def _conv_in_body(h_ref, gpre_ref, w_ref, b_ref, o_ref, *, tn):
    d = h_ref.shape[1]
    xn = _rms(h_ref[...], gpre_ref[...]).astype(BF16)
    for j in range(d // tn):
        ca, cg = slice(j * tn, (j + 1) * tn), slice(d + j * tn, d + (j + 1) * tn)
        a = _mm(xn, w_ref[:, ca]) + b_ref[:, ca]
        g = _mm(xn, w_ref[:, cg]) + b_ref[:, cg]
        o_ref[:, ca] = (a * _sigmoid(g)).astype(o_ref.dtype)


def _conv_in(h, gpre, w_in, b_in, *, tm, tn):
    rows, d = h.shape
    pipelined = _nbytes((tm, d), F32) + _nbytes((tm, d), BF16)
    resident = _nbytes(w_in.shape, BF16) + _nbytes((tm, d), BF16) + 6 * _nbytes((tm, tn), F32)
    return pl.pallas_call(
        functools.partial(_conv_in_body, tn=tn),
        grid=(rows // tm,),
        in_specs=[
            pl.BlockSpec((tm, d), lambda i: (i, 0)),
            _resident((1, d)), _resident(w_in.shape), _resident(b_in.shape),
        ],
        out_specs=pl.BlockSpec((tm, d), lambda i: (i, 0)),
        out_shape=jax.ShapeDtypeStruct((rows, d), BF16),
        compiler_params=pltpu.CompilerParams(
            dimension_semantics=("parallel",),
            vmem_limit_bytes=_vmem_limit(pipelined, resident)),
        name="conv_in",
    )(h, gpre, w_in, b_in)


def _conv_out_body(v_ref, vprev_ref, halo0_ref, wdw_ref, bdw_ref, lng_ref, lnb_ref, wout_ref,
                   bout_ref, gpost_ref, h_ref, o_ref, xbuf_ref, u_ref, *, tm, rt, blocks_per_seq):
    n_slab, _, cw = xbuf_ref.shape
    first = pl.program_id(0) % blocks_per_seq == 0

    def fill_halo(src_ref):
        for c in range(n_slab):
            xbuf_ref[c, pl.ds(0, HALO_ROWS, stride=2), :] = src_ref[:, c * cw:(c + 1) * cw].astype(F32)

    pl.when(first)(functools.partial(fill_halo, halo0_ref))
    pl.when(jnp.logical_not(first))(functools.partial(fill_halo, vprev_ref))
    for c in range(n_slab):
        xbuf_ref[c, pl.ds(2 * HALO_ROWS, tm, stride=2), :] = v_ref[:, c * cw:(c + 1) * cw].astype(F32)

    def chunk(c, carry):
        ls = pl.ds(pl.multiple_of(c * cw, cw), cw)
        for r0 in range(0, tm, rt):
            acc = jnp.broadcast_to(bdw_ref[:, ls], (rt, cw)).reshape(rt // V7X_SUBLANES, V7X_SUBLANES, cw)
            for m in range(CONV_WIDTH):
                xs = xbuf_ref[c, pl.ds(2 * (HALO_ROWS + r0 - m), rt, stride=2), :]
                acc = acc + xs.reshape(acc.shape) * wdw_ref[c, CONV_WIDTH - 1 - m][None]
            u_ref[r0:r0 + rt, ls] = acc.reshape(rt, cw)
        return carry

    lax.fori_loop(0, n_slab, chunk, 0)

    u = u_ref[...]
    mu = jnp.mean(u, axis=-1, keepdims=True)
    uc = u - mu
    var = jnp.mean(uc * uc, axis=-1, keepdims=True)
    y = uc * lax.rsqrt(var + LN_EPS) * lng_ref[...] + lnb_ref[...]
    y = y * _sigmoid(y)
    z = _mm(y.astype(BF16), wout_ref[...]) + bout_ref[...]
    o_ref[...] = h_ref[...] + _rms(z, gpost_ref[...])


def _conv_out(v, vprev_src, halo0, w_dw, b_dw, ln_g, ln_b, w_out, b_out, gpost, h, *, tm,
              rows_per_seq):
    rows, d = h.shape
    cw = V7X_LANES
    n_slab = d // cw
    body = functools.partial(_conv_out_body, tm=tm, rt=min(tm, CONV_ROW_CHUNK),
                             blocks_per_seq=rows_per_seq // tm)
    halo_blocks_per_tile = tm // HALO_ROWS
    w_slabs = jnp.broadcast_to(jnp.transpose(w_dw.reshape(CONV_WIDTH, n_slab, cw), (1, 0, 2))[:, :, None, :],
                               (n_slab, CONV_WIDTH, V7X_SUBLANES, cw))
    vec = lambda: pl.BlockSpec((1, d), lambda i: (0, 0))
    pipelined = (_nbytes((tm, d), BF16) + 2 * _nbytes((HALO_ROWS, d), BF16)
                 + _nbytes((HALO_ROWS, d), F32) + 2 * _nbytes((tm, d), F32))
    resident = (_nbytes((d, d), BF16) + 2 * _nbytes((tm + HALO_ROWS, d), F32)
                + 4 * _nbytes((tm, d), F32))
    return pl.pallas_call(
        body,
        grid=(rows // tm,),
        in_specs=[
            pl.BlockSpec((tm, d), lambda i: (i, 0)),
            pl.BlockSpec((HALO_ROWS, d), lambda i: (jnp.maximum(i * halo_blocks_per_tile - 1, 0), 0)),
            pl.BlockSpec((HALO_ROWS, d), lambda i: (0, 0)),
            _resident((n_slab, CONV_WIDTH, V7X_SUBLANES, cw)),
            vec(), vec(), vec(),
            _resident((d, d)),
            vec(), vec(),
            pl.BlockSpec((tm, d), lambda i: (i, 0)),
        ],
        out_specs=pl.BlockSpec((tm, d), lambda i: (i, 0)),
        out_shape=jax.ShapeDtypeStruct((rows, d), F32),
        scratch_shapes=[pltpu.VMEM((n_slab, 2 * (tm + HALO_ROWS), cw), F32), pltpu.VMEM((tm, d), F32)],
        compiler_params=pltpu.CompilerParams(
            dimension_semantics=("parallel",),
            vmem_limit_bytes=_vmem_limit(pipelined, resident)),
        name="conv_out",
    )(v, vprev_src, halo0, w_slabs, b_dw, ln_g, ln_b, w_out, b_out, gpost, h)


def _ffn_body(h_ref, gpre_ref, wg_ref, wu_ref, wd_ref, gpost_ref, o_ref, xn_ref, acc_ref, hkeep_ref,
              *, n_blocks):
    i, f = pl.program_id(0), pl.program_id(1)

    def down_partial(xn):
        g = _mm(xn, wg_ref[...])
        u = _mm(xn, wu_ref[...])
        return _mm((g * _sigmoid(g) * u).astype(BF16), wd_ref[...])

    @pl.when((i == 0) & (f == 0))
    def _():
        acc_ref[...] = jnp.zeros_like(acc_ref)
        hkeep_ref[...] = jnp.zeros_like(hkeep_ref)

    @pl.when(f == 0)
    def _():
        o_ref[...] = hkeep_ref[...] + _rms(acc_ref[...], gpost_ref[...])
        h = h_ref[...]
        hkeep_ref[...] = h
        xn = _rms(h, gpre_ref[...]).astype(BF16)
        xn_ref[...] = xn
        acc_ref[...] = down_partial(xn)

    @pl.when((f > 0) & (i < n_blocks))
    def _():
        acc_ref[...] += down_partial(xn_ref[...])


def _ffn(h, gpre, w_gate, w_up, w_down, gpost, *, layer, tm, tf):
    rows, d = h.shape
    ff = w_gate.shape[2]
    n_blocks = rows // tm
    cur = lambda i, f: (jnp.minimum(i, n_blocks - 1), 0)
    prev = lambda i, f: (jnp.maximum(i - 1, 0), 0)
    tile = lambda i, f: jnp.where(i < n_blocks, f, 0)
    pipelined = 2 * _nbytes((tm, d), F32) + 3 * _nbytes((d, tf), BF16)
    resident = _nbytes((tm, d), BF16) + 2 * _nbytes((tm, d), F32) + 4 * _nbytes((tm, tf), F32)
    return pl.pallas_call(
        functools.partial(_ffn_body, n_blocks=n_blocks),
        grid=(n_blocks + 1, ff // tf),
        in_specs=[
            pl.BlockSpec((tm, d), cur),
            pl.BlockSpec((1, d), lambda i, f: (0, 0)),
            pl.BlockSpec((None, d, tf), lambda i, f: (layer, 0, tile(i, f))),
            pl.BlockSpec((None, d, tf), lambda i, f: (layer, 0, tile(i, f))),
            pl.BlockSpec((None, tf, d), lambda i, f: (layer, tile(i, f), 0)),
            pl.BlockSpec((1, d), lambda i, f: (0, 0)),
        ],
        out_specs=pl.BlockSpec((tm, d), prev),
        out_shape=jax.ShapeDtypeStruct((rows, d), F32),
        scratch_shapes=[pltpu.VMEM((tm, d), BF16), pltpu.VMEM((tm, d), F32), pltpu.VMEM((tm, d), F32)],
        compiler_params=pltpu.CompilerParams(
            dimension_semantics=("arbitrary", "arbitrary"),
            vmem_limit_bytes=_vmem_limit(pipelined, resident)),
        name="ffn",
    )(h, gpre, w_gate, w_up, w_down, gpost)


def _qkv_body(h_ref, gpre_ref, w_ref, b_ref, q_ref, k_ref, v_ref, *, tn, n_kv):
    d = h_ref.shape[1]
    xn = _rms(h_ref[...], gpre_ref[...]).astype(BF16)
    for j in range(d // tn):
        cq = slice(j * tn, (j + 1) * tn)
        q_ref[:, cq] = ((_mm(xn, w_ref[:, cq]) + b_ref[:, cq]) * (HEAD_DIM ** -0.5)).astype(BF16)
    y = _mm(xn, w_ref[:, d:]) + b_ref[:, d:]
    for hh in range(n_kv):
        k_ref[hh] = y[:, hh * HEAD_DIM:(hh + 1) * HEAD_DIM].astype(BF16)
        v_ref[hh] = y[:, (n_kv + hh) * HEAD_DIM:(n_kv + hh + 1) * HEAD_DIM].astype(BF16)


def _qkv(h, gpre, w_qkv, b_qkv, *, tm, tn, n_kv):
    rows, d = h.shape
    kv_shape = jax.ShapeDtypeStruct((n_kv, rows, HEAD_DIM), BF16)
    pipelined = (_nbytes((tm, d), F32) + _nbytes((tm, d), BF16)
                 + 2 * _nbytes((n_kv, tm, V7X_LANES), BF16))
    resident = _nbytes(w_qkv.shape, BF16) + _nbytes((tm, d), BF16) + 4 * _nbytes((tm, tn), F32)
    return pl.pallas_call(
        functools.partial(_qkv_body, tn=tn, n_kv=n_kv),
        grid=(rows // tm,),
        in_specs=[
            pl.BlockSpec((tm, d), lambda i: (i, 0)),
            _resident((1, d)), _resident(w_qkv.shape), _resident(b_qkv.shape),
        ],
        out_specs=[
            pl.BlockSpec((tm, d), lambda i: (i, 0)),
            pl.BlockSpec((n_kv, tm, HEAD_DIM), lambda i: (0, i, 0)),
            pl.BlockSpec((n_kv, tm, HEAD_DIM), lambda i: (0, i, 0)),
        ],
        out_shape=[jax.ShapeDtypeStruct((rows, d), BF16), kv_shape, kv_shape],
        compiler_params=pltpu.CompilerParams(
            dimension_semantics=("parallel",),
            vmem_limit_bytes=_vmem_limit(pipelined, resident)),
        name="qkv",
    )(h, gpre, w_qkv, b_qkv)


def _attn_body(q_ref, kp_ref, kc_ref, vp_ref, vc_ref, km_ref, vm_ref, tbl_ref, o_ref):
    qw = GROUP * HEAD_DIM
    zpad = jnp.zeros((SLOTS - BAND_SLOTS - N_META, HEAD_DIM), BF16)
    for hh in range(kp_ref.shape[0]):
        outs = []
        for sb in range(N_SUB):
            lo = sb * SUB
            band = lambda p_ref, c_ref: jnp.concatenate(
                [p_ref[hh, lo:BLOCK], c_ref[hh, 0:lo + SUB]], axis=0)
            kf = jnp.concatenate([band(kp_ref, kc_ref), km_ref[hh], zpad], axis=0)
            vf = jnp.concatenate([band(vp_ref, vc_ref), vm_ref[hh], zpad], axis=0)
            q = q_ref[lo:lo + SUB, hh * qw:(hh + 1) * qw]
            qs = jnp.concatenate([q[:, g * HEAD_DIM:(g + 1) * HEAD_DIM] for g in range(GROUP)], axis=0)
            s = lax.dot_general(qs, kf, (((1,), (1,)), ((), ())), preferred_element_type=F32)
            s = s + tbl_ref[0, hh, sb]
            m = jnp.max(s, axis=-1, keepdims=True)
            p = jnp.exp(s - m)
            l = jnp.sum(p, axis=-1, keepdims=True)
            outs.append(_mm(p.astype(BF16), vf) / l)
        o_ref[:, hh * qw:(hh + 1) * qw] = jnp.concatenate(
            [jnp.concatenate([o[g * SUB:(g + 1) * SUB] for o in outs], axis=0) for g in range(GROUP)],
            axis=1).astype(o_ref.dtype)


def _attention(q, k, v, k_meta, v_meta, tbl, *, batch, seq, hps):
    rows, d = q.shape
    n_kv = k.shape[0]
    n_blk = seq // BLOCK
    qw = hps * GROUP * HEAD_DIM
    row_blk = lambda h, b, n: b * n_blk + n
    prev_blk = lambda h, b, n: b * n_blk + jnp.maximum(n - 1, 0)
    kv_spec = lambda fn: pl.BlockSpec((hps, BLOCK, HEAD_DIM), lambda h, b, n: (h, fn(h, b, n), 0))
    meta_spec = pl.BlockSpec((hps, N_META, HEAD_DIM), lambda h, b, n: (h, 0, 0))
    pipelined = (2 * _nbytes((BLOCK, qw), BF16) + 4 * _nbytes((hps, BLOCK, V7X_LANES), BF16)
                 + _nbytes((hps, GROUP * BLOCK, SLOTS), F32))
    resident = 6 * hps * _nbytes((GROUP * BLOCK, SLOTS), F32)
    return pl.pallas_call(
        _attn_body,
        grid=(n_kv // hps, batch, n_blk),
        in_specs=[
            pl.BlockSpec((BLOCK, qw), lambda h, b, n: (row_blk(h, b, n), h)),
            kv_spec(prev_blk), kv_spec(row_blk), kv_spec(prev_blk), kv_spec(row_blk),
            meta_spec, meta_spec,
            pl.BlockSpec((1, hps, N_SUB, GROUP * SUB, SLOTS),
                         lambda h, b, n: (jnp.minimum(n, 1), h, 0, 0, 0)),
        ],
        out_specs=pl.BlockSpec((BLOCK, qw), lambda h, b, n: (row_blk(h, b, n), h)),
        out_shape=jax.ShapeDtypeStruct((rows, d), BF16),
        compiler_params=pltpu.CompilerParams(
            dimension_semantics=("parallel", "parallel", "parallel"),
            vmem_limit_bytes=_vmem_limit(pipelined, resident)),
        name="swa_attention",
    )(q, k, k, v, v, k_meta, v_meta, tbl)


def _oproj_body(a_ref, w_ref, b_ref, gpost_ref, h_ref, o_ref):
    z = _mm(a_ref[...], w_ref[...]) + b_ref[...]
    o_ref[...] = h_ref[...] + _rms(z, gpost_ref[...])


def _oproj(a, w_o, b_o, gpost, h, *, tm):
    rows, d = h.shape
    vec = lambda: pl.BlockSpec((1, d), lambda i: (0, 0))
    pipelined = _nbytes((tm, d), BF16) + 2 * _nbytes((tm, d), F32)
    resident = _nbytes((d, d), BF16) + 3 * _nbytes((tm, d), F32)
    return pl.pallas_call(
        _oproj_body,
        grid=(rows // tm,),
        in_specs=[
            pl.BlockSpec((tm, d), lambda i: (i, 0)),
            _resident((d, d)),
            vec(), vec(),
            pl.BlockSpec((tm, d), lambda i: (i, 0)),
        ],
        out_specs=pl.BlockSpec((tm, d), lambda i: (i, 0)),
        out_shape=jax.ShapeDtypeStruct((rows, d), F32),
        compiler_params=pltpu.CompilerParams(
            dimension_semantics=("parallel",),
            vmem_limit_bytes=_vmem_limit(pipelined, resident)),
        name="attn_out_proj",
    )(a, w_o, b_o, gpost, h)


def _t5_bucket(dist):
    max_exact = N_BUCKETS // 2
    dd = jnp.maximum(dist, max_exact).astype(F32)
    large = max_exact + (jnp.log(dd / max_exact) / math.log(MAX_DISTANCE / max_exact)
                         * (N_BUCKETS - max_exact)).astype(jnp.int32)
    return jnp.where(dist < max_exact, dist, jnp.minimum(large, N_BUCKETS - 1))


def _slot_table(rel_bias, sinks, n_kv):
    n_heads = rel_bias.shape[1]
    rb = rel_bias.astype(F32)

    def lookup(bucket):
        onehot = (bucket[..., None] == jnp.arange(N_BUCKETS)).astype(F32)
        return jnp.einsum("...b,bh->h...", onehot, rb, precision=lax.Precision.HIGHEST)

    qpos = jnp.arange(BLOCK)[:, None]
    kpos = jnp.arange(2 * BLOCK)[None, :]
    mpos = jnp.arange(N_META)[None, :]
    d_band = BLOCK + qpos - kpos
    in_window = (d_band >= 0) & (d_band < WINDOW)
    bias_band = lookup(_t5_bucket(jnp.maximum(d_band, 0)))
    sink = jnp.broadcast_to(sinks.astype(F32)[:, None, None], (n_heads, SUB, 1))
    pad = jnp.full((n_heads, SUB, SLOTS - BAND_SLOTS - N_META - 1), NEG_INF, F32)
    tables = []
    for n in (0, 1):
        valid = in_window & (n * BLOCK + kpos >= BLOCK)
        band = jnp.where(valid[None], bias_band, NEG_INF)
        meta = lookup(_t5_bucket(N_META + n * BLOCK + qpos - mpos))
        subs = []
        for sb in range(N_SUB):
            lo = sb * SUB
            subs.append(jnp.concatenate([band[:, lo:lo + SUB, lo:lo + BAND_SLOTS],
                                         meta[:, lo:lo + SUB], sink, pad], axis=2))
        t = jnp.stack(subs, axis=1).reshape(n_kv, GROUP, N_SUB, SUB, SLOTS)
        tables.append(jnp.transpose(t, (0, 2, 1, 3, 4)).reshape(n_kv, N_SUB, GROUP * SUB, SLOTS))
    return jnp.stack(tables)


def kernel(x, meta_tokens, rel_bias, conv_w_in, conv_b_in, conv_w_dw, conv_b_dw, conv_ln_g, conv_ln_b,
           conv_w_out, conv_b_out, attn_w_qkv, attn_b_qkv, attn_sinks, attn_w_o, attn_b_o,
           norm_mix_pre, norm_mix_post, norm_ffn_pre, norm_ffn_post, ffn_w_gate, ffn_w_up, ffn_w_down):
    batch, seq, d = x.shape
    n_kv = (attn_w_qkv.shape[2] - d) // (2 * HEAD_DIM)
    rows = batch * seq
    row = lambda a: a.reshape(1, -1).astype(F32)
    bf = functools.partial(_to_bf16, kb=CAST_ROWS)

    tm_mm, tn_mm, tm_conv, tm_ffn, tf = 512, 512, 512, 512, 512
    tm_meta = N_META

    h0 = x.reshape(rows, d)
    hm0 = meta_tokens.astype(x.dtype)
    w_gate, w_up, w_down = bf(ffn_w_gate), bf(ffn_w_up), bf(ffn_w_down)
    ffn = lambda h, layer, tm: _ffn(h, row(norm_ffn_pre[layer]), w_gate, w_up, w_down,
                                    row(norm_ffn_post[layer]), layer=layer, tm=tm, tf=tf)

    w_in, b_in = bf(conv_w_in)[0], row(conv_b_in[0])
    conv_args = (conv_w_dw[0].astype(F32), row(conv_b_dw[0]), row(conv_ln_g[0]), row(conv_ln_b[0]),
                 bf(conv_w_out)[0], row(conv_b_out[0]), row(norm_mix_post[0]))
    gpre0 = row(norm_mix_pre[0])

    vm = _conv_in(hm0, gpre0, w_in, b_in, tm=tm_meta, tn=tn_mm)
    zero_halo = jnp.zeros((HALO_ROWS, d), BF16)
    hm1 = _conv_out(vm, zero_halo, zero_halo, *conv_args, hm0, tm=tm_meta, rows_per_seq=N_META)
    hm2 = ffn(hm1, 0, tm_meta)

    v = _conv_in(h0, gpre0, w_in, b_in, tm=tm_mm, tn=tn_mm)
    halo0 = jnp.concatenate([jnp.zeros((HALO_ROWS - N_META, d), BF16), vm], axis=0)
    h1 = _conv_out(v, v, halo0, *conv_args, h0, tm=tm_conv, rows_per_seq=seq)
    h2 = ffn(h1, 0, tm_ffn)

    w_qkv, b_qkv = bf(attn_w_qkv)[0], row(attn_b_qkv[0])
    gpre1 = row(norm_mix_pre[1])
    _, k_meta, v_meta = _qkv(hm2, gpre1, w_qkv, b_qkv, tm=tm_meta, tn=tn_mm, n_kv=n_kv)
    q, k, vv = _qkv(h2, gpre1, w_qkv, b_qkv, tm=tm_mm, tn=tn_mm, n_kv=n_kv)
    tbl = _slot_table(rel_bias, attn_sinks[0], n_kv)
    a = _attention(q, k, vv, k_meta, v_meta, tbl, batch=batch, seq=seq, hps=ATTN_KV_HEADS_PER_STEP)
    h3 = _oproj(a, bf(attn_w_o)[0], row(attn_b_o[0]), row(norm_mix_post[1]), h2, tm=tm_mm)
    h4 = ffn(h3, 1, tm_ffn)
    return h4.reshape(batch, seq, d)
```

```python
import functools
import math

import jax
import jax.numpy as jnp
from jax import lax
from jax.experimental import pallas as pl
from jax.experimental.pallas import tpu as pltpu

N_META = 16
CONV_WIDTH = 31
HEAD_DIM = 64
GROUP = 8
WINDOW = 128
BLOCK = 128
N_BUCKETS = 32
MAX_DISTANCE = 128
RMS_EPS = 1e-6
LN_EPS = 1e-5
NEG_INF = -1e30

V7X_LANES = 128
V7X_SUBLANES = 8
V7X_BF16_SUBLANE_ROWS = 16
ATTN_KV_HEADS_PER_STEP = 2
CAST_ROWS = 256
FFN_RESULT_COLS = 256
FFN_NORM_ROWS = 128
CONV_ROW_CHUNK = 256
V7X_VMEM_BYTES = 64 * 1024 * 1024
V7X_VMEM_REQUEST_CAP = V7X_VMEM_BYTES - 6 * 1024 * 1024

HALO_ROWS = 2 * V7X_BF16_SUBLANE_ROWS
SUB = BLOCK // 2
N_SUB = BLOCK // SUB
BAND_SLOTS = BLOCK + SUB
SLOTS = 2 * V7X_LANES
BF16 = jnp.bfloat16
F32 = jnp.float32


def _vmem_limit(pipelined_bytes, resident_bytes):
    need = 2 * pipelined_bytes + resident_bytes
    return int(min(V7X_VMEM_REQUEST_CAP, need + need // 4 + (2 << 20)))


def _nbytes(shape, dtype):
    return math.prod(shape) * jnp.dtype(dtype).itemsize


def _rms(x, g):
    return x * lax.rsqrt(jnp.mean(x * x, axis=-1, keepdims=True) + RMS_EPS) * g


def _sigmoid(x):
    return 1.0 / (1.0 + jnp.exp(-x))


def _mm(a, b):
    return jnp.dot(a, b, preferred_element_type=F32)


def _cast_body(x_ref, o_ref):
    o_ref[...] = x_ref[...].astype(o_ref.dtype)


def _to_bf16(w, *, kb):
    n_l, k, n = w.shape
    blk = pl.BlockSpec((None, kb, n), lambda l, i: (l, i, 0))
    return pl.pallas_call(
        _cast_body,
        grid=(n_l, k // kb),
        in_specs=[blk],
        out_specs=blk,
        out_shape=jax.ShapeDtypeStruct(w.shape, BF16),
        compiler_params=pltpu.CompilerParams(
            dimension_semantics=("parallel", "parallel"),
            vmem_limit_bytes=_vmem_limit(_nbytes((kb, n), F32) + _nbytes((kb, n), BF16), 0)),
        name="cast_bf16",
    )(w)


def _resident(shape):
    return pl.BlockSpec(shape, lambda i: (0,) * len(shape), pipeline_mode=pl.Buffered(1))


def _conv_in_body(h_ref, gpre_ref, w_ref, b_ref, o_ref, *, tn):
    d = h_ref.shape[1]
    xn = _rms(h_ref[...], gpre_ref[...]).astype(BF16)
    for j in range(d // tn):
        ca, cg = slice(j * tn, (j + 1) * tn), slice(d + j * tn, d + (j + 1) * tn)
        a = _mm(xn, w_ref[:, ca]) + b_ref[:, ca]
        g = _mm(xn, w_ref[:, cg]) + b_ref[:, cg]
        o_ref[:, ca] = (a * _sigmoid(g)).astype(o_ref.dtype)


def _conv_in(h, gpre, w_in, b_in, *, tm, tn):
    rows, d = h.shape
    pipelined = _nbytes((tm, d), F32) + _nbytes((tm, d), BF16)
    resident = _nbytes(w_in.shape, BF16) + _nbytes((tm, d), BF16) + 6 * _nbytes((tm, tn), F32)
    return pl.pallas_call(
        functools.partial(_conv_in_body, tn=tn),
        grid=(rows // tm,),
        in_specs=[
            pl.BlockSpec((tm, d), lambda i: (i, 0)),
            _resident((1, d)), _resident(w_in.shape), _resident(b_in.shape),
        ],
        out_specs=pl.BlockSpec((tm, d), lambda i: (i, 0)),
        out_shape=jax.ShapeDtypeStruct((rows, d), BF16),
        compiler_params=pltpu.CompilerParams(
            dimension_semantics=("parallel",),
            vmem_limit_bytes=_vmem_limit(pipelined, resident)),
        name="conv_in",
    )(h, gpre, w_in, b_in)


def _conv_out_body(v_ref, vprev_ref, halo0_ref, wdw_ref, bdw_ref, lng_ref, lnb_ref, wout_ref,
                   bout_ref, gpost_ref, h_ref, o_ref, xbuf_ref, u_ref, *, tm, rt, blocks_per_seq):
    n_slab, _, cw = xbuf_ref.shape
    first = pl.program_id(0) % blocks_per_seq == 0

    def fill_halo(src_ref):
        for c in range(n_slab):
            xbuf_ref[c, pl.ds(0, HALO_ROWS, stride=2), :] = src_ref[:, c * cw:(c + 1) * cw].astype(F32)

    pl.when(first)(functools.partial(fill_halo, halo0_ref))
    pl.when(jnp.logical_not(first))(functools.partial(fill_halo, vprev_ref))
    for c in range(n_slab):
        xbuf_ref[c, pl.ds(2 * HALO_ROWS, tm, stride=2), :] = v_ref[:, c * cw:(c + 1) * cw].astype(F32)

    def chunk(c, carry):
        ls = pl.ds(pl.multiple_of(c * cw, cw), cw)
        for r0 in range(0, tm, rt):
            acc = jnp.broadcast_to(bdw_ref[:, ls], (rt, cw)).reshape(rt // V7X_SUBLANES, V7X_SUBLANES, cw)
            for m in range(CONV_WIDTH):
                xs = xbuf_ref[c, pl.ds(2 * (HALO_ROWS + r0 - m), rt, stride=2), :]
                acc = acc + xs.reshape(acc.shape) * wdw_ref[c, CONV_WIDTH - 1 - m][None]
            u_ref[r0:r0 + rt, ls] = acc.reshape(rt, cw)
        return carry

    lax.fori_loop(0, n_slab, chunk, 0)

    u = u_ref[...]
    mu = jnp.mean(u, axis=-1, keepdims=True)
    uc = u - mu
    var = jnp.mean(uc * uc, axis=-1, keepdims=True)
    y = uc * lax.rsqrt(var + LN_EPS) * lng_ref[...] + lnb_ref[...]
    y = y * _sigmoid(y)
    z = _mm(y.astype(BF16), wout_ref[...]) + bout_ref[...]
    o_ref[...] = h_ref[...] + _rms(z, gpost_ref[...])


def _conv_out(v, vprev_src, halo0, w_dw, b_dw, ln_g, ln_b, w_out, b_out, gpost, h, *, tm,
              rows_per_seq):
    rows, d = h.shape
    cw = V7X_LANES
    n_slab = d // cw
    body = functools.partial(_conv_out_body, tm=tm, rt=min(tm, CONV_ROW_CHUNK),
                             blocks_per_seq=rows_per_seq // tm)
    halo_blocks_per_tile = tm // HALO_ROWS
    w_slabs = jnp.broadcast_to(jnp.transpose(w_dw.reshape(CONV_WIDTH, n_slab, cw), (1, 0, 2))[:, :, None, :],
                               (n_slab, CONV_WIDTH, V7X_SUBLANES, cw))
    vec = lambda: pl.BlockSpec((1, d), lambda i: (0, 0))
    pipelined = (_nbytes((tm, d), BF16) + 2 * _nbytes((HALO_ROWS, d), BF16)
                 + _nbytes((HALO_ROWS, d), F32) + 2 * _nbytes((tm, d), F32))
    resident = (_nbytes((d, d), BF16) + 2 * _nbytes((tm + HALO_ROWS, d), F32)
                + 4 * _nbytes((tm, d), F32))
    return pl.pallas_call(
        body,
        grid=(rows // tm,),
        in_specs=[
            pl.BlockSpec((tm, d), lambda i: (i, 0)),
            pl.BlockSpec((HALO_ROWS, d), lambda i: (jnp.maximum(i * halo_blocks_per_tile - 1, 0), 0)),
            pl.BlockSpec((HALO_ROWS, d), lambda i: (0, 0)),
            _resident((n_slab, CONV_WIDTH, V7X_SUBLANES, cw)),
            vec(), vec(), vec(),
            _resident((d, d)),
            vec(), vec(),
            pl.BlockSpec((tm, d), lambda i: (i, 0)),
        ],
        out_specs=pl.BlockSpec((tm, d), lambda i: (i, 0)),
        out_shape=jax.ShapeDtypeStruct((rows, d), F32),
        scratch_shapes=[pltpu.VMEM((n_slab, 2 * (tm + HALO_ROWS), cw), F32), pltpu.VMEM((tm, d), F32)],
        compiler_params=pltpu.CompilerParams(
            dimension_semantics=("parallel",),
            vmem_limit_bytes=_vmem_limit(pipelined, resident)),
        name="conv_out",
    )(v, vprev_src, halo0, w_slabs, b_dw, ln_g, ln_b, w_out, b_out, gpost, h)


def _ffn_body(h_ref, gpre_ref, wg_ref, wu_ref, wd_ref, gpost_ref, o_ref, xn_ref, *, tc, tr):
    f = pl.program_id(1)
    tf, d = wd_ref.shape
    row_groups = [slice(r, r + tr) for r in range(0, o_ref.shape[0], tr)]

    @pl.when(f == 0)
    def _():
        for rs in row_groups:
            xn_ref[rs, :] = _rms(h_ref[rs, :], gpre_ref[...]).astype(BF16)
        o_ref[...] = jnp.zeros_like(o_ref)

    xn = xn_ref[...]
    acts = []
    for c in range(0, tf, tc):
        g = _mm(xn, wg_ref[:, c:c + tc])
        u = _mm(xn, wu_ref[:, c:c + tc])
        acts.append((g * _sigmoid(g) * u).astype(BF16))
    a = jnp.concatenate(acts, axis=1)
    for c in range(0, d, tc):
        o_ref[:, c:c + tc] += _mm(a, wd_ref[:, c:c + tc])

    @pl.when(f == pl.num_programs(1) - 1)
    def _():
        for rs in row_groups:
            o_ref[rs, :] = h_ref[rs, :] + _rms(o_ref[rs, :], gpost_ref[...])


def _ffn(h, gpre, w_gate, w_up, w_down, gpost, *, layer, tm, tf):
    rows, d = h.shape
    ff = w_gate.shape[2]
    tc = min(tf, FFN_RESULT_COLS)
    pipelined = 2 * _nbytes((tm, d), F32) + 3 * _nbytes((d, tf), BF16)
    resident = _nbytes((tm, d), BF16) + _nbytes((tm, tf), BF16) + 6 * _nbytes((tm, tc), F32)
    return pl.pallas_call(
        functools.partial(_ffn_body, tc=tc, tr=min(tm, FFN_NORM_ROWS)),
        grid=(rows // tm, ff // tf),
        in_specs=[
            pl.BlockSpec((tm, d), lambda i, f: (i, 0)),
            pl.BlockSpec((1, d), lambda i, f: (0, 0)),
            pl.BlockSpec((None, d, tf), lambda i, f: (layer, 0, f)),
            pl.BlockSpec((None, d, tf), lambda i, f: (layer, 0, f)),
            pl.BlockSpec((None, tf, d), lambda i, f: (layer, f, 0)),
            pl.BlockSpec((1, d), lambda i, f: (0, 0)),
        ],
        out_specs=pl.BlockSpec((tm, d), lambda i, f: (i, 0)),
        out_shape=jax.ShapeDtypeStruct((rows, d), F32),
        scratch_shapes=[pltpu.VMEM((tm, d), BF16)],
        compiler_params=pltpu.CompilerParams(
            dimension_semantics=("parallel", "arbitrary"),
            vmem_limit_bytes=_vmem_limit(pipelined, resident)),
        name="ffn",
    )(h, gpre, w_gate, w_up, w_down, gpost)


def _qkv_body(h_ref, gpre_ref, w_ref, b_ref, q_ref, k_ref, v_ref, *, tn, n_kv):
    d = h_ref.shape[1]
    xn = _rms(h_ref[...], gpre_ref[...]).astype(BF16)
    for j in range(d // tn):
        cq = slice(j * tn, (j + 1) * tn)
        q_ref[:, cq] = ((_mm(xn, w_ref[:, cq]) + b_ref[:, cq]) * (HEAD_DIM ** -0.5)).astype(BF16)
    y = _mm(xn, w_ref[:, d:]) + b_ref[:, d:]
    for hh in range(n_kv):
        k_ref[hh] = y[:, hh * HEAD_DIM:(hh + 1) * HEAD_DIM].astype(BF16)
        v_ref[hh] = y[:, (n_kv + hh) * HEAD_DIM:(n_kv + hh + 1) * HEAD_DIM].astype(BF16)


def _qkv(h, gpre, w_qkv, b_qkv, *, tm, tn, n_kv):
    rows, d = h.shape
    kv_shape = jax.ShapeDtypeStruct((n_kv, rows, HEAD_DIM), BF16)
    pipelined = (_nbytes((tm, d), F32) + _nbytes((tm, d), BF16)
                 + 2 * _nbytes((n_kv, tm, V7X_LANES), BF16))
    resident = _nbytes(w_qkv.shape, BF16) + _nbytes((tm, d), BF16) + 4 * _nbytes((tm, tn), F32)
    return pl.pallas_call(
        functools.partial(_qkv_body, tn=tn, n_kv=n_kv),
        grid=(rows // tm,),
        in_specs=[
            pl.BlockSpec((tm, d), lambda i: (i, 0)),
            _resident((1, d)), _resident(w_qkv.shape), _resident(b_qkv.shape),
        ],
        out_specs=[
            pl.BlockSpec((tm, d), lambda i: (i, 0)),
            pl.BlockSpec((n_kv, tm, HEAD_DIM), lambda i: (0, i, 0)),
            pl.BlockSpec((n_kv, tm, HEAD_DIM), lambda i: (0, i, 0)),
        ],
        out_shape=[jax.ShapeDtypeStruct((rows, d), BF16), kv_shape, kv_shape],
        compiler_params=pltpu.CompilerParams(
            dimension_semantics=("parallel",),
            vmem_limit_bytes=_vmem_limit(pipelined, resident)),
        name="qkv",
    )(h, gpre, w_qkv, b_qkv)


def _attn_body(q_ref, kp_ref, kc_ref, vp_ref, vc_ref, km_ref, vm_ref, tbl_ref, o_ref):
    qw = GROUP * HEAD_DIM
    zpad = jnp.zeros((SLOTS - BAND_SLOTS - N_META, HEAD_DIM), BF16)
    for hh in range(kp_ref.shape[0]):
        outs = []
        for sb in range(N_SUB):
            lo = sb * SUB
            band = lambda p_ref, c_ref: jnp.concatenate(
                [p_ref[hh, lo:BLOCK], c_ref[hh, 0:lo + SUB]], axis=0)
            kf = jnp.concatenate([band(kp_ref, kc_ref), km_ref[hh], zpad], axis=0)
            vf = jnp.concatenate([band(vp_ref, vc_ref), vm_ref[hh], zpad], axis=0)
            q = q_ref[lo:lo + SUB, hh * qw:(hh + 1) * qw]
            qs = jnp.concatenate([q[:, g * HEAD_DIM:(g + 1) * HEAD_DIM] for g in range(GROUP)], axis=0)
            s = lax.dot_general(qs, kf, (((1,), (1,)), ((), ())), preferred_element_type=F32)
            s = s + tbl_ref[0, hh, sb]
            m = jnp.max(s, axis=-1, keepdims=True)
            p = jnp.exp(s - m)
            l = jnp.sum(p, axis=-1, keepdims=True)
            outs.append(_mm(p.astype(BF16), vf) / l)
        o_ref[:, hh * qw:(hh + 1) * qw] = jnp.concatenate(
            [jnp.concatenate([o[g * SUB:(g + 1) * SUB] for o in outs], axis=0) for g in range(GROUP)],
            axis=1).astype(o_ref.dtype)


def _attention(q, k, v, k_meta, v_meta, tbl, *, batch, seq, hps):
    rows, d = q.shape
    n_kv = k.shape[0]
    n_blk = seq // BLOCK
    qw = hps * GROUP * HEAD_DIM
    row_blk = lambda h, b, n: b * n_blk + n
    prev_blk = lambda h, b, n: b * n_blk + jnp.maximum(n - 1, 0)
    kv_spec = lambda fn: pl.BlockSpec((hps, BLOCK, HEAD_DIM), lambda h, b, n: (h, fn(h, b, n), 0))
    meta_spec = pl.BlockSpec((hps, N_META, HEAD_DIM), lambda h, b, n: (h, 0, 0))
    pipelined = (2 * _nbytes((BLOCK, qw), BF16) + 4 * _nbytes((hps, BLOCK, V7X_LANES), BF16)
                 + _nbytes((hps, GROUP * BLOCK, SLOTS), F32))
    resident = 6 * hps * _nbytes((GROUP * BLOCK, SLOTS), F32)
    return pl.pallas_call(
        _attn_body,
        grid=(n_kv // hps, batch, n_blk),
        in_specs=[
            pl.BlockSpec((BLOCK, qw), lambda h, b, n: (row_blk(h, b, n), h)),
            kv_spec(prev_blk), kv_spec(row_blk), kv_spec(prev_blk), kv_spec(row_blk),
            meta_spec, meta_spec,
            pl.BlockSpec((1, hps, N_SUB, GROUP * SUB, SLOTS),
                         lambda h, b, n: (jnp.minimum(n, 1), h, 0, 0, 0)),
        ],
        out_specs=pl.BlockSpec((BLOCK, qw), lambda h, b, n: (row_blk(h, b, n), h)),
        out_shape=jax.ShapeDtypeStruct((rows, d), BF16),
        compiler_params=pltpu.CompilerParams(
            dimension_semantics=("parallel", "parallel", "parallel"),
            vmem_limit_bytes=_vmem_limit(pipelined, resident)),
        name="swa_attention",
    )(q, k, k, v, v, k_meta, v_meta, tbl)


def _oproj_body(a_ref, w_ref, b_ref, gpost_ref, h_ref, o_ref):
    z = _mm(a_ref[...], w_ref[...]) + b_ref[...]
    o_ref[...] = h_ref[...] + _rms(z, gpost_ref[...])


def _oproj(a, w_o, b_o, gpost, h, *, tm):
    rows, d = h.shape
    vec = lambda: pl.BlockSpec((1, d), lambda i: (0, 0))
    pipelined = _nbytes((tm, d), BF16) + 2 * _nbytes((tm, d), F32)
    resident = _nbytes((d, d), BF16) + 3 * _nbytes((tm, d), F32)
    return pl.pallas_call(
        _oproj_body,
        grid=(rows // tm,),
        in_specs=[
            pl.BlockSpec((tm, d), lambda i: (i, 0)),
            _resident((d, d)),
            vec(), vec(),
            pl.BlockSpec((tm, d), lambda i: (i, 0)),
        ],
        out_specs=pl.BlockSpec((tm, d), lambda i: (i, 0)),
        out_shape=jax.ShapeDtypeStruct((rows, d), F32),
        compiler_params=pltpu.CompilerParams(
            dimension_semantics=("parallel",),
            vmem_limit_bytes=_vmem_limit(pipelined, resident)),
        name="attn_out_proj",
    )(a, w_o, b_o, gpost, h)


def _t5_bucket(dist):
    max_exact = N_BUCKETS // 2
    dd = jnp.maximum(dist, max_exact).astype(F32)
    large = max_exact + (jnp.log(dd / max_exact) / math.log(MAX_DISTANCE / max_exact)
                         * (N_BUCKETS - max_exact)).astype(jnp.int32)
    return jnp.where(dist < max_exact, dist, jnp.minimum(large, N_BUCKETS - 1))


def _slot_table(rel_bias, sinks, n_kv):
    n_heads = rel_bias.shape[1]
    rb = rel_bias.astype(F32)

    def lookup(bucket):
        onehot = (bucket[..., None] == jnp.arange(N_BUCKETS)).astype(F32)
        return jnp.einsum("...b,bh->h...", onehot, rb, precision=lax.Precision.HIGHEST)

    qpos = jnp.arange(BLOCK)[:, None]
    kpos = jnp.arange(2 * BLOCK)[None, :]
    mpos = jnp.arange(N_META)[None, :]
    d_band = BLOCK + qpos - kpos
    in_window = (d_band >= 0) & (d_band < WINDOW)
    bias_band = lookup(_t5_bucket(jnp.maximum(d_band, 0)))
    sink = jnp.broadcast_to(sinks.astype(F32)[:, None, None], (n_heads, SUB, 1))
    pad = jnp.full((n_heads, SUB, SLOTS - BAND_SLOTS - N_META - 1), NEG_INF, F32)
    tables = []
    for n in (0, 1):
        valid = in_window & (n * BLOCK + kpos >= BLOCK)
        band = jnp.where(valid[None], bias_band, NEG_INF)
        meta = lookup(_t5_bucket(N_META + n * BLOCK + qpos - mpos))
        subs = []
        for sb in range(N_SUB):
            lo = sb * SUB
            subs.append(jnp.concatenate([band[:, lo:lo + SUB, lo:lo + BAND_SLOTS],
                                         meta[:, lo:lo + SUB], sink, pad], axis=2))
        t = jnp.stack(subs, axis=1).reshape(n_kv, GROUP, N_SUB, SUB, SLOTS)
        tables.append(jnp.transpose(t, (0, 2, 1, 3, 4)).reshape(n_kv, N_SUB, GROUP * SUB, SLOTS))
    return jnp.stack(tables)


def kernel(x, meta_tokens, rel_bias, conv_w_in, conv_b_in, conv_w_dw, conv_b_dw, conv_ln_g, conv_ln_b,
           conv_w_out, conv_b_out, attn_w_qkv, attn_b_qkv, attn_sinks, attn_w_o, attn_b_o,
           norm_mix_pre, norm_mix_post, norm_ffn_pre, norm_ffn_post, ffn_w_gate, ffn_w_up, ffn_w_down):
    batch, seq, d = x.shape
    n_kv = (attn_w_qkv.shape[2] - d) // (2 * HEAD_DIM)
    rows = batch * seq
    row = lambda a: a.reshape(1, -1).astype(F32)
    bf = functools.partial(_to_bf16, kb=CAST_ROWS)

    tm_mm, tn_mm, tm_conv, tm_ffn, tf = 512, 512, 512, 1024, 512
    tm_meta = N_META

    h0 = x.reshape(rows, d)
    hm0 = meta_tokens.astype(x.dtype)
    w_gate, w_up, w_down = bf(ffn_w_gate), bf(ffn_w_up), bf(ffn_w_down)
    ffn = lambda h, layer, tm: _ffn(h, row(norm_ffn_pre[layer]), w_gate, w_up, w_down,
                                    row(norm_ffn_post[layer]), layer=layer, tm=tm, tf=tf)

    w_in, b_in = bf(conv_w_in)[0], row(conv_b_in[0])
    conv_args = (conv_w_dw[0].astype(F32), row(conv_b_dw[0]), row(conv_ln_g[0]), row(conv_ln_b[0]),
                 bf(conv_w_out)[0], row(conv_b_out[0]), row(norm_mix_post[0]))
    gpre0 = row(norm_mix_pre[0])

    vm = _conv_in(hm0, gpre0, w_in, b_in, tm=tm_meta, tn=tn_mm)
    zero_halo = jnp.zeros((HALO_ROWS, d), BF16)
    hm1 = _conv_out(vm, zero_halo, zero_halo, *conv_args, hm0, tm=tm_meta, rows_per_seq=N_META)
    hm2 = ffn(hm1, 0, tm_meta)

    v = _conv_in(h0, gpre0, w_in, b_in, tm=tm_mm, tn=tn_mm)
    halo0 = jnp.concatenate([jnp.zeros((HALO_ROWS - N_META, d), BF16), vm], axis=0)
    h1 = _conv_out(v, v, halo0, *conv_args, h0, tm=tm_conv, rows_per_seq=seq)
    h2 = ffn(h1, 0, tm_ffn)

    w_qkv, b_qkv = bf(attn_w_qkv)[0], row(attn_b_qkv[0])
    gpre1 = row(norm_mix_pre[1])
    _, k_meta, v_meta = _qkv(hm2, gpre1, w_qkv, b_qkv, tm=tm_meta, tn=tn_mm, n_kv=n_kv)
    q, k, vv = _qkv(h2, gpre1, w_qkv, b_qkv, tm=tm_mm, tn=tn_mm, n_kv=n_kv)
    tbl = _slot_table(rel_bias, attn_sinks[0], n_kv)
    a = _attention(q, k, vv, k_meta, v_meta, tbl, batch=batch, seq=seq, hps=ATTN_KV_HEADS_PER_STEP)
    h3 = _oproj(a, bf(attn_w_o)[0], row(attn_b_o[0]), row(norm_mix_post[1]), h2, tm=tm_mm)
    h4 = ffn(h3, 1, tm_ffn)
    return h4.reshape(batch, seq, d)
```

```python
import functools
import math

import jax
import jax.numpy as jnp
from jax import lax
from jax.experimental import pallas as pl
from jax.experimental.pallas import tpu as pltpu

N_META = 16
CONV_WIDTH = 31
HEAD_DIM = 64
GROUP = 8
WINDOW = 128
BLOCK = 128
N_BUCKETS = 32
MAX_DISTANCE = 128
RMS_EPS = 1e-6
LN_EPS = 1e-5
NEG_INF = -1e30

V7X_LANES = 128
V7X_SUBLANES = 8
V7X_BF16_SUBLANE_ROWS = 16
ATTN_KV_HEADS_PER_STEP = 2
CAST_ROWS = 256
FFN_RESULT_COLS = 256
NORM_ROWS = 128
CONV_ROW_CHUNK = 256
V7X_VMEM_BYTES = 64 * 1024 * 1024
V7X_VMEM_REQUEST_CAP = V7X_VMEM_BYTES - 6 * 1024 * 1024

HALO_ROWS = 2 * V7X_BF16_SUBLANE_ROWS
SUB = BLOCK // 2
N_SUB = BLOCK // SUB
BAND_SLOTS = BLOCK + SUB
SLOTS = 2 * V7X_LANES
BF16 = jnp.bfloat16
F32 = jnp.float32


def _vmem_limit(pipelined_bytes, resident_bytes):
    need = 2 * pipelined_bytes + resident_bytes
    return int(min(V7X_VMEM_REQUEST_CAP, need + need // 4 + (2 << 20)))


def _nbytes(shape, dtype):
    return math.prod(shape) * jnp.dtype(dtype).itemsize


def _rms(x, g):
    return x * lax.rsqrt(jnp.mean(x * x, axis=-1, keepdims=True) + RMS_EPS) * g


def _sigmoid(x):
    return 1.0 / (1.0 + jnp.exp(-x))


def _mm(a, b):
    return jnp.dot(a, b, preferred_element_type=F32)


def _row_groups(rows):
    step = min(rows, NORM_ROWS)
    return [slice(r, r + step) for r in range(0, rows, step)]


def _pre_norm(h_ref, g_ref, xn_ref):
    for rs in _row_groups(h_ref.shape[0]):
        xn_ref[rs, :] = _rms(h_ref[rs, :], g_ref[...]).astype(BF16)


def _post_norm_residual(o_ref, h_ref, g_ref):
    for rs in _row_groups(o_ref.shape[0]):
        o_ref[rs, :] = h_ref[rs, :] + _rms(o_ref[rs, :], g_ref[...])


def _cast_body(x_ref, o_ref):
    o_ref[...] = x_ref[...].astype(o_ref.dtype)


def _to_bf16(w, *, kb):
    n_l, k, n = w.shape
    blk = pl.BlockSpec((None, kb, n), lambda l, i: (l, i, 0))
    return pl.pallas_call(
        _cast_body,
        grid=(n_l, k // kb),
        in_specs=[blk],
        out_specs=blk,
        out_shape=jax.ShapeDtypeStruct(w.shape, BF16),
        compiler_params=pltpu.CompilerParams(
            dimension_semantics=("parallel", "parallel"),
            vmem_limit_bytes=_vmem_limit(_nbytes((kb, n), F32) + _nbytes((kb, n), BF16), 0)),
        name="cast_bf16",
    )(w)


def _resident(shape):
    return pl.BlockSpec(shape, lambda i: (0,) * len(shape), pipeline_mode=pl.Buffered(1))


def _conv_in_body(h_ref, gpre_ref, w_ref, b_ref, o_ref, xn_ref, *, tn):
    d = h_ref.shape[1]
    _pre_norm(h_ref, gpre_ref, xn_ref)
    xn = xn_ref[...]
    for j in range(d // tn):
        ca, cg = slice(j * tn, (j + 1) * tn), slice(d + j * tn, d + (j + 1) * tn)
        a = _mm(xn, w_ref[:, ca]) + b_ref[:, ca]
        g = _mm(xn, w_ref[:, cg]) + b_ref[:, cg]
        o_ref[:, ca] = (a * _sigmoid(g)).astype(o_ref.dtype)


def _conv_in(h, gpre, w_in, b_in, *, tm, tn):
    rows, d = h.shape
    pipelined = _nbytes((tm, d), F32) + _nbytes((tm, d), BF16)
    resident = _nbytes(w_in.shape, BF16) + _nbytes((tm, d), BF16) + 6 * _nbytes((tm, tn), F32)
    return pl.pallas_call(
        functools.partial(_conv_in_body, tn=tn),
        grid=(rows // tm,),
        in_specs=[
            pl.BlockSpec((tm, d), lambda i: (i, 0)),
            _resident((1, d)), _resident(w_in.shape), _resident(b_in.shape),
        ],
        out_specs=pl.BlockSpec((tm, d), lambda i: (i, 0)),
        out_shape=jax.ShapeDtypeStruct((rows, d), BF16),
        scratch_shapes=[pltpu.VMEM((tm, d), BF16)],
        compiler_params=pltpu.CompilerParams(
            dimension_semantics=("parallel",),
            vmem_limit_bytes=_vmem_limit(pipelined, resident)),
        name="conv_in",
    )(h, gpre, w_in, b_in)


def _conv_out_body(v_ref, vprev_ref, halo0_ref, wdw_ref, bdw_ref, lng_ref, lnb_ref, wout_ref,
                   bout_ref, gpost_ref, h_ref, o_ref, xbuf_ref, u_ref, y_ref, *, tm, rt, blocks_per_seq):
    n_slab, _, cw = xbuf_ref.shape
    first = pl.program_id(0) % blocks_per_seq == 0

    def fill_halo(src_ref):
        for c in range(n_slab):
            xbuf_ref[c, pl.ds(0, HALO_ROWS, stride=2), :] = src_ref[:, c * cw:(c + 1) * cw].astype(F32)

    pl.when(first)(functools.partial(fill_halo, halo0_ref))
    pl.when(jnp.logical_not(first))(functools.partial(fill_halo, vprev_ref))
    for c in range(n_slab):
        xbuf_ref[c, pl.ds(2 * HALO_ROWS, tm, stride=2), :] = v_ref[:, c * cw:(c + 1) * cw].astype(F32)

    def chunk(c, carry):
        ls = pl.ds(pl.multiple_of(c * cw, cw), cw)
        for r0 in range(0, tm, rt):
            acc = jnp.broadcast_to(bdw_ref[:, ls], (rt, cw)).reshape(rt // V7X_SUBLANES, V7X_SUBLANES, cw)
            for m in range(CONV_WIDTH):
                xs = xbuf_ref[c, pl.ds(2 * (HALO_ROWS + r0 - m), rt, stride=2), :]
                acc = acc + xs.reshape(acc.shape) * wdw_ref[c, CONV_WIDTH - 1 - m][None]
            u_ref[r0:r0 + rt, ls] = acc.reshape(rt, cw)
        return carry

    lax.fori_loop(0, n_slab, chunk, 0)

    for rs in _row_groups(tm):
        u = u_ref[rs, :]
        mu = jnp.mean(u, axis=-1, keepdims=True)
        uc = u - mu
        var = jnp.mean(uc * uc, axis=-1, keepdims=True)
        y = uc * lax.rsqrt(var + LN_EPS) * lng_ref[...] + lnb_ref[...]
        y_ref[rs, :] = (y * _sigmoid(y)).astype(BF16)
    o_ref[...] = _mm(y_ref[...], wout_ref[...]) + bout_ref[...]
    _post_norm_residual(o_ref, h_ref, gpost_ref)


def _conv_out(v, vprev_src, halo0, w_dw, b_dw, ln_g, ln_b, w_out, b_out, gpost, h, *, tm,
              rows_per_seq):
    rows, d = h.shape
    cw = V7X_LANES
    n_slab = d // cw
    body = functools.partial(_conv_out_body, tm=tm, rt=min(tm, CONV_ROW_CHUNK),
                             blocks_per_seq=rows_per_seq // tm)
    halo_blocks_per_tile = tm // HALO_ROWS
    w_slabs = jnp.broadcast_to(jnp.transpose(w_dw.reshape(CONV_WIDTH, n_slab, cw), (1, 0, 2))[:, :, None, :],
                               (n_slab, CONV_WIDTH, V7X_SUBLANES, cw))
    vec = lambda: pl.BlockSpec((1, d), lambda i: (0, 0))
    pipelined = (_nbytes((tm, d), BF16) + 2 * _nbytes((HALO_ROWS, d), BF16)
                 + _nbytes((HALO_ROWS, d), F32) + 2 * _nbytes((tm, d), F32))
    resident = (_nbytes((d, d), BF16) + 2 * _nbytes((tm + HALO_ROWS, d), F32)
                + 4 * _nbytes((tm, d), F32))
    return pl.pallas_call(
        body,
        grid=(rows // tm,),
        in_specs=[
            pl.BlockSpec((tm, d), lambda i: (i, 0)),
            pl.BlockSpec((HALO_ROWS, d), lambda i: (jnp.maximum(i * halo_blocks_per_tile - 1, 0), 0)),
            pl.BlockSpec((HALO_ROWS, d), lambda i: (0, 0)),
            _resident((n_slab, CONV_WIDTH, V7X_SUBLANES, cw)),
            vec(), vec(), vec(),
            _resident((d, d)),
            vec(), vec(),
            pl.BlockSpec((tm, d), lambda i: (i, 0)),
        ],
        out_specs=pl.BlockSpec((tm, d), lambda i: (i, 0)),
        out_shape=jax.ShapeDtypeStruct((rows, d), F32),
        scratch_shapes=[pltpu.VMEM((n_slab, 2 * (tm + HALO_ROWS), cw), F32), pltpu.VMEM((tm, d), F32),
                        pltpu.VMEM((tm, d), BF16)],
        compiler_params=pltpu.CompilerParams(
            dimension_semantics=("parallel",),
            vmem_limit_bytes=_vmem_limit(pipelined, resident)),
        name="conv_out",
    )(v, vprev_src, halo0, w_slabs, b_dw, ln_g, ln_b, w_out, b_out, gpost, h)


def _ffn_body(h_ref, gpre_ref, wg_ref, wu_ref, wd_ref, gpost_ref, o_ref, xn_ref, *, tc):
    f = pl.program_id(1)
    tf, d = wd_ref.shape

    @pl.when(f == 0)
    def _():
        _pre_norm(h_ref, gpre_ref, xn_ref)
        o_ref[...] = jnp.zeros_like(o_ref)

    xn = xn_ref[...]
    acts = []
    for c in range(0, tf, tc):
        g = _mm(xn, wg_ref[:, c:c + tc])
        u = _mm(xn, wu_ref[:, c:c + tc])
        acts.append((g * _sigmoid(g) * u).astype(BF16))
    a = jnp.concatenate(acts, axis=1)
    for c in range(0, d, tc):
        o_ref[:, c:c + tc] += _mm(a, wd_ref[:, c:c + tc])

    @pl.when(f == pl.num_programs(1) - 1)
    def _():
        _post_norm_residual(o_ref, h_ref, gpost_ref)


def _ffn(h, gpre, w_gate, w_up, w_down, gpost, *, layer, tm, tf):
    rows, d = h.shape
    ff = w_gate.shape[2]
    tc = min(tf, FFN_RESULT_COLS)
    pipelined = 2 * _nbytes((tm, d), F32) + 3 * _nbytes((d, tf), BF16)
    resident = _nbytes((tm, d), BF16) + _nbytes((tm, tf), BF16) + 6 * _nbytes((tm, tc), F32)
    return pl.pallas_call(
        functools.partial(_ffn_body, tc=tc),
        grid=(rows // tm, ff // tf),
        in_specs=[
            pl.BlockSpec((tm, d), lambda i, f: (i, 0)),
            pl.BlockSpec((1, d), lambda i, f: (0, 0)),
            pl.BlockSpec((None, d, tf), lambda i, f: (layer, 0, f)),
            pl.BlockSpec((None, d, tf), lambda i, f: (layer, 0, f)),
            pl.BlockSpec((None, tf, d), lambda i, f: (layer, f, 0)),
            pl.BlockSpec((1, d), lambda i, f: (0, 0)),
        ],
        out_specs=pl.BlockSpec((tm, d), lambda i, f: (i, 0)),
        out_shape=jax.ShapeDtypeStruct((rows, d), F32),
        scratch_shapes=[pltpu.VMEM((tm, d), BF16)],
        compiler_params=pltpu.CompilerParams(
            dimension_semantics=("parallel", "arbitrary"),
            vmem_limit_bytes=_vmem_limit(pipelined, resident)),
        name="ffn",
    )(h, gpre, w_gate, w_up, w_down, gpost)


def _qkv_body(h_ref, gpre_ref, w_ref, b_ref, q_ref, k_ref, v_ref, xn_ref, *, tn, n_kv):
    d = h_ref.shape[1]
    _pre_norm(h_ref, gpre_ref, xn_ref)
    xn = xn_ref[...]
    for j in range(d // tn):
        cq = slice(j * tn, (j + 1) * tn)
        q_ref[:, cq] = ((_mm(xn, w_ref[:, cq]) + b_ref[:, cq]) * (HEAD_DIM ** -0.5)).astype(BF16)
    y = _mm(xn, w_ref[:, d:]) + b_ref[:, d:]
    for hh in range(n_kv):
        k_ref[hh] = y[:, hh * HEAD_DIM:(hh + 1) * HEAD_DIM].astype(BF16)
        v_ref[hh] = y[:, (n_kv + hh) * HEAD_DIM:(n_kv + hh + 1) * HEAD_DIM].astype(BF16)


def _qkv(h, gpre, w_qkv, b_qkv, *, tm, tn, n_kv):
    rows, d = h.shape
    kv_shape = jax.ShapeDtypeStruct((n_kv, rows, HEAD_DIM), BF16)
    pipelined = (_nbytes((tm, d), F32) + _nbytes((tm, d), BF16)
                 + 2 * _nbytes((n_kv, tm, V7X_LANES), BF16))
    resident = _nbytes(w_qkv.shape, BF16) + _nbytes((tm, d), BF16) + 4 * _nbytes((tm, tn), F32)
    return pl.pallas_call(
        functools.partial(_qkv_body, tn=tn, n_kv=n_kv),
        grid=(rows // tm,),
        in_specs=[
            pl.BlockSpec((tm, d), lambda i: (i, 0)),
            _resident((1, d)), _resident(w_qkv.shape), _resident(b_qkv.shape),
        ],
        out_specs=[
            pl.BlockSpec((tm, d), lambda i: (i, 0)),
            pl.BlockSpec((n_kv, tm, HEAD_DIM), lambda i: (0, i, 0)),
            pl.BlockSpec((n_kv, tm, HEAD_DIM), lambda i: (0, i, 0)),
        ],
        out_shape=[jax.ShapeDtypeStruct((rows, d), BF16), kv_shape, kv_shape],
        scratch_shapes=[pltpu.VMEM((tm, d), BF16)],
        compiler_params=pltpu.CompilerParams(
            dimension_semantics=("parallel",),
            vmem_limit_bytes=_vmem_limit(pipelined, resident)),
        name="qkv",
    )(h, gpre, w_qkv, b_qkv)


def _attn_body(q_ref, kp_ref, kc_ref, vp_ref, vc_ref, km_ref, vm_ref, tbl_ref, o_ref):
    qw = GROUP * HEAD_DIM
    zpad = jnp.zeros((SLOTS - BAND_SLOTS - N_META, HEAD_DIM), BF16)
    for hh in range(kp_ref.shape[0]):
        outs = []
        for sb in range(N_SUB):
            lo = sb * SUB
            band = lambda p_ref, c_ref: jnp.concatenate(
                [p_ref[hh, lo:BLOCK], c_ref[hh, 0:lo + SUB]], axis=0)
            kf = jnp.concatenate([band(kp_ref, kc_ref), km_ref[hh], zpad], axis=0)
            vf = jnp.concatenate([band(vp_ref, vc_ref), vm_ref[hh], zpad], axis=0)
            q = q_ref[lo:lo + SUB, hh * qw:(hh + 1) * qw]
            qs = jnp.concatenate([q[:, g * HEAD_DIM:(g + 1) * HEAD_DIM] for g in range(GROUP)], axis=0)
            s = lax.dot_general(qs, kf, (((1,), (1,)), ((), ())), preferred_element_type=F32)
            s = s + tbl_ref[0, hh, sb]
            m = jnp.max(s, axis=-1, keepdims=True)
            p = jnp.exp(s - m)
            l = jnp.sum(p, axis=-1, keepdims=True)
            outs.append(_mm(p.astype(BF16), vf) / l)
        o_ref[:, hh * qw:(hh + 1) * qw] = jnp.concatenate(
            [jnp.concatenate([o[g * SUB:(g + 1) * SUB] for o in outs], axis=0) for g in range(GROUP)],
            axis=1).astype(o_ref.dtype)


def _attention(q, k, v, k_meta, v_meta, tbl, *, batch, seq, hps):
    rows, d = q.shape
    n_kv = k.shape[0]
    n_blk = seq // BLOCK
    qw = hps * GROUP * HEAD_DIM
    row_blk = lambda h, b, n: b * n_blk + n
    prev_blk = lambda h, b, n: b * n_blk + jnp.maximum(n - 1, 0)
    kv_spec = lambda fn: pl.BlockSpec((hps, BLOCK, HEAD_DIM), lambda h, b, n: (h, fn(h, b, n), 0))
    meta_spec = pl.BlockSpec((hps, N_META, HEAD_DIM), lambda h, b, n: (h, 0, 0))
    pipelined = (2 * _nbytes((BLOCK, qw), BF16) + 4 * _nbytes((hps, BLOCK, V7X_LANES), BF16)
                 + _nbytes((hps, GROUP * BLOCK, SLOTS), F32))
    resident = 6 * hps * _nbytes((GROUP * BLOCK, SLOTS), F32)
    return pl.pallas_call(
        _attn_body,
        grid=(n_kv // hps, batch, n_blk),
        in_specs=[
            pl.BlockSpec((BLOCK, qw), lambda h, b, n: (row_blk(h, b, n), h)),
            kv_spec(prev_blk), kv_spec(row_blk), kv_spec(prev_blk), kv_spec(row_blk),
            meta_spec, meta_spec,
            pl.BlockSpec((1, hps, N_SUB, GROUP * SUB, SLOTS),
                         lambda h, b, n: (jnp.minimum(n, 1), h, 0, 0, 0)),
        ],
        out_specs=pl.BlockSpec((BLOCK, qw), lambda h, b, n: (row_blk(h, b, n), h)),
        out_shape=jax.ShapeDtypeStruct((rows, d), BF16),
        compiler_params=pltpu.CompilerParams(
            dimension_semantics=("parallel", "parallel", "parallel"),
            vmem_limit_bytes=_vmem_limit(pipelined, resident)),
        name="swa_attention",
    )(q, k, k, v, v, k_meta, v_meta, tbl)


def _oproj_body(a_ref, w_ref, b_ref, gpost_ref, h_ref, o_ref):
    o_ref[...] = _mm(a_ref[...], w_ref[...]) + b_ref[...]
    _post_norm_residual(o_ref, h_ref, gpost_ref)


def _oproj(a, w_o, b_o, gpost, h, *, tm):
    rows, d = h.shape
    vec = lambda: pl.BlockSpec((1, d), lambda i: (0, 0))
    pipelined = _nbytes((tm, d), BF16) + 2 * _nbytes((tm, d), F32)
    resident = _nbytes((d, d), BF16) + 3 * _nbytes((tm, d), F32)
    return pl.pallas_call(
        _oproj_body,
        grid=(rows // tm,),
        in_specs=[
            pl.BlockSpec((tm, d), lambda i: (i, 0)),
            _resident((d, d)),
            vec(), vec(),
            pl.BlockSpec((tm, d), lambda i: (i, 0)),
        ],
        out_specs=pl.BlockSpec((tm, d), lambda i: (i, 0)),
        out_shape=jax.ShapeDtypeStruct((rows, d), F32),
        compiler_params=pltpu.CompilerParams(
            dimension_semantics=("parallel",),
            vmem_limit_bytes=_vmem_limit(pipelined, resident)),
        name="attn_out_proj",
    )(a, w_o, b_o, gpost, h)


def _t5_bucket(dist):
    max_exact = N_BUCKETS // 2
    dd = jnp.maximum(dist, max_exact).astype(F32)
    large = max_exact + (jnp.log(dd / max_exact) / math.log(MAX_DISTANCE / max_exact)
                         * (N_BUCKETS - max_exact)).astype(jnp.int32)
    return jnp.where(dist < max_exact, dist, jnp.minimum(large, N_BUCKETS - 1))


def _slot_table(rel_bias, sinks, n_kv):
    n_heads = rel_bias.shape[1]
    rb = rel_bias.astype(F32)

    def lookup(bucket):
        onehot = (bucket[..., None] == jnp.arange(N_BUCKETS)).astype(F32)
        return jnp.einsum("...b,bh->h...", onehot, rb, precision=lax.Precision.HIGHEST)

    qpos = jnp.arange(BLOCK)[:, None]
    kpos = jnp.arange(2 * BLOCK)[None, :]
    mpos = jnp.arange(N_META)[None, :]
    d_band = BLOCK + qpos - kpos
    in_window = (d_band >= 0) & (d_band < WINDOW)
    bias_band = lookup(_t5_bucket(jnp.maximum(d_band, 0)))
    sink = jnp.broadcast_to(sinks.astype(F32)[:, None, None], (n_heads, SUB, 1))
    pad = jnp.full((n_heads, SUB, SLOTS - BAND_SLOTS - N_META - 1), NEG_INF, F32)
    tables = []
    for n in (0, 1):
        valid = in_window & (n * BLOCK + kpos >= BLOCK)
        band = jnp.where(valid[None], bias_band, NEG_INF)
        meta = lookup(_t5_bucket(N_META + n * BLOCK + qpos - mpos))
        subs = []
        for sb in range(N_SUB):
            lo = sb * SUB
            subs.append(jnp.concatenate([band[:, lo:lo + SUB, lo:lo + BAND_SLOTS],
                                         meta[:, lo:lo + SUB], sink, pad], axis=2))
        t = jnp.stack(subs, axis=1).reshape(n_kv, GROUP, N_SUB, SUB, SLOTS)
        tables.append(jnp.transpose(t, (0, 2, 1, 3, 4)).reshape(n_kv, N_SUB, GROUP * SUB, SLOTS))
    return jnp.stack(tables)


def kernel(x, meta_tokens, rel_bias, conv_w_in, conv_b_in, conv_w_dw, conv_b_dw, conv_ln_g, conv_ln_b,
           conv_w_out, conv_b_out, attn_w_qkv, attn_b_qkv, attn_sinks, attn_w_o, attn_b_o,
           norm_mix_pre, norm_mix_post, norm_ffn_pre, norm_ffn_post, ffn_w_gate, ffn_w_up, ffn_w_down):
    batch, seq, d = x.shape
    n_kv = (attn_w_qkv.shape[2] - d) // (2 * HEAD_DIM)
    rows = batch * seq
    row = lambda a: a.reshape(1, -1).astype(F32)
    bf = functools.partial(_to_bf16, kb=CAST_ROWS)

    tm_mm, tn_mm, tm_conv, tm_ffn, tf = 1024, 512, 512, 1024, 512
    tm_meta = N_META

    h0 = x.reshape(rows, d)
    hm0 = meta_tokens.astype(x.dtype)
    w_gate, w_up, w_down = bf(ffn_w_gate), bf(ffn_w_up), bf(ffn_w_down)
    ffn = lambda h, layer, tm: _ffn(h, row(norm_ffn_pre[layer]), w_gate, w_up, w_down,
                                    row(norm_ffn_post[layer]), layer=layer, tm=tm, tf=tf)

    w_in, b_in = bf(conv_w_in)[0], row(conv_b_in[0])
    conv_args = (conv_w_dw[0].astype(F32), row(conv_b_dw[0]), row(conv_ln_g[0]), row(conv_ln_b[0]),
                 bf(conv_w_out)[0], row(conv_b_out[0]), row(norm_mix_post[0]))
    gpre0 = row(norm_mix_pre[0])

    vm = _conv_in(hm0, gpre0, w_in, b_in, tm=tm_meta, tn=tn_mm)
    zero_halo = jnp.zeros((HALO_ROWS, d), BF16)
    hm1 = _conv_out(vm, zero_halo, zero_halo, *conv_args, hm0, tm=tm_meta, rows_per_seq=N_META)
    hm2 = ffn(hm1, 0, tm_meta)

    v = _conv_in(h0, gpre0, w_in, b_in, tm=tm_mm, tn=tn_mm)
    halo0 = jnp.concatenate([jnp.zeros((HALO_ROWS - N_META, d), BF16), vm], axis=0)
    h1 = _conv_out(v, v, halo0, *conv_args, h0, tm=tm_conv, rows_per_seq=seq)
    h2 = ffn(h1, 0, tm_ffn)

    w_qkv, b_qkv = bf(attn_w_qkv)[0], row(attn_b_qkv[0])
    gpre1 = row(norm_mix_pre[1])
    _, k_meta, v_meta = _qkv(hm2, gpre1, w_qkv, b_qkv, tm=tm_meta, tn=tn_mm, n_kv=n_kv)
    q, k, vv = _qkv(h2, gpre1, w_qkv, b_qkv, tm=tm_mm, tn=tn_mm, n_kv=n_kv)
    tbl = _slot_table(rel_bias, attn_sinks[0], n_kv)
    a = _attention(q, k, vv, k_meta, v_meta, tbl, batch=batch, seq=seq, hps=ATTN_KV_HEADS_PER_STEP)
    h3 = _oproj(a, bf(attn_w_o)[0], row(attn_b_o[0]), row(norm_mix_post[1]), h2, tm=tm_mm)
    h4 = ffn(h3, 1, tm_ffn)
    return h4.reshape(batch, seq, d)
```

```python
import functools
import math

import jax
import jax.numpy as jnp
from jax import lax
from jax.experimental import pallas as pl
from jax.experimental.pallas import tpu as pltpu

N_META = 16
CONV_WIDTH = 31
HEAD_DIM = 64
GROUP = 8
WINDOW = 128
BLOCK = 128
N_BUCKETS = 32
MAX_DISTANCE = 128
RMS_EPS = 1e-6
LN_EPS = 1e-5
NEG_INF = -1e30

V7X_LANES = 128
V7X_SUBLANES = 8
V7X_BF16_SUBLANE_ROWS = 16
ATTN_KV_HEADS_PER_STEP = 2
CAST_ROWS = 256
FFN_RESULT_COLS = 256
NORM_ROWS = 128
CONV_ROW_CHUNK = 256
V7X_VMEM_BYTES = 64 * 1024 * 1024
V7X_VMEM_REQUEST_CAP = V7X_VMEM_BYTES - 6 * 1024 * 1024
VMEM_SPILL_FRACTION = 4
VMEM_PIPELINE_STATE_BYTES = 2 * 1024 * 1024

HALO_ROWS = 2 * V7X_BF16_SUBLANE_ROWS
SUB = BLOCK // 2
N_SUB = BLOCK // SUB
BAND_SLOTS = BLOCK + SUB
SLOTS = 2 * V7X_LANES
BF16 = jnp.bfloat16
F32 = jnp.float32


def _vmem_limit(pipelined_bytes, resident_bytes):
    need = 2 * pipelined_bytes + resident_bytes
    need += need // VMEM_SPILL_FRACTION + VMEM_PIPELINE_STATE_BYTES
    return int(min(V7X_VMEM_REQUEST_CAP, need))


def _nbytes(shape, dtype):
    return math.prod(shape) * jnp.dtype(dtype).itemsize


def _rms(x, g):
    return x * lax.rsqrt(jnp.mean(x * x, axis=-1, keepdims=True) + RMS_EPS) * g


def _sigmoid(x):
    return 1.0 / (1.0 + jnp.exp(-x))


def _mm(a, b):
    return jnp.dot(a, b, preferred_element_type=F32)


def _row_groups(rows):
    step = min(rows, NORM_ROWS)
    return [slice(r, r + step) for r in range(0, rows, step)]


def _pre_norm(h_ref, g_ref, xn_ref):
    for rs in _row_groups(h_ref.shape[0]):
        xn_ref[rs, :] = _rms(h_ref[rs, :], g_ref[...]).astype(BF16)


def _post_norm_residual(o_ref, h_ref, g_ref):
    for rs in _row_groups(o_ref.shape[0]):
        o_ref[rs, :] = h_ref[rs, :] + _rms(o_ref[rs, :], g_ref[...])


def _cast_body(x_ref, o_ref):
    o_ref[...] = x_ref[...].astype(o_ref.dtype)


def _to_bf16(w, *, kb):
    n_l, k, n = w.shape
    blk = pl.BlockSpec((None, kb, n), lambda l, i: (l, i, 0))
    return pl.pallas_call(
        _cast_body,
        grid=(n_l, k // kb),
        in_specs=[blk],
        out_specs=blk,
        out_shape=jax.ShapeDtypeStruct(w.shape, BF16),
        compiler_params=pltpu.CompilerParams(
            dimension_semantics=("parallel", "parallel"),
            vmem_limit_bytes=_vmem_limit(_nbytes((kb, n), F32) + _nbytes((kb, n), BF16), 0)),
        name="cast_bf16",
    )(w)


def _resident(shape):
    return pl.BlockSpec(shape, lambda i: (0,) * len(shape), pipeline_mode=pl.Buffered(1))


def _conv_in_body(h_ref, gpre_ref, w_ref, b_ref, o_ref, xn_ref, *, tn):
    d = h_ref.shape[1]
    _pre_norm(h_ref, gpre_ref, xn_ref)
    xn = xn_ref[...]
    for j in range(d // tn):
        ca, cg = slice(j * tn, (j + 1) * tn), slice(d + j * tn, d + (j + 1) * tn)
        a = _mm(xn, w_ref[:, ca]) + b_ref[:, ca]
        g = _mm(xn, w_ref[:, cg]) + b_ref[:, cg]
        o_ref[:, ca] = (a * _sigmoid(g)).astype(o_ref.dtype)


def _conv_in(h, gpre, w_in, b_in, *, tm, tn):
    rows, d = h.shape
    pipelined = _nbytes((tm, d), F32) + _nbytes((tm, d), BF16)
    resident = _nbytes(w_in.shape, BF16) + _nbytes((tm, d), BF16) + 6 * _nbytes((tm, tn), F32)
    return pl.pallas_call(
        functools.partial(_conv_in_body, tn=tn),
        grid=(rows // tm,),
        in_specs=[
            pl.BlockSpec((tm, d), lambda i: (i, 0)),
            _resident((1, d)), _resident(w_in.shape), _resident(b_in.shape),
        ],
        out_specs=pl.BlockSpec((tm, d), lambda i: (i, 0)),
        out_shape=jax.ShapeDtypeStruct((rows, d), BF16),
        scratch_shapes=[pltpu.VMEM((tm, d), BF16)],
        compiler_params=pltpu.CompilerParams(
            dimension_semantics=("parallel",),
            vmem_limit_bytes=_vmem_limit(pipelined, resident)),
        name="conv_in",
    )(h, gpre, w_in, b_in)


def _conv_out_body(v_ref, vprev_ref, halo0_ref, wdw_ref, bdw_ref, lng_ref, lnb_ref, wout_ref,
                   bout_ref, gpost_ref, h_ref, o_ref, xbuf_ref, u_ref, y_ref, *, tm, rt, blocks_per_seq):
    n_slab, _, cw = xbuf_ref.shape
    first = pl.program_id(0) % blocks_per_seq == 0

    def fill_halo(src_ref):
        for c in range(n_slab):
            xbuf_ref[c, pl.ds(0, HALO_ROWS, stride=2), :] = src_ref[:, c * cw:(c + 1) * cw].astype(F32)

    pl.when(first)(functools.partial(fill_halo, halo0_ref))
    pl.when(jnp.logical_not(first))(functools.partial(fill_halo, vprev_ref))
    for c in range(n_slab):
        xbuf_ref[c, pl.ds(2 * HALO_ROWS, tm, stride=2), :] = v_ref[:, c * cw:(c + 1) * cw].astype(F32)

    def chunk(c, carry):
        ls = pl.ds(pl.multiple_of(c * cw, cw), cw)
        for r0 in range(0, tm, rt):
            acc = jnp.broadcast_to(bdw_ref[:, ls], (rt, cw)).reshape(rt // V7X_SUBLANES, V7X_SUBLANES, cw)
            for m in range(CONV_WIDTH):
                xs = xbuf_ref[c, pl.ds(2 * (HALO_ROWS + r0 - m), rt, stride=2), :]
                acc = acc + xs.reshape(acc.shape) * wdw_ref[c, CONV_WIDTH - 1 - m][None]
            u_ref[r0:r0 + rt, ls] = acc.reshape(rt, cw)
        return carry

    lax.fori_loop(0, n_slab, chunk, 0)

    for rs in _row_groups(tm):
        u = u_ref[rs, :]
        mu = jnp.mean(u, axis=-1, keepdims=True)
        uc = u - mu
        var = jnp.mean(uc * uc, axis=-1, keepdims=True)
        y = uc * lax.rsqrt(var + LN_EPS) * lng_ref[...] + lnb_ref[...]
        y_ref[rs, :] = (y * _sigmoid(y)).astype(BF16)
    o_ref[...] = _mm(y_ref[...], wout_ref[...]) + bout_ref[...]
    _post_norm_residual(o_ref, h_ref, gpost_ref)


def _conv_out(v, vprev_src, halo0, w_dw, b_dw, ln_g, ln_b, w_out, b_out, gpost, h, *, tm,
              rows_per_seq):
    rows, d = h.shape
    cw = V7X_LANES
    n_slab = d // cw
    body = functools.partial(_conv_out_body, tm=tm, rt=min(tm, CONV_ROW_CHUNK),
                             blocks_per_seq=rows_per_seq // tm)
    halo_blocks_per_tile = tm // HALO_ROWS
    w_slabs = jnp.broadcast_to(jnp.transpose(w_dw.reshape(CONV_WIDTH, n_slab, cw), (1, 0, 2))[:, :, None, :],
                               (n_slab, CONV_WIDTH, V7X_SUBLANES, cw))
    vec = lambda: pl.BlockSpec((1, d), lambda i: (0, 0))
    pipelined = (_nbytes((tm, d), BF16) + 2 * _nbytes((HALO_ROWS, d), BF16)
                 + _nbytes((HALO_ROWS, d), F32) + 2 * _nbytes((tm, d), F32))
    resident = (_nbytes((d, d), BF16) + 2 * _nbytes((tm + HALO_ROWS, d), F32)
                + 4 * _nbytes((tm, d), F32))
    return pl.pallas_call(
        body,
        grid=(rows // tm,),
        in_specs=[
            pl.BlockSpec((tm, d), lambda i: (i, 0)),
            pl.BlockSpec((HALO_ROWS, d), lambda i: (jnp.maximum(i * halo_blocks_per_tile - 1, 0), 0)),
            pl.BlockSpec((HALO_ROWS, d), lambda i: (0, 0)),
            _resident((n_slab, CONV_WIDTH, V7X_SUBLANES, cw)),
            vec(), vec(), vec(),
            _resident((d, d)),
            vec(), vec(),
            pl.BlockSpec((tm, d), lambda i: (i, 0)),
        ],
        out_specs=pl.BlockSpec((tm, d), lambda i: (i, 0)),
        out_shape=jax.ShapeDtypeStruct((rows, d), F32),
        scratch_shapes=[pltpu.VMEM((n_slab, 2 * (tm + HALO_ROWS), cw), F32), pltpu.VMEM((tm, d), F32),
                        pltpu.VMEM((tm, d), BF16)],
        compiler_params=pltpu.CompilerParams(
            dimension_semantics=("parallel",),
            vmem_limit_bytes=_vmem_limit(pipelined, resident)),
        name="conv_out",
    )(v, vprev_src, halo0, w_slabs, b_dw, ln_g, ln_b, w_out, b_out, gpost, h)


def _ffn_body(h_ref, gpre_ref, wg_ref, wu_ref, wd_ref, gpost_ref, o_ref, xn_ref, *, tc):
    f = pl.program_id(1)
    tf, d = wd_ref.shape

    def hidden_tile(first):
        xn = xn_ref[...]
        acts = []
        for c in range(0, tf, tc):
            g = _mm(xn, wg_ref[:, c:c + tc])
            u = _mm(xn, wu_ref[:, c:c + tc])
            acts.append((g * _sigmoid(g) * u).astype(BF16))
        a = jnp.concatenate(acts, axis=1)
        for c in range(0, d, tc):
            part = _mm(a, wd_ref[:, c:c + tc])
            if first:
                o_ref[:, c:c + tc] = part
            else:
                o_ref[:, c:c + tc] += part

    @pl.when(f == 0)
    def _():
        _pre_norm(h_ref, gpre_ref, xn_ref)
        hidden_tile(first=True)

    @pl.when(f > 0)
    def _():
        hidden_tile(first=False)

    @pl.when(f == pl.num_programs(1) - 1)
    def _():
        _post_norm_residual(o_ref, h_ref, gpost_ref)


def _ffn(h, gpre, w_gate, w_up, w_down, gpost, *, layer, tm, tf):
    rows, d = h.shape
    ff = w_gate.shape[2]
    tc = min(tf, FFN_RESULT_COLS)
    pipelined = 2 * _nbytes((tm, d), F32) + 3 * _nbytes((d, tf), BF16)
    resident = _nbytes((tm, d), BF16) + _nbytes((tm, tf), BF16) + 6 * _nbytes((tm, tc), F32)
    return pl.pallas_call(
        functools.partial(_ffn_body, tc=tc),
        grid=(rows // tm, ff // tf),
        in_specs=[
            pl.BlockSpec((tm, d), lambda i, f: (i, 0)),
            pl.BlockSpec((1, d), lambda i, f: (0, 0)),
            pl.BlockSpec((None, d, tf), lambda i, f: (layer, 0, f)),
            pl.BlockSpec((None, d, tf), lambda i, f: (layer, 0, f)),
            pl.BlockSpec((None, tf, d), lambda i, f: (layer, f, 0)),
            pl.BlockSpec((1, d), lambda i, f: (0, 0)),
        ],
        out_specs=pl.BlockSpec((tm, d), lambda i, f: (i, 0)),
        out_shape=jax.ShapeDtypeStruct((rows, d), F32),
        scratch_shapes=[pltpu.VMEM((tm, d), BF16)],
        compiler_params=pltpu.CompilerParams(
            dimension_semantics=("parallel", "arbitrary"),
            vmem_limit_bytes=_vmem_limit(pipelined, resident)),
        name="ffn",
    )(h, gpre, w_gate, w_up, w_down, gpost)


def _qkv_body(h_ref, gpre_ref, w_ref, b_ref, q_ref, k_ref, v_ref, xn_ref, *, tn, n_kv):
    d = h_ref.shape[1]
    _pre_norm(h_ref, gpre_ref, xn_ref)
    xn = xn_ref[...]
    for j in range(d // tn):
        cq = slice(j * tn, (j + 1) * tn)
        q_ref[:, cq] = ((_mm(xn, w_ref[:, cq]) + b_ref[:, cq]) * (HEAD_DIM ** -0.5)).astype(BF16)
    y = _mm(xn, w_ref[:, d:]) + b_ref[:, d:]
    for hh in range(n_kv):
        k_ref[hh] = y[:, hh * HEAD_DIM:(hh + 1) * HEAD_DIM].astype(BF16)
        v_ref[hh] = y[:, (n_kv + hh) * HEAD_DIM:(n_kv + hh + 1) * HEAD_DIM].astype(BF16)


def _qkv(h, gpre, w_qkv, b_qkv, *, tm, tn, n_kv):
    rows, d = h.shape
    kv_shape = jax.ShapeDtypeStruct((n_kv, rows, HEAD_DIM), BF16)
    pipelined = (_nbytes((tm, d), F32) + _nbytes((tm, d), BF16)
                 + 2 * _nbytes((n_kv, tm, V7X_LANES), BF16))
    resident = _nbytes(w_qkv.shape, BF16) + _nbytes((tm, d), BF16) + 4 * _nbytes((tm, tn), F32)
    return pl.pallas_call(
        functools.partial(_qkv_body, tn=tn, n_kv=n_kv),
        grid=(rows // tm,),
        in_specs=[
            pl.BlockSpec((tm, d), lambda i: (i, 0)),
            _resident((1, d)), _resident(w_qkv.shape), _resident(b_qkv.shape),
        ],
        out_specs=[
            pl.BlockSpec((tm, d), lambda i: (i, 0)),
            pl.BlockSpec((n_kv, tm, HEAD_DIM), lambda i: (0, i, 0)),
            pl.BlockSpec((n_kv, tm, HEAD_DIM), lambda i: (0, i, 0)),
        ],
        out_shape=[jax.ShapeDtypeStruct((rows, d), BF16), kv_shape, kv_shape],
        scratch_shapes=[pltpu.VMEM((tm, d), BF16)],
        compiler_params=pltpu.CompilerParams(
            dimension_semantics=("parallel",),
            vmem_limit_bytes=_vmem_limit(pipelined, resident)),
        name="qkv",
    )(h, gpre, w_qkv, b_qkv)


def _attn_body(q_ref, kp_ref, kc_ref, vp_ref, vc_ref, km_ref, vm_ref, tbl_ref, o_ref):
    qw = GROUP * HEAD_DIM
    zpad = jnp.zeros((SLOTS - BAND_SLOTS - N_META, HEAD_DIM), BF16)
    for hh in range(kp_ref.shape[0]):
        outs = []
        for sb in range(N_SUB):
            lo = sb * SUB
            band = lambda p_ref, c_ref: jnp.concatenate(
                [p_ref[hh, lo:BLOCK], c_ref[hh, 0:lo + SUB]], axis=0)
            kf = jnp.concatenate([band(kp_ref, kc_ref), km_ref[hh], zpad], axis=0)
            vf = jnp.concatenate([band(vp_ref, vc_ref), vm_ref[hh], zpad], axis=0)
            q = q_ref[lo:lo + SUB, hh * qw:(hh + 1) * qw]
            qs = jnp.concatenate([q[:, g * HEAD_DIM:(g + 1) * HEAD_DIM] for g in range(GROUP)], axis=0)
            s = lax.dot_general(qs, kf, (((1,), (1,)), ((), ())), preferred_element_type=F32)
            s = s + tbl_ref[0, hh, sb]
            m = jnp.max(s, axis=-1, keepdims=True)
            p = jnp.exp(s - m)
            l = jnp.sum(p, axis=-1, keepdims=True)
            outs.append(_mm(p.astype(BF16), vf) / l)
        o_ref[:, hh * qw:(hh + 1) * qw] = jnp.concatenate(
            [jnp.concatenate([o[g * SUB:(g + 1) * SUB] for o in outs], axis=0) for g in range(GROUP)],
            axis=1).astype(o_ref.dtype)


def _attention(q, k, v, k_meta, v_meta, tbl, *, batch, seq, hps):
    rows, d = q.shape
    n_kv = k.shape[0]
    n_blk = seq // BLOCK
    qw = hps * GROUP * HEAD_DIM
    row_blk = lambda h, b, n: b * n_blk + n
    prev_blk = lambda h, b, n: b * n_blk + jnp.maximum(n - 1, 0)
    kv_spec = lambda fn: pl.BlockSpec((hps, BLOCK, HEAD_DIM), lambda h, b, n: (h, fn(h, b, n), 0))
    meta_spec = pl.BlockSpec((hps, N_META, HEAD_DIM), lambda h, b, n: (h, 0, 0))
    pipelined = (2 * _nbytes((BLOCK, qw), BF16) + 4 * _nbytes((hps, BLOCK, V7X_LANES), BF16)
                 + _nbytes((hps, GROUP * BLOCK, SLOTS), F32))
    resident = 6 * hps * _nbytes((GROUP * BLOCK, SLOTS), F32)
    return pl.pallas_call(
        _attn_body,
        grid=(n_kv // hps, batch, n_blk),
        in_specs=[
            pl.BlockSpec((BLOCK, qw), lambda h, b, n: (row_blk(h, b, n), h)),
            kv_spec(prev_blk), kv_spec(row_blk), kv_spec(prev_blk), kv_spec(row_blk),
            meta_spec, meta_spec,
            pl.BlockSpec((1, hps, N_SUB, GROUP * SUB, SLOTS),
                         lambda h, b, n: (jnp.minimum(n, 1), h, 0, 0, 0)),
        ],
        out_specs=pl.BlockSpec((BLOCK, qw), lambda h, b, n: (row_blk(h, b, n), h)),
        out_shape=jax.ShapeDtypeStruct((rows, d), BF16),
        compiler_params=pltpu.CompilerParams(
            dimension_semantics=("parallel", "parallel", "parallel"),
            vmem_limit_bytes=_vmem_limit(pipelined, resident)),
        name="swa_attention",
    )(q, k, k, v, v, k_meta, v_meta, tbl)


def _oproj_body(a_ref, w_ref, b_ref, gpost_ref, h_ref, o_ref):
    o_ref[...] = _mm(a_ref[...], w_ref[...]) + b_ref[...]
    _post_norm_residual(o_ref, h_ref, gpost_ref)


def _oproj(a, w_o, b_o, gpost, h, *, tm):
    rows, d = h.shape
    vec = lambda: pl.BlockSpec((1, d), lambda i: (0, 0))
    pipelined = _nbytes((tm, d), BF16) + 2 * _nbytes((tm, d), F32)
    resident = _nbytes((d, d), BF16) + 3 * _nbytes((tm, d), F32)
    return pl.pallas_call(
        _oproj_body,
        grid=(rows // tm,),
        in_specs=[
            pl.BlockSpec((tm, d), lambda i: (i, 0)),
            _resident((d, d)),
            vec(), vec(),
            pl.BlockSpec((tm, d), lambda i: (i, 0)),
        ],
        out_specs=pl.BlockSpec((tm, d), lambda i: (i, 0)),
        out_shape=jax.ShapeDtypeStruct((rows, d), F32),
        compiler_params=pltpu.CompilerParams(
            dimension_semantics=("parallel",),
            vmem_limit_bytes=_vmem_limit(pipelined, resident)),
        name="attn_out_proj",
    )(a, w_o, b_o, gpost, h)


def _t5_bucket(dist):
    max_exact = N_BUCKETS // 2
    dd = jnp.maximum(dist, max_exact).astype(F32)
    large = max_exact + (jnp.log(dd / max_exact) / math.log(MAX_DISTANCE / max_exact)
                         * (N_BUCKETS - max_exact)).astype(jnp.int32)
    return jnp.where(dist < max_exact, dist, jnp.minimum(large, N_BUCKETS - 1))


def _slot_table(rel_bias, sinks, n_kv):
    n_heads = rel_bias.shape[1]
    rb = rel_bias.astype(F32)

    def lookup(bucket):
        onehot = (bucket[..., None] == jnp.arange(N_BUCKETS)).astype(F32)
        return jnp.einsum("...b,bh->h...", onehot, rb, precision=lax.Precision.HIGHEST)

    qpos = jnp.arange(BLOCK)[:, None]
    kpos = jnp.arange(2 * BLOCK)[None, :]
    mpos = jnp.arange(N_META)[None, :]
    d_band = BLOCK + qpos - kpos
    in_window = (d_band >= 0) & (d_band < WINDOW)
    bias_band = lookup(_t5_bucket(jnp.maximum(d_band, 0)))
    sink = jnp.broadcast_to(sinks.astype(F32)[:, None, None], (n_heads, SUB, 1))
    pad = jnp.full((n_heads, SUB, SLOTS - BAND_SLOTS - N_META - 1), NEG_INF, F32)
    tables = []
    for n in (0, 1):
        valid = in_window & (n * BLOCK + kpos >= BLOCK)
        band = jnp.where(valid[None], bias_band, NEG_INF)
        meta = lookup(_t5_bucket(N_META + n * BLOCK + qpos - mpos))
        subs = []
        for sb in range(N_SUB):
            lo = sb * SUB
            subs.append(jnp.concatenate([band[:, lo:lo + SUB, lo:lo + BAND_SLOTS],
                                         meta[:, lo:lo + SUB], sink, pad], axis=2))
        t = jnp.stack(subs, axis=1).reshape(n_kv, GROUP, N_SUB, SUB, SLOTS)
        tables.append(jnp.transpose(t, (0, 2, 1, 3, 4)).reshape(n_kv, N_SUB, GROUP * SUB, SLOTS))
    return jnp.stack(tables)


def kernel(x, meta_tokens, rel_bias, conv_w_in, conv_b_in, conv_w_dw, conv_b_dw, conv_ln_g, conv_ln_b,
           conv_w_out, conv_b_out, attn_w_qkv, attn_b_qkv, attn_sinks, attn_w_o, attn_b_o,
           norm_mix_pre, norm_mix_post, norm_ffn_pre, norm_ffn_post, ffn_w_gate, ffn_w_up, ffn_w_down):
    batch, seq, d = x.shape
    n_kv = (attn_w_qkv.shape[2] - d) // (2 * HEAD_DIM)
    rows = batch * seq
    row = lambda a: a.reshape(1, -1).astype(F32)
    bf = functools.partial(_to_bf16, kb=CAST_ROWS)

    tm_mm, tn_mm, tm_conv, tm_ffn, tf = 1024, 512, 512, 1024, 512
    tm_meta = N_META

    h0 = x.reshape(rows, d)
    hm0 = meta_tokens.astype(x.dtype)
    w_gate, w_up, w_down = bf(ffn_w_gate), bf(ffn_w_up), bf(ffn_w_down)
    ffn = lambda h, layer, tm: _ffn(h, row(norm_ffn_pre[layer]), w_gate, w_up, w_down,
                                    row(norm_ffn_post[layer]), layer=layer, tm=tm, tf=tf)

    w_in, b_in = bf(conv_w_in)[0], row(conv_b_in[0])
    conv_args = (conv_w_dw[0].astype(F32), row(conv_b_dw[0]), row(conv_ln_g[0]), row(conv_ln_b[0]),
                 bf(conv_w_out)[0], row(conv_b_out[0]), row(norm_mix_post[0]))
    gpre0 = row(norm_mix_pre[0])

    vm = _conv_in(hm0, gpre0, w_in, b_in, tm=tm_meta, tn=tn_mm)
    zero_halo = jnp.zeros((HALO_ROWS, d), BF16)
    hm1 = _conv_out(vm, zero_halo, zero_halo, *conv_args, hm0, tm=tm_meta, rows_per_seq=N_META)
    hm2 = ffn(hm1, 0, tm_meta)

    v = _conv_in(h0, gpre0, w_in, b_in, tm=tm_mm, tn=tn_mm)
    halo0 = jnp.concatenate([jnp.zeros((HALO_ROWS - N_META, d), BF16), vm], axis=0)
    h1 = _conv_out(v, v, halo0, *conv_args, h0, tm=tm_conv, rows_per_seq=seq)
    h2 = ffn(h1, 0, tm_ffn)

    w_qkv, b_qkv = bf(attn_w_qkv)[0], row(attn_b_qkv[0])
    gpre1 = row(norm_mix_pre[1])
    _, k_meta, v_meta = _qkv(hm2, gpre1, w_qkv, b_qkv, tm=tm_meta, tn=tn_mm, n_kv=n_kv)
    q, k, vv = _qkv(h2, gpre1, w_qkv, b_qkv, tm=tm_mm, tn=tn_mm, n_kv=n_kv)
    tbl = _slot_table(rel_bias, attn_sinks[0], n_kv)
    a = _attention(q, k, vv, k_meta, v_meta, tbl, batch=batch, seq=seq, hps=ATTN_KV_HEADS_PER_STEP)
    h3 = _oproj(a, bf(attn_w_o)[0], row(attn_b_o[0]), row(norm_mix_post[1]), h2, tm=tm_mm)
    h4 = ffn(h3, 1, tm_ffn)
    return h4.reshape(batch, seq, d)
```

```python
import functools
import math

import jax
import jax.numpy as jnp
from jax import lax
from jax.experimental import pallas as pl
from jax.experimental.pallas import tpu as pltpu

N_META = 16
CONV_WIDTH = 31
HEAD_DIM = 64
GROUP = 8
WINDOW = 128
BLOCK = 128
N_BUCKETS = 32
MAX_DISTANCE = 128
RMS_EPS = 1e-6
LN_EPS = 1e-5
NEG_INF = -1e30

V7X_LANES = 128
V7X_SUBLANES = 8
V7X_BF16_SUBLANE_ROWS = 16
ATTN_KV_HEADS_PER_STEP = 2
CAST_ROWS = 256
FFN_RESULT_COLS = 256
FFN_TAIL_ROWS = 256
NORM_ROWS = 128
CONV_ROW_CHUNK = 256
V7X_VMEM_BYTES = 64 * 1024 * 1024
V7X_VMEM_REQUEST_CAP = V7X_VMEM_BYTES - 6 * 1024 * 1024
VMEM_SPILL_FRACTION = 4
VMEM_PIPELINE_STATE_BYTES = 2 * 1024 * 1024

HALO_ROWS = 2 * V7X_BF16_SUBLANE_ROWS
SUB = BLOCK // 2
N_SUB = BLOCK // SUB
BAND_SLOTS = BLOCK + SUB
SLOTS = 2 * V7X_LANES
BF16 = jnp.bfloat16
F32 = jnp.float32


def _vmem_limit(pipelined_bytes, resident_bytes):
    need = 2 * pipelined_bytes + resident_bytes
    need += need // VMEM_SPILL_FRACTION + VMEM_PIPELINE_STATE_BYTES
    return int(min(V7X_VMEM_REQUEST_CAP, need))


def _nbytes(shape, dtype):
    return math.prod(shape) * jnp.dtype(dtype).itemsize


def _rms(x, g):
    return x * lax.rsqrt(jnp.mean(x * x, axis=-1, keepdims=True) + RMS_EPS) * g


def _sigmoid(x):
    return 1.0 / (1.0 + jnp.exp(-x))


def _mm(a, b):
    return jnp.dot(a, b, preferred_element_type=F32)


def _row_groups(rows):
    step = min(rows, NORM_ROWS)
    return [slice(r, r + step) for r in range(0, rows, step)]


def _pre_norm(h_ref, g_ref, xn_ref):
    for rs in _row_groups(h_ref.shape[0]):
        xn_ref[rs, :] = _rms(h_ref[rs, :], g_ref[...]).astype(BF16)


def _post_norm_residual(o_ref, h_ref, g_ref):
    for rs in _row_groups(o_ref.shape[0]):
        o_ref[rs, :] = h_ref[rs, :] + _rms(o_ref[rs, :], g_ref[...])


def _cast_body(x_ref, o_ref):
    o_ref[...] = x_ref[...].astype(o_ref.dtype)


def _to_bf16(w, *, kb):
    n_l, k, n = w.shape
    blk = pl.BlockSpec((None, kb, n), lambda l, i: (l, i, 0))
    return pl.pallas_call(
        _cast_body,
        grid=(n_l, k // kb),
        in_specs=[blk],
        out_specs=blk,
        out_shape=jax.ShapeDtypeStruct(w.shape, BF16),
        compiler_params=pltpu.CompilerParams(
            dimension_semantics=("parallel", "parallel"),
            vmem_limit_bytes=_vmem_limit(_nbytes((kb, n), F32) + _nbytes((kb, n), BF16), 0)),
        name="cast_bf16",
    )(w)


def _resident(shape):
    return pl.BlockSpec(shape, lambda i: (0,) * len(shape), pipeline_mode=pl.Buffered(1))


def _conv_in_body(h_ref, gpre_ref, w_ref, b_ref, o_ref, xn_ref, *, tn):
    d = h_ref.shape[1]
    _pre_norm(h_ref, gpre_ref, xn_ref)
    xn = xn_ref[...]
    for j in range(d // tn):
        ca, cg = slice(j * tn, (j + 1) * tn), slice(d + j * tn, d + (j + 1) * tn)
        a = _mm(xn, w_ref[:, ca]) + b_ref[:, ca]
        g = _mm(xn, w_ref[:, cg]) + b_ref[:, cg]
        o_ref[:, ca] = (a * _sigmoid(g)).astype(o_ref.dtype)


def _conv_in(h, gpre, w_in, b_in, *, tm, tn):
    rows, d = h.shape
    pipelined = _nbytes((tm, d), F32) + _nbytes((tm, d), BF16)
    resident = _nbytes(w_in.shape, BF16) + _nbytes((tm, d), BF16) + 6 * _nbytes((tm, tn), F32)
    return pl.pallas_call(
        functools.partial(_conv_in_body, tn=tn),
        grid=(rows // tm,),
        in_specs=[
            pl.BlockSpec((tm, d), lambda i: (i, 0)),
            _resident((1, d)), _resident(w_in.shape), _resident(b_in.shape),
        ],
        out_specs=pl.BlockSpec((tm, d), lambda i: (i, 0)),
        out_shape=jax.ShapeDtypeStruct((rows, d), BF16),
        scratch_shapes=[pltpu.VMEM((tm, d), BF16)],
        compiler_params=pltpu.CompilerParams(
            dimension_semantics=("parallel",),
            vmem_limit_bytes=_vmem_limit(pipelined, resident)),
        name="conv_in",
    )(h, gpre, w_in, b_in)


def _conv_out_body(v_ref, vprev_ref, halo0_ref, wdw_ref, bdw_ref, lng_ref, lnb_ref, wout_ref,
                   bout_ref, gpost_ref, h_ref, o_ref, xbuf_ref, u_ref, y_ref, *, tm, rt, blocks_per_seq):
    n_slab, _, cw = xbuf_ref.shape
    first = pl.program_id(0) % blocks_per_seq == 0

    def fill_halo(src_ref):
        for c in range(n_slab):
            xbuf_ref[c, pl.ds(0, HALO_ROWS, stride=2), :] = src_ref[:, c * cw:(c + 1) * cw].astype(F32)

    pl.when(first)(functools.partial(fill_halo, halo0_ref))
    pl.when(jnp.logical_not(first))(functools.partial(fill_halo, vprev_ref))
    for c in range(n_slab):
        xbuf_ref[c, pl.ds(2 * HALO_ROWS, tm, stride=2), :] = v_ref[:, c * cw:(c + 1) * cw].astype(F32)

    def chunk(c, carry):
        ls = pl.ds(pl.multiple_of(c * cw, cw), cw)
        for r0 in range(0, tm, rt):
            acc = jnp.broadcast_to(bdw_ref[:, ls], (rt, cw)).reshape(rt // V7X_SUBLANES, V7X_SUBLANES, cw)
            for m in range(CONV_WIDTH):
                xs = xbuf_ref[c, pl.ds(2 * (HALO_ROWS + r0 - m), rt, stride=2), :]
                acc = acc + xs.reshape(acc.shape) * wdw_ref[c, CONV_WIDTH - 1 - m][None]
            u_ref[r0:r0 + rt, ls] = acc.reshape(rt, cw)
        return carry

    lax.fori_loop(0, n_slab, chunk, 0)

    for rs in _row_groups(tm):
        u = u_ref[rs, :]
        mu = jnp.mean(u, axis=-1, keepdims=True)
        uc = u - mu
        var = jnp.mean(uc * uc, axis=-1, keepdims=True)
        y = uc * lax.rsqrt(var + LN_EPS) * lng_ref[...] + lnb_ref[...]
        y_ref[rs, :] = (y * _sigmoid(y)).astype(BF16)
    o_ref[...] = _mm(y_ref[...], wout_ref[...]) + bout_ref[...]
    _post_norm_residual(o_ref, h_ref, gpost_ref)


def _conv_out(v, vprev_src, halo0, w_dw, b_dw, ln_g, ln_b, w_out, b_out, gpost, h, *, tm,
              rows_per_seq):
    rows, d = h.shape
    cw = V7X_LANES
    n_slab = d // cw
    body = functools.partial(_conv_out_body, tm=tm, rt=min(tm, CONV_ROW_CHUNK),
                             blocks_per_seq=rows_per_seq // tm)
    halo_blocks_per_tile = tm // HALO_ROWS
    w_slabs = jnp.broadcast_to(jnp.transpose(w_dw.reshape(CONV_WIDTH, n_slab, cw), (1, 0, 2))[:, :, None, :],
                               (n_slab, CONV_WIDTH, V7X_SUBLANES, cw))
    vec = lambda: pl.BlockSpec((1, d), lambda i: (0, 0))
    pipelined = (_nbytes((tm, d), BF16) + 2 * _nbytes((HALO_ROWS, d), BF16)
                 + _nbytes((HALO_ROWS, d), F32) + 2 * _nbytes((tm, d), F32))
    resident = (_nbytes((d, d), BF16) + 2 * _nbytes((tm + HALO_ROWS, d), F32)
                + 4 * _nbytes((tm, d), F32))
    return pl.pallas_call(
        body,
        grid=(rows // tm,),
        in_specs=[
            pl.BlockSpec((tm, d), lambda i: (i, 0)),
            pl.BlockSpec((HALO_ROWS, d), lambda i: (jnp.maximum(i * halo_blocks_per_tile - 1, 0), 0)),
            pl.BlockSpec((HALO_ROWS, d), lambda i: (0, 0)),
            _resident((n_slab, CONV_WIDTH, V7X_SUBLANES, cw)),
            vec(), vec(), vec(),
            _resident((d, d)),
            vec(), vec(),
            pl.BlockSpec((tm, d), lambda i: (i, 0)),
        ],
        out_specs=pl.BlockSpec((tm, d), lambda i: (i, 0)),
        out_shape=jax.ShapeDtypeStruct((rows, d), F32),
        scratch_shapes=[pltpu.VMEM((n_slab, 2 * (tm + HALO_ROWS), cw), F32), pltpu.VMEM((tm, d), F32),
                        pltpu.VMEM((tm, d), BF16)],
        compiler_params=pltpu.CompilerParams(
            dimension_semantics=("parallel",),
            vmem_limit_bytes=_vmem_limit(pipelined, resident)),
        name="conv_out",
    )(v, vprev_src, halo0, w_slabs, b_dw, ln_g, ln_b, w_out, b_out, gpost, h)


def _ffn_body(h_ref, gpre_ref, wg_ref, wu_ref, wd_ref, gpost_ref, o_ref, xn_ref, *, tc):
    f = pl.program_id(1)
    tf, d = wd_ref.shape

    last_f = pl.num_programs(1) - 1
    tm = o_ref.shape[0]

    def hidden_tile(first, last):
        xn = xn_ref[...]
        acts = []
        for c in range(0, tf, tc):
            g = _mm(xn, wg_ref[:, c:c + tc])
            u = _mm(xn, wu_ref[:, c:c + tc])
            acts.append((g * _sigmoid(g) * u).astype(BF16))
        a = jnp.concatenate(acts, axis=1)
        tr = min(tm, FFN_TAIL_ROWS) if last else tm
        for r in range(0, tm, tr):
            for c in range(0, d, tc):
                part = _mm(a[r:r + tr], wd_ref[:, c:c + tc])
                if first:
                    o_ref[r:r + tr, c:c + tc] = part
                else:
                    o_ref[r:r + tr, c:c + tc] += part
            if last:
                _post_norm_residual(o_ref.at[r:r + tr], h_ref.at[r:r + tr], gpost_ref)

    @pl.when(f == 0)
    def _():
        _pre_norm(h_ref, gpre_ref, xn_ref)
        hidden_tile(first=True, last=False)

    @pl.when((f > 0) & (f < last_f))
    def _():
        hidden_tile(first=False, last=False)

    @pl.when(f == last_f)
    def _():
        hidden_tile(first=False, last=True)


def _ffn(h, gpre, w_gate, w_up, w_down, gpost, *, layer, tm, tf):
    rows, d = h.shape
    ff = w_gate.shape[2]
    tc = min(tf, FFN_RESULT_COLS)
    assert ff // tf >= 2, "first and last hidden tile are separate grid steps"
    pipelined = 2 * _nbytes((tm, d), F32) + 3 * _nbytes((d, tf), BF16)
    resident = _nbytes((tm, d), BF16) + _nbytes((tm, tf), BF16) + 6 * _nbytes((tm, tc), F32)
    return pl.pallas_call(
        functools.partial(_ffn_body, tc=tc),
        grid=(rows // tm, ff // tf),
        in_specs=[
            pl.BlockSpec((tm, d), lambda i, f: (i, 0)),
            pl.BlockSpec((1, d), lambda i, f: (0, 0)),
            pl.BlockSpec((None, d, tf), lambda i, f: (layer, 0, f)),
            pl.BlockSpec((None, d, tf), lambda i, f: (layer, 0, f)),
            pl.BlockSpec((None, tf, d), lambda i, f: (layer, f, 0)),
            pl.BlockSpec((1, d), lambda i, f: (0, 0)),
        ],
        out_specs=pl.BlockSpec((tm, d), lambda i, f: (i, 0)),
        out_shape=jax.ShapeDtypeStruct((rows, d), F32),
        scratch_shapes=[pltpu.VMEM((tm, d), BF16)],
        compiler_params=pltpu.CompilerParams(
            dimension_semantics=("parallel", "arbitrary"),
            vmem_limit_bytes=_vmem_limit(pipelined, resident)),
        name="ffn",
    )(h, gpre, w_gate, w_up, w_down, gpost)


def _qkv_body(h_ref, gpre_ref, w_ref, b_ref, q_ref, k_ref, v_ref, xn_ref, *, tn, n_kv):
    d = h_ref.shape[1]
    _pre_norm(h_ref, gpre_ref, xn_ref)
    xn = xn_ref[...]
    for j in range(d // tn):
        cq = slice(j * tn, (j + 1) * tn)
        q_ref[:, cq] = ((_mm(xn, w_ref[:, cq]) + b_ref[:, cq]) * (HEAD_DIM ** -0.5)).astype(BF16)
    y = _mm(xn, w_ref[:, d:]) + b_ref[:, d:]
    for hh in range(n_kv):
        k_ref[hh] = y[:, hh * HEAD_DIM:(hh + 1) * HEAD_DIM].astype(BF16)
        v_ref[hh] = y[:, (n_kv + hh) * HEAD_DIM:(n_kv + hh + 1) * HEAD_DIM].astype(BF16)


def _qkv(h, gpre, w_qkv, b_qkv, *, tm, tn, n_kv):
    rows, d = h.shape
    kv_shape = jax.ShapeDtypeStruct((n_kv, rows, HEAD_DIM), BF16)
    pipelined = (_nbytes((tm, d), F32) + _nbytes((tm, d), BF16)
                 + 2 * _nbytes((n_kv, tm, V7X_LANES), BF16))
    resident = _nbytes(w_qkv.shape, BF16) + _nbytes((tm, d), BF16) + 4 * _nbytes((tm, tn), F32)
    return pl.pallas_call(
        functools.partial(_qkv_body, tn=tn, n_kv=n_kv),
        grid=(rows // tm,),
        in_specs=[
            pl.BlockSpec((tm, d), lambda i: (i, 0)),
            _resident((1, d)), _resident(w_qkv.shape), _resident(b_qkv.shape),
        ],
        out_specs=[
            pl.BlockSpec((tm, d), lambda i: (i, 0)),
            pl.BlockSpec((n_kv, tm, HEAD_DIM), lambda i: (0, i, 0)),
            pl.BlockSpec((n_kv, tm, HEAD_DIM), lambda i: (0, i, 0)),
        ],
        out_shape=[jax.ShapeDtypeStruct((rows, d), BF16), kv_shape, kv_shape],
        scratch_shapes=[pltpu.VMEM((tm, d), BF16)],
        compiler_params=pltpu.CompilerParams(
            dimension_semantics=("parallel",),
            vmem_limit_bytes=_vmem_limit(pipelined, resident)),
        name="qkv",
    )(h, gpre, w_qkv, b_qkv)


def _attn_body(q_ref, kp_ref, kc_ref, vp_ref, vc_ref, km_ref, vm_ref, tbl_ref, o_ref):
    qw = GROUP * HEAD_DIM
    zpad = jnp.zeros((SLOTS - BAND_SLOTS - N_META, HEAD_DIM), BF16)
    for hh in range(kp_ref.shape[0]):
        outs = []
        for sb in range(N_SUB):
            lo = sb * SUB
            band = lambda p_ref, c_ref: jnp.concatenate(
                [p_ref[hh, lo:BLOCK], c_ref[hh, 0:lo + SUB]], axis=0)
            kf = jnp.concatenate([band(kp_ref, kc_ref), km_ref[hh], zpad], axis=0)
            vf = jnp.concatenate([band(vp_ref, vc_ref), vm_ref[hh], zpad], axis=0)
            q = q_ref[lo:lo + SUB, hh * qw:(hh + 1) * qw]
            qs = jnp.concatenate([q[:, g * HEAD_DIM:(g + 1) * HEAD_DIM] for g in range(GROUP)], axis=0)
            s = lax.dot_general(qs, kf, (((1,), (1,)), ((), ())), preferred_element_type=F32)
            s = s + tbl_ref[0, hh, sb]
            m = jnp.max(s, axis=-1, keepdims=True)
            p = jnp.exp(s - m)
            l = jnp.sum(p, axis=-1, keepdims=True)
            outs.append(_mm(p.astype(BF16), vf) / l)
        o_ref[:, hh * qw:(hh + 1) * qw] = jnp.concatenate(
            [jnp.concatenate([o[g * SUB:(g + 1) * SUB] for o in outs], axis=0) for g in range(GROUP)],
            axis=1).astype(o_ref.dtype)


def _attention(q, k, v, k_meta, v_meta, tbl, *, batch, seq, hps):
    rows, d = q.shape
    n_kv = k.shape[0]
    n_blk = seq // BLOCK
    qw = hps * GROUP * HEAD_DIM
    row_blk = lambda h, b, n: b * n_blk + n
    prev_blk = lambda h, b, n: b * n_blk + jnp.maximum(n - 1, 0)
    kv_spec = lambda fn: pl.BlockSpec((hps, BLOCK, HEAD_DIM), lambda h, b, n: (h, fn(h, b, n), 0))
    meta_spec = pl.BlockSpec((hps, N_META, HEAD_DIM), lambda h, b, n: (h, 0, 0))
    pipelined = (2 * _nbytes((BLOCK, qw), BF16) + 4 * _nbytes((hps, BLOCK, V7X_LANES), BF16)
                 + _nbytes((hps, GROUP * BLOCK, SLOTS), F32))
    resident = 6 * hps * _nbytes((GROUP * BLOCK, SLOTS), F32)
    return pl.pallas_call(
        _attn_body,
        grid=(n_kv // hps, batch, n_blk),
        in_specs=[
            pl.BlockSpec((BLOCK, qw), lambda h, b, n: (row_blk(h, b, n), h)),
            kv_spec(prev_blk), kv_spec(row_blk), kv_spec(prev_blk), kv_spec(row_blk),
            meta_spec, meta_spec,
            pl.BlockSpec((1, hps, N_SUB, GROUP * SUB, SLOTS),
                         lambda h, b, n: (jnp.minimum(n, 1), h, 0, 0, 0)),
        ],
        out_specs=pl.BlockSpec((BLOCK, qw), lambda h, b, n: (row_blk(h, b, n), h)),
        out_shape=jax.ShapeDtypeStruct((rows, d), BF16),
        compiler_params=pltpu.CompilerParams(
            dimension_semantics=("parallel", "parallel", "parallel"),
            vmem_limit_bytes=_vmem_limit(pipelined, resident)),
        name="swa_attention",
    )(q, k, k, v, v, k_meta, v_meta, tbl)


def _oproj_body(a_ref, w_ref, b_ref, gpost_ref, h_ref, o_ref):
    o_ref[...] = _mm(a_ref[...], w_ref[...]) + b_ref[...]
    _post_norm_residual(o_ref, h_ref, gpost_ref)


def _oproj(a, w_o, b_o, gpost, h, *, tm):
    rows, d = h.shape
    vec = lambda: pl.BlockSpec((1, d), lambda i: (0, 0))
    pipelined = _nbytes((tm, d), BF16) + 2 * _nbytes((tm, d), F32)
    resident = _nbytes((d, d), BF16) + 3 * _nbytes((tm, d), F32)
    return pl.pallas_call(
        _oproj_body,
        grid=(rows // tm,),
        in_specs=[
            pl.BlockSpec((tm, d), lambda i: (i, 0)),
            _resident((d, d)),
            vec(), vec(),
            pl.BlockSpec((tm, d), lambda i: (i, 0)),
        ],
        out_specs=pl.BlockSpec((tm, d), lambda i: (i, 0)),
        out_shape=jax.ShapeDtypeStruct((rows, d), F32),
        compiler_params=pltpu.CompilerParams(
            dimension_semantics=("parallel",),
            vmem_limit_bytes=_vmem_limit(pipelined, resident)),
        name="attn_out_proj",
    )(a, w_o, b_o, gpost, h)


def _t5_bucket(dist):
    max_exact = N_BUCKETS // 2
    dd = jnp.maximum(dist, max_exact).astype(F32)
    large = max_exact + (jnp.log(dd / max_exact) / math.log(MAX_DISTANCE / max_exact)
                         * (N_BUCKETS - max_exact)).astype(jnp.int32)
    return jnp.where(dist < max_exact, dist, jnp.minimum(large, N_BUCKETS - 1))


def _slot_table(rel_bias, sinks, n_kv):
    n_heads = rel_bias.shape[1]
    rb = rel_bias.astype(F32)

    def lookup(bucket):
        onehot = (bucket[..., None] == jnp.arange(N_BUCKETS)).astype(F32)
        return jnp.einsum("...b,bh->h...", onehot, rb, precision=lax.Precision.HIGHEST)

    qpos = jnp.arange(BLOCK)[:, None]
    kpos = jnp.arange(2 * BLOCK)[None, :]
    mpos = jnp.arange(N_META)[None, :]
    d_band = BLOCK + qpos - kpos
    in_window = (d_band >= 0) & (d_band < WINDOW)
    bias_band = lookup(_t5_bucket(jnp.maximum(d_band, 0)))
    sink = jnp.broadcast_to(sinks.astype(F32)[:, None, None], (n_heads, SUB, 1))
    pad = jnp.full((n_heads, SUB, SLOTS - BAND_SLOTS - N_META - 1), NEG_INF, F32)
    tables = []
    for n in (0, 1):
        valid = in_window & (n * BLOCK + kpos >= BLOCK)
        band = jnp.where(valid[None], bias_band, NEG_INF)
        meta = lookup(_t5_bucket(N_META + n * BLOCK + qpos - mpos))
        subs = []
        for sb in range(N_SUB):
            lo = sb * SUB
            subs.append(jnp.concatenate([band[:, lo:lo + SUB, lo:lo + BAND_SLOTS],
                                         meta[:, lo:lo + SUB], sink, pad], axis=2))
        t = jnp.stack(subs, axis=1).reshape(n_kv, GROUP, N_SUB, SUB, SLOTS)
        tables.append(jnp.transpose(t, (0, 2, 1, 3, 4)).reshape(n_kv, N_SUB, GROUP * SUB, SLOTS))
    return jnp.stack(tables)


def kernel(x, meta_tokens, rel_bias, conv_w_in, conv_b_in, conv_w_dw, conv_b_dw, conv_ln_g, conv_ln_b,
           conv_w_out, conv_b_out, attn_w_qkv, attn_b_qkv, attn_sinks, attn_w_o, attn_b_o,
           norm_mix_pre, norm_mix_post, norm_ffn_pre, norm_ffn_post, ffn_w_gate, ffn_w_up, ffn_w_down):
    batch, seq, d = x.shape
    n_kv = (attn_w_qkv.shape[2] - d) // (2 * HEAD_DIM)
    rows = batch * seq
    row = lambda a: a.reshape(1, -1).astype(F32)
    bf = functools.partial(_to_bf16, kb=CAST_ROWS)

    tm_mm, tn_mm, tm_conv, tm_ffn, tf = 1024, 512, 512, 1024, 512
    tm_meta = N_META

    h0 = x.reshape(rows, d)
    hm0 = meta_tokens.astype(x.dtype)
    w_gate, w_up, w_down = bf(ffn_w_gate), bf(ffn_w_up), bf(ffn_w_down)
    ffn = lambda h, layer, tm: _ffn(h, row(norm_ffn_pre[layer]), w_gate, w_up, w_down,
                                    row(norm_ffn_post[layer]), layer=layer, tm=tm, tf=tf)

    w_in, b_in = bf(conv_w_in)[0], row(conv_b_in[0])
    conv_args = (conv_w_dw[0].astype(F32), row(conv_b_dw[0]), row(conv_ln_g[0]), row(conv_ln_b[0]),
                 bf(conv_w_out)[0], row(conv_b_out[0]), row(norm_mix_post[0]))
    gpre0 = row(norm_mix_pre[0])

    vm = _conv_in(hm0, gpre0, w_in, b_in, tm=tm_meta, tn=tn_mm)
    zero_halo = jnp.zeros((HALO_ROWS, d), BF16)
    hm1 = _conv_out(vm, zero_halo, zero_halo, *conv_args, hm0, tm=tm_meta, rows_per_seq=N_META)
    hm2 = ffn(hm1, 0, tm_meta)

    v = _conv_in(h0, gpre0, w_in, b_in, tm=tm_mm, tn=tn_mm)
    halo0 = jnp.concatenate([jnp.zeros((HALO_ROWS - N_META, d), BF16), vm], axis=0)
    h1 = _conv_out(v, v, halo0, *conv_args, h0, tm=tm_conv, rows_per_seq=seq)
    h2 = ffn(h1, 0, tm_ffn)

    w_qkv, b_qkv = bf(attn_w_qkv)[0], row(attn_b_qkv[0])
    gpre1 = row(norm_mix_pre[1])
    _, k_meta, v_meta = _qkv(hm2, gpre1, w_qkv, b_qkv, tm=tm_meta, tn=tn_mm, n_kv=n_kv)
    q, k, vv = _qkv(h2, gpre1, w_qkv, b_qkv, tm=tm_mm, tn=tn_mm, n_kv=n_kv)
    tbl = _slot_table(rel_bias, attn_sinks[0], n_kv)
    a = _attention(q, k, vv, k_meta, v_meta, tbl, batch=batch, seq=seq, hps=ATTN_KV_HEADS_PER_STEP)
    h3 = _oproj(a, bf(attn_w_o)[0], row(attn_b_o[0]), row(norm_mix_post[1]), h2, tm=tm_mm)
    h4 = ffn(h3, 1, tm_ffn)
    return h4.reshape(batch, seq, d)
```

```python
import functools
import math

import jax
import jax.numpy as jnp
from jax import lax
from jax.experimental import pallas as pl
from jax.experimental.pallas import tpu as pltpu

N_META = 16
CONV_WIDTH = 31
HEAD_DIM = 64
GROUP = 8
WINDOW = 128
BLOCK = 128
N_BUCKETS = 32
MAX_DISTANCE = 128
RMS_EPS = 1e-6
LN_EPS = 1e-5
NEG_INF = -1e30

V7X_LANES = 128
V7X_SUBLANES = 8
V7X_BF16_SUBLANE_ROWS = 16
ATTN_KV_HEADS_PER_STEP = 4
CAST_ROWS = 256
FFN_RESULT_COLS = 256
FFN_TAIL_ROWS = 256
NORM_ROWS = 128
CONV_ROW_CHUNK = 256
V7X_VMEM_BYTES = 64 * 1024 * 1024
V7X_VMEM_REQUEST_CAP = V7X_VMEM_BYTES - 6 * 1024 * 1024
VMEM_SPILL_FRACTION = 4
VMEM_PIPELINE_STATE_BYTES = 2 * 1024 * 1024

HALO_ROWS = 2 * V7X_BF16_SUBLANE_ROWS
SUB = BLOCK // 2
N_SUB = BLOCK // SUB
BAND_SLOTS = BLOCK + SUB
SLOTS = 2 * V7X_LANES
BF16 = jnp.bfloat16
F32 = jnp.float32


def _vmem_limit(pipelined_bytes, resident_bytes):
    need = 2 * pipelined_bytes + resident_bytes
    need += need // VMEM_SPILL_FRACTION + VMEM_PIPELINE_STATE_BYTES
    return int(min(V7X_VMEM_REQUEST_CAP, need))


def _nbytes(shape, dtype):
    return math.prod(shape) * jnp.dtype(dtype).itemsize


def _rms(x, g):
    return x * lax.rsqrt(jnp.mean(x * x, axis=-1, keepdims=True) + RMS_EPS) * g


def _sigmoid(x):
    return 1.0 / (1.0 + jnp.exp(-x))


def _mm(a, b):
    return jnp.dot(a, b, preferred_element_type=F32)


def _row_groups(rows):
    step = min(rows, NORM_ROWS)
    return [slice(r, r + step) for r in range(0, rows, step)]


def _pre_norm(h_ref, g_ref, xn_ref):
    for rs in _row_groups(h_ref.shape[0]):
        xn_ref[rs, :] = _rms(h_ref[rs, :], g_ref[...]).astype(BF16)


def _post_norm_residual(o_ref, h_ref, g_ref):
    for rs in _row_groups(o_ref.shape[0]):
        o_ref[rs, :] = h_ref[rs, :] + _rms(o_ref[rs, :], g_ref[...])


def _cast_body(x_ref, o_ref):
    o_ref[...] = x_ref[...].astype(o_ref.dtype)


def _to_bf16(w, *, kb):
    n_l, k, n = w.shape
    blk = pl.BlockSpec((None, kb, n), lambda l, i: (l, i, 0))
    return pl.pallas_call(
        _cast_body,
        grid=(n_l, k // kb),
        in_specs=[blk],
        out_specs=blk,
        out_shape=jax.ShapeDtypeStruct(w.shape, BF16),
        compiler_params=pltpu.CompilerParams(
            dimension_semantics=("parallel", "parallel"),
            vmem_limit_bytes=_vmem_limit(_nbytes((kb, n), F32) + _nbytes((kb, n), BF16), 0)),
        name="cast_bf16",
    )(w)


def _resident(shape):
    return pl.BlockSpec(shape, lambda i: (0,) * len(shape), pipeline_mode=pl.Buffered(1))


def _conv_in_body(h_ref, gpre_ref, w_ref, b_ref, o_ref, xn_ref, *, tn):
    d = h_ref.shape[1]
    _pre_norm(h_ref, gpre_ref, xn_ref)
    xn = xn_ref[...]
    for j in range(d // tn):
        ca, cg = slice(j * tn, (j + 1) * tn), slice(d + j * tn, d + (j + 1) * tn)
        a = _mm(xn, w_ref[:, ca]) + b_ref[:, ca]
        g = _mm(xn, w_ref[:, cg]) + b_ref[:, cg]
        o_ref[:, ca] = (a * _sigmoid(g)).astype(o_ref.dtype)


def _conv_in(h, gpre, w_in, b_in, *, tm, tn):
    rows, d = h.shape
    pipelined = _nbytes((tm, d), F32) + _nbytes((tm, d), BF16)
    resident = _nbytes(w_in.shape, BF16) + _nbytes((tm, d), BF16) + 6 * _nbytes((tm, tn), F32)
    return pl.pallas_call(
        functools.partial(_conv_in_body, tn=tn),
        grid=(rows // tm,),
        in_specs=[
            pl.BlockSpec((tm, d), lambda i: (i, 0)),
            _resident((1, d)), _resident(w_in.shape), _resident(b_in.shape),
        ],
        out_specs=pl.BlockSpec((tm, d), lambda i: (i, 0)),
        out_shape=jax.ShapeDtypeStruct((rows, d), BF16),
        scratch_shapes=[pltpu.VMEM((tm, d), BF16)],
        compiler_params=pltpu.CompilerParams(
            dimension_semantics=("parallel",),
            vmem_limit_bytes=_vmem_limit(pipelined, resident)),
        name="conv_in",
    )(h, gpre, w_in, b_in)


def _conv_out_body(v_ref, vprev_ref, halo0_ref, wdw_ref, bdw_ref, lng_ref, lnb_ref, wout_ref,
                   bout_ref, gpost_ref, h_ref, o_ref, xbuf_ref, u_ref, y_ref, *, tm, rt, blocks_per_seq):
    n_slab, _, cw = xbuf_ref.shape
    first = pl.program_id(0) % blocks_per_seq == 0

    def fill_halo(src_ref):
        for c in range(n_slab):
            xbuf_ref[c, pl.ds(0, HALO_ROWS, stride=2), :] = src_ref[:, c * cw:(c + 1) * cw].astype(F32)

    pl.when(first)(functools.partial(fill_halo, halo0_ref))
    pl.when(jnp.logical_not(first))(functools.partial(fill_halo, vprev_ref))
    for c in range(n_slab):
        xbuf_ref[c, pl.ds(2 * HALO_ROWS, tm, stride=2), :] = v_ref[:, c * cw:(c + 1) * cw].astype(F32)

    def chunk(c, carry):
        ls = pl.ds(pl.multiple_of(c * cw, cw), cw)
        for r0 in range(0, tm, rt):
            acc = jnp.broadcast_to(bdw_ref[:, ls], (rt, cw)).reshape(rt // V7X_SUBLANES, V7X_SUBLANES, cw)
            for m in range(CONV_WIDTH):
                xs = xbuf_ref[c, pl.ds(2 * (HALO_ROWS + r0 - m), rt, stride=2), :]
                acc = acc + xs.reshape(acc.shape) * wdw_ref[c, CONV_WIDTH - 1 - m][None]
            u_ref[r0:r0 + rt, ls] = acc.reshape(rt, cw)
        return carry

    lax.fori_loop(0, n_slab, chunk, 0)

    for rs in _row_groups(tm):
        u = u_ref[rs, :]
        mu = jnp.mean(u, axis=-1, keepdims=True)
        uc = u - mu
        var = jnp.mean(uc * uc, axis=-1, keepdims=True)
        y = uc * lax.rsqrt(var + LN_EPS) * lng_ref[...] + lnb_ref[...]
        y_ref[rs, :] = (y * _sigmoid(y)).astype(BF16)
    o_ref[...] = _mm(y_ref[...], wout_ref[...]) + bout_ref[...]
    _post_norm_residual(o_ref, h_ref, gpost_ref)


def _conv_out(v, vprev_src, halo0, w_dw, b_dw, ln_g, ln_b, w_out, b_out, gpost, h, *, tm,
              rows_per_seq):
    rows, d = h.shape
    cw = V7X_LANES
    n_slab = d // cw
    body = functools.partial(_conv_out_body, tm=tm, rt=min(tm, CONV_ROW_CHUNK),
                             blocks_per_seq=rows_per_seq // tm)
    halo_blocks_per_tile = tm // HALO_ROWS
    w_slabs = jnp.broadcast_to(jnp.transpose(w_dw.reshape(CONV_WIDTH, n_slab, cw), (1, 0, 2))[:, :, None, :],
                               (n_slab, CONV_WIDTH, V7X_SUBLANES, cw))
    vec = lambda: pl.BlockSpec((1, d), lambda i: (0, 0))
    pipelined = (_nbytes((tm, d), BF16) + 2 * _nbytes((HALO_ROWS, d), BF16)
                 + _nbytes((HALO_ROWS, d), F32) + 2 * _nbytes((tm, d), F32))
    resident = (_nbytes((d, d), BF16) + 2 * _nbytes((tm + HALO_ROWS, d), F32)
                + 4 * _nbytes((tm, d), F32))
    return pl.pallas_call(
        body,
        grid=(rows // tm,),
        in_specs=[
            pl.BlockSpec((tm, d), lambda i: (i, 0)),
            pl.BlockSpec((HALO_ROWS, d), lambda i: (jnp.maximum(i * halo_blocks_per_tile - 1, 0), 0)),
            pl.BlockSpec((HALO_ROWS, d), lambda i: (0, 0)),
            _resident((n_slab, CONV_WIDTH, V7X_SUBLANES, cw)),
            vec(), vec(), vec(),
            _resident((d, d)),
            vec(), vec(),
            pl.BlockSpec((tm, d), lambda i: (i, 0)),
        ],
        out_specs=pl.BlockSpec((tm, d), lambda i: (i, 0)),
        out_shape=jax.ShapeDtypeStruct((rows, d), F32),
        scratch_shapes=[pltpu.VMEM((n_slab, 2 * (tm + HALO_ROWS), cw), F32), pltpu.VMEM((tm, d), F32),
                        pltpu.VMEM((tm, d), BF16)],
        compiler_params=pltpu.CompilerParams(
            dimension_semantics=("parallel",),
            vmem_limit_bytes=_vmem_limit(pipelined, resident)),
        name="conv_out",
    )(v, vprev_src, halo0, w_slabs, b_dw, ln_g, ln_b, w_out, b_out, gpost, h)


def _ffn_body(h_ref, gpre_ref, wg_ref, wu_ref, wd_ref, gpost_ref, o_ref, xn_ref, *, tc):
    f = pl.program_id(1)
    tf, d = wd_ref.shape

    last_f = pl.num_programs(1) - 1
    tm = o_ref.shape[0]

    def hidden_tile(first, last):
        xn = xn_ref[...]
        acts = []
        for c in range(0, tf, tc):
            g = _mm(xn, wg_ref[:, c:c + tc])
            u = _mm(xn, wu_ref[:, c:c + tc])
            acts.append((g * _sigmoid(g) * u).astype(BF16))
        a = jnp.concatenate(acts, axis=1)
        tr = min(tm, FFN_TAIL_ROWS) if last else tm
        for r in range(0, tm, tr):
            for c in range(0, d, tc):
                part = _mm(a[r:r + tr], wd_ref[:, c:c + tc])
                if first:
                    o_ref[r:r + tr, c:c + tc] = part
                else:
                    o_ref[r:r + tr, c:c + tc] += part
            if last:
                _post_norm_residual(o_ref.at[r:r + tr], h_ref.at[r:r + tr], gpost_ref)

    @pl.when(f == 0)
    def _():
        _pre_norm(h_ref, gpre_ref, xn_ref)
        hidden_tile(first=True, last=False)

    @pl.when((f > 0) & (f < last_f))
    def _():
        hidden_tile(first=False, last=False)

    @pl.when(f == last_f)
    def _():
        hidden_tile(first=False, last=True)


def _ffn(h, gpre, w_gate, w_up, w_down, gpost, *, layer, tm, tf):
    rows, d = h.shape
    ff = w_gate.shape[2]
    tc = min(tf, FFN_RESULT_COLS)
    assert ff // tf >= 2, "first and last hidden tile are separate grid steps"
    pipelined = 2 * _nbytes((tm, d), F32) + 3 * _nbytes((d, tf), BF16)
    resident = _nbytes((tm, d), BF16) + _nbytes((tm, tf), BF16) + 6 * _nbytes((tm, tc), F32)
    return pl.pallas_call(
        functools.partial(_ffn_body, tc=tc),
        grid=(rows // tm, ff // tf),
        in_specs=[
            pl.BlockSpec((tm, d), lambda i, f: (i, 0)),
            pl.BlockSpec((1, d), lambda i, f: (0, 0)),
            pl.BlockSpec((None, d, tf), lambda i, f: (layer, 0, f)),
            pl.BlockSpec((None, d, tf), lambda i, f: (layer, 0, f)),
            pl.BlockSpec((None, tf, d), lambda i, f: (layer, f, 0)),
            pl.BlockSpec((1, d), lambda i, f: (0, 0)),
        ],
        out_specs=pl.BlockSpec((tm, d), lambda i, f: (i, 0)),
        out_shape=jax.ShapeDtypeStruct((rows, d), F32),
        scratch_shapes=[pltpu.VMEM((tm, d), BF16)],
        compiler_params=pltpu.CompilerParams(
            dimension_semantics=("parallel", "arbitrary"),
            vmem_limit_bytes=_vmem_limit(pipelined, resident)),
        name="ffn",
    )(h, gpre, w_gate, w_up, w_down, gpost)


def _qkv_body(h_ref, gpre_ref, w_ref, b_ref, q_ref, k_ref, v_ref, xn_ref, *, tn, n_kv):
    d = h_ref.shape[1]
    _pre_norm(h_ref, gpre_ref, xn_ref)
    xn = xn_ref[...]
    for j in range(d // tn):
        cq = slice(j * tn, (j + 1) * tn)
        q_ref[:, cq] = ((_mm(xn, w_ref[:, cq]) + b_ref[:, cq]) * (HEAD_DIM ** -0.5)).astype(BF16)
    y = _mm(xn, w_ref[:, d:]) + b_ref[:, d:]
    for hh in range(n_kv):
        k_ref[hh] = y[:, hh * HEAD_DIM:(hh + 1) * HEAD_DIM].astype(BF16)
        v_ref[hh] = y[:, (n_kv + hh) * HEAD_DIM:(n_kv + hh + 1) * HEAD_DIM].astype(BF16)


def _qkv(h, gpre, w_qkv, b_qkv, *, tm, tn, n_kv):
    rows, d = h.shape
    kv_shape = jax.ShapeDtypeStruct((n_kv, rows, HEAD_DIM), BF16)
    pipelined = (_nbytes((tm, d), F32) + _nbytes((tm, d), BF16)
                 + 2 * _nbytes((n_kv, tm, V7X_LANES), BF16))
    resident = _nbytes(w_qkv.shape, BF16) + _nbytes((tm, d), BF16) + 4 * _nbytes((tm, tn), F32)
    return pl.pallas_call(
        functools.partial(_qkv_body, tn=tn, n_kv=n_kv),
        grid=(rows // tm,),
        in_specs=[
            pl.BlockSpec((tm, d), lambda i: (i, 0)),
            _resident((1, d)), _resident(w_qkv.shape), _resident(b_qkv.shape),
        ],
        out_specs=[
            pl.BlockSpec((tm, d), lambda i: (i, 0)),
            pl.BlockSpec((n_kv, tm, HEAD_DIM), lambda i: (0, i, 0)),
            pl.BlockSpec((n_kv, tm, HEAD_DIM), lambda i: (0, i, 0)),
        ],
        out_shape=[jax.ShapeDtypeStruct((rows, d), BF16), kv_shape, kv_shape],
        scratch_shapes=[pltpu.VMEM((tm, d), BF16)],
        compiler_params=pltpu.CompilerParams(
            dimension_semantics=("parallel",),
            vmem_limit_bytes=_vmem_limit(pipelined, resident)),
        name="qkv",
    )(h, gpre, w_qkv, b_qkv)


def _attn_body(q_ref, kp_ref, kc_ref, vp_ref, vc_ref, km_ref, vm_ref, tbl_ref, o_ref):
    qw = GROUP * HEAD_DIM
    zpad = jnp.zeros((SLOTS - BAND_SLOTS - N_META, HEAD_DIM), BF16)
    units = [(hh, sb) for hh in range(kp_ref.shape[0]) for sb in range(N_SUB)]
    band = lambda p_ref, c_ref, hh, lo: jnp.concatenate(
        [p_ref[hh, lo:BLOCK], c_ref[hh, 0:lo + SUB]], axis=0)
    scores, vfs = [], []
    for hh, sb in units:
        lo = sb * SUB
        kf = jnp.concatenate([band(kp_ref, kc_ref, hh, lo), km_ref[hh], zpad], axis=0)
        vfs.append(jnp.concatenate([band(vp_ref, vc_ref, hh, lo), vm_ref[hh], zpad], axis=0))
        q = q_ref[lo:lo + SUB, hh * qw:(hh + 1) * qw]
        qs = jnp.concatenate([q[:, g * HEAD_DIM:(g + 1) * HEAD_DIM] for g in range(GROUP)], axis=0)
        s = lax.dot_general(qs, kf, (((1,), (1,)), ((), ())), preferred_element_type=F32)
        scores.append(s + tbl_ref[0, hh, sb])
    probs = []
    for s in scores:
        p = jnp.exp(s - jnp.max(s, axis=-1, keepdims=True))
        probs.append((p.astype(BF16), jnp.sum(p, axis=-1, keepdims=True)))
    outs = [_mm(p, vf) / l for (p, l), vf in zip(probs, vfs)]
    for hh in range(kp_ref.shape[0]):
        o_ref[:, hh * qw:(hh + 1) * qw] = jnp.concatenate(
            [jnp.concatenate([o[g * SUB:(g + 1) * SUB] for o in outs[hh * N_SUB:(hh + 1) * N_SUB]],
                             axis=0) for g in range(GROUP)], axis=1).astype(o_ref.dtype)


def _attention(q, k, v, k_meta, v_meta, tbl, *, batch, seq, hps):
    rows, d = q.shape
    n_kv = k.shape[0]
    n_blk = seq // BLOCK
    qw = hps * GROUP * HEAD_DIM
    row_blk = lambda h, b, n: b * n_blk + n
    prev_blk = lambda h, b, n: b * n_blk + jnp.maximum(n - 1, 0)
    kv_spec = lambda fn: pl.BlockSpec((hps, BLOCK, HEAD_DIM), lambda h, b, n: (h, fn(h, b, n), 0))
    meta_spec = pl.BlockSpec((hps, N_META, HEAD_DIM), lambda h, b, n: (h, 0, 0))
    pipelined = (2 * _nbytes((BLOCK, qw), BF16) + 4 * _nbytes((hps, BLOCK, V7X_LANES), BF16)
                 + _nbytes((hps, GROUP * BLOCK, SLOTS), F32))
    resident = 6 * hps * _nbytes((GROUP * BLOCK, SLOTS), F32)
    return pl.pallas_call(
        _attn_body,
        grid=(n_kv // hps, batch, n_blk),
        in_specs=[
            pl.BlockSpec((BLOCK, qw), lambda h, b, n: (row_blk(h, b, n), h)),
            kv_spec(prev_blk), kv_spec(row_blk), kv_spec(prev_blk), kv_spec(row_blk),
            meta_spec, meta_spec,
            pl.BlockSpec((1, hps, N_SUB, GROUP * SUB, SLOTS),
                         lambda h, b, n: (jnp.minimum(n, 1), h, 0, 0, 0)),
        ],
        out_specs=pl.BlockSpec((BLOCK, qw), lambda h, b, n: (row_blk(h, b, n), h)),
        out_shape=jax.ShapeDtypeStruct((rows, d), BF16),
        compiler_params=pltpu.CompilerParams(
            dimension_semantics=("parallel", "parallel", "parallel"),
            vmem_limit_bytes=_vmem_limit(pipelined, resident)),
        name="swa_attention",
    )(q, k, k, v, v, k_meta, v_meta, tbl)


def _oproj_body(a_ref, w_ref, b_ref, gpost_ref, h_ref, o_ref):
    o_ref[...] = _mm(a_ref[...], w_ref[...]) + b_ref[...]
    _post_norm_residual(o_ref, h_ref, gpost_ref)


def _oproj(a, w_o, b_o, gpost, h, *, tm):
    rows, d = h.shape
    vec = lambda: pl.BlockSpec((1, d), lambda i: (0, 0))
    pipelined = _nbytes((tm, d), BF16) + 2 * _nbytes((tm, d), F32)
    resident = _nbytes((d, d), BF16) + 3 * _nbytes((tm, d), F32)
    return pl.pallas_call(
        _oproj_body,
        grid=(rows // tm,),
        in_specs=[
            pl.BlockSpec((tm, d), lambda i: (i, 0)),
            _resident((d, d)),
            vec(), vec(),
            pl.BlockSpec((tm, d), lambda i: (i, 0)),
        ],
        out_specs=pl.BlockSpec((tm, d), lambda i: (i, 0)),
        out_shape=jax.ShapeDtypeStruct((rows, d), F32),
        compiler_params=pltpu.CompilerParams(
            dimension_semantics=("parallel",),
            vmem_limit_bytes=_vmem_limit(pipelined, resident)),
        name="attn_out_proj",
    )(a, w_o, b_o, gpost, h)


def _t5_bucket(dist):
    max_exact = N_BUCKETS // 2
    dd = jnp.maximum(dist, max_exact).astype(F32)
    large = max_exact + (jnp.log(dd / max_exact) / math.log(MAX_DISTANCE / max_exact)
                         * (N_BUCKETS - max_exact)).astype(jnp.int32)
    return jnp.where(dist < max_exact, dist, jnp.minimum(large, N_BUCKETS - 1))


def _slot_table(rel_bias, sinks, n_kv):
    n_heads = rel_bias.shape[1]
    rb = rel_bias.astype(F32)

    def lookup(bucket):
        onehot = (bucket[..., None] == jnp.arange(N_BUCKETS)).astype(F32)
        return jnp.einsum("...b,bh->h...", onehot, rb, precision=lax.Precision.HIGHEST)

    qpos = jnp.arange(BLOCK)[:, None]
    kpos = jnp.arange(2 * BLOCK)[None, :]
    mpos = jnp.arange(N_META)[None, :]
    d_band = BLOCK + qpos - kpos
    in_window = (d_band >= 0) & (d_band < WINDOW)
    bias_band = lookup(_t5_bucket(jnp.maximum(d_band, 0)))
    sink = jnp.broadcast_to(sinks.astype(F32)[:, None, None], (n_heads, SUB, 1))
    pad = jnp.full((n_heads, SUB, SLOTS - BAND_SLOTS - N_META - 1), NEG_INF, F32)
    tables = []
    for n in (0, 1):
        valid = in_window & (n * BLOCK + kpos >= BLOCK)
        band = jnp.where(valid[None], bias_band, NEG_INF)
        meta = lookup(_t5_bucket(N_META + n * BLOCK + qpos - mpos))
        subs = []
        for sb in range(N_SUB):
            lo = sb * SUB
            subs.append(jnp.concatenate([band[:, lo:lo + SUB, lo:lo + BAND_SLOTS],
                                         meta[:, lo:lo + SUB], sink, pad], axis=2))
        t = jnp.stack(subs, axis=1).reshape(n_kv, GROUP, N_SUB, SUB, SLOTS)
        tables.append(jnp.transpose(t, (0, 2, 1, 3, 4)).reshape(n_kv, N_SUB, GROUP * SUB, SLOTS))
    return jnp.stack(tables)


def kernel(x, meta_tokens, rel_bias, conv_w_in, conv_b_in, conv_w_dw, conv_b_dw, conv_ln_g, conv_ln_b,
           conv_w_out, conv_b_out, attn_w_qkv, attn_b_qkv, attn_sinks, attn_w_o, attn_b_o,
           norm_mix_pre, norm_mix_post, norm_ffn_pre, norm_ffn_post, ffn_w_gate, ffn_w_up, ffn_w_down):
    batch, seq, d = x.shape
    n_kv = (attn_w_qkv.shape[2] - d) // (2 * HEAD_DIM)
    rows = batch * seq
    row = lambda a: a.reshape(1, -1).astype(F32)
    bf = functools.partial(_to_bf16, kb=CAST_ROWS)

    tm_mm, tn_mm, tm_conv, tm_ffn, tf = 1024, 512, 512, 1024, 512
    tm_meta = N_META

    h0 = x.reshape(rows, d)
    hm0 = meta_tokens.astype(x.dtype)
    w_gate, w_up, w_down = bf(ffn_w_gate), bf(ffn_w_up), bf(ffn_w_down)
    ffn = lambda h, layer, tm: _ffn(h, row(norm_ffn_pre[layer]), w_gate, w_up, w_down,
                                    row(norm_ffn_post[layer]), layer=layer, tm=tm, tf=tf)

    w_in, b_in = bf(conv_w_in)[0], row(conv_b_in[0])
    conv_args = (conv_w_dw[0].astype(F32), row(conv_b_dw[0]), row(conv_ln_g[0]), row(conv_ln_b[0]),
                 bf(conv_w_out)[0], row(conv_b_out[0]), row(norm_mix_post[0]))
    gpre0 = row(norm_mix_pre[0])

    vm = _conv_in(hm0, gpre0, w_in, b_in, tm=tm_meta, tn=tn_mm)
    zero_halo = jnp.zeros((HALO_ROWS, d), BF16)
    hm1 = _conv_out(vm, zero_halo, zero_halo, *conv_args, hm0, tm=tm_meta, rows_per_seq=N_META)
    hm2 = ffn(hm1, 0, tm_meta)

    v = _conv_in(h0, gpre0, w_in, b_in, tm=tm_mm, tn=tn_mm)
    halo0 = jnp.concatenate([jnp.zeros((HALO_ROWS - N_META, d), BF16), vm], axis=0)
    h1 = _conv_out(v, v, halo0, *conv_args, h0, tm=tm_conv, rows_per_seq=seq)
    h2 = ffn(h1, 0, tm_ffn)

    w_qkv, b_qkv = bf(attn_w_qkv)[0], row(attn_b_qkv[0])
    gpre1 = row(norm_mix_pre[1])
    _, k_meta, v_meta = _qkv(hm2, gpre1, w_qkv, b_qkv, tm=tm_meta, tn=tn_mm, n_kv=n_kv)
    q, k, vv = _qkv(h2, gpre1, w_qkv, b_qkv, tm=tm_mm, tn=tn_mm, n_kv=n_kv)
    tbl = _slot_table(rel_bias, attn_sinks[0], n_kv)
    a = _attention(q, k, vv, k_meta, v_meta, tbl, batch=batch, seq=seq, hps=ATTN_KV_HEADS_PER_STEP)
    h3 = _oproj(a, bf(attn_w_o)[0], row(attn_b_o[0]), row(norm_mix_post[1]), h2, tm=tm_mm)
    h4 = ffn(h3, 1, tm_ffn)
    return h4.reshape(batch, seq, d)
```

```python
import functools
import math

import jax
import jax.numpy as jnp
from jax import lax
from jax.experimental import pallas as pl
from jax.experimental.pallas import tpu as pltpu

N_META = 16
CONV_WIDTH = 31
HEAD_DIM = 64
GROUP = 8
WINDOW = 128
BLOCK = 128
N_BUCKETS = 32
MAX_DISTANCE = 128
RMS_EPS = 1e-6
LN_EPS = 1e-5
NEG_INF = -1e30

V7X_LANES = 128
V7X_SUBLANES = 8
V7X_BF16_SUBLANE_ROWS = 16
ATTN_KV_HEADS_PER_STEP = 4
CAST_ROWS = 256
FFN_RESULT_COLS = 256
POINTWISE_ROWS = 256
FFN_TAIL_ROWS = 256
NORM_ROWS = 128
CONV_ROW_CHUNK = 256
V7X_VMEM_BYTES = 64 * 1024 * 1024
V7X_VMEM_REQUEST_CAP = V7X_VMEM_BYTES - 6 * 1024 * 1024
VMEM_SPILL_FRACTION = 4
VMEM_PIPELINE_STATE_BYTES = 2 * 1024 * 1024

HALO_ROWS = 2 * V7X_BF16_SUBLANE_ROWS
SUB = BLOCK // 2
N_SUB = BLOCK // SUB
BAND_SLOTS = BLOCK + SUB
SLOTS = 2 * V7X_LANES
V_EXT = 4 * HEAD_DIM
BF16 = jnp.bfloat16
F32 = jnp.float32


def _vmem_limit(pipelined_bytes, resident_bytes):
    need = 2 * pipelined_bytes + resident_bytes
    need += need // VMEM_SPILL_FRACTION + VMEM_PIPELINE_STATE_BYTES
    return int(min(V7X_VMEM_REQUEST_CAP, need))


def _nbytes(shape, dtype):
    return math.prod(shape) * jnp.dtype(dtype).itemsize


def _rms(x, g):
    return x * lax.rsqrt(jnp.mean(x * x, axis=-1, keepdims=True) + RMS_EPS) * g


def _sigmoid(x):
    return 1.0 / (1.0 + jnp.exp(-x))


def _mm(a, b):
    return jnp.dot(a, b, preferred_element_type=F32)


def _row_groups(rows):
    step = min(rows, NORM_ROWS)
    return [slice(r, r + step) for r in range(0, rows, step)]


def _pre_norm(h_ref, g_ref, xn_ref):
    for rs in _row_groups(h_ref.shape[0]):
        xn_ref[rs, :] = _rms(h_ref[rs, :], g_ref[...]).astype(BF16)


def _post_norm_residual(o_ref, h_ref, g_ref):
    for rs in _row_groups(o_ref.shape[0]):
        o_ref[rs, :] = h_ref[rs, :] + _rms(o_ref[rs, :], g_ref[...])


def _cast_body(x_ref, o_ref):
    o_ref[...] = x_ref[...].astype(o_ref.dtype)


def _to_bf16(w, *, kb):
    n_l, k, n = w.shape
    blk = pl.BlockSpec((None, kb, n), lambda l, i: (l, i, 0))
    return pl.pallas_call(
        _cast_body,
        grid=(n_l, k // kb),
        in_specs=[blk],
        out_specs=blk,
        out_shape=jax.ShapeDtypeStruct(w.shape, BF16),
        compiler_params=pltpu.CompilerParams(
            dimension_semantics=("parallel", "parallel"),
            vmem_limit_bytes=_vmem_limit(_nbytes((kb, n), F32) + _nbytes((kb, n), BF16), 0)),
        name="cast_bf16",
    )(w)


def _resident(shape):
    return pl.BlockSpec(shape, lambda i: (0,) * len(shape), pipeline_mode=pl.Buffered(1))


def _conv_in_body(h_ref, gpre_ref, w_ref, b_ref, o_ref, xn_ref, *, tn):
    d = h_ref.shape[1]
    _pre_norm(h_ref, gpre_ref, xn_ref)
    xn = xn_ref[...]
    for j in range(d // tn):
        ca, cg = slice(j * tn, (j + 1) * tn), slice(d + j * tn, d + (j + 1) * tn)
        a = _mm(xn, w_ref[:, ca]) + b_ref[:, ca]
        g = _mm(xn, w_ref[:, cg]) + b_ref[:, cg]
        o_ref[:, ca] = (a * _sigmoid(g)).astype(o_ref.dtype)


def _conv_in(h, gpre, w_in, b_in, *, tm, tn):
    rows, d = h.shape
    pipelined = _nbytes((tm, d), F32) + _nbytes((tm, d), BF16)
    resident = _nbytes(w_in.shape, BF16) + _nbytes((tm, d), BF16) + 6 * _nbytes((tm, tn), F32)
    return pl.pallas_call(
        functools.partial(_conv_in_body, tn=tn),
        grid=(rows // tm,),
        in_specs=[
            pl.BlockSpec((tm, d), lambda i: (i, 0)),
            _resident((1, d)), _resident(w_in.shape), _resident(b_in.shape),
        ],
        out_specs=pl.BlockSpec((tm, d), lambda i: (i, 0)),
        out_shape=jax.ShapeDtypeStruct((rows, d), BF16),
        scratch_shapes=[pltpu.VMEM((tm, d), BF16)],
        compiler_params=pltpu.CompilerParams(
            dimension_semantics=("parallel",),
            vmem_limit_bytes=_vmem_limit(pipelined, resident)),
        name="conv_in",
    )(h, gpre, w_in, b_in)


def _conv_out_body(v_ref, vprev_ref, halo0_ref, wdw_ref, bdw_ref, lng_ref, lnb_ref, wout_ref,
                   bout_ref, gpost_ref, h_ref, o_ref, xbuf_ref, u_ref, y_ref, *, tm, rt, blocks_per_seq):
    n_slab, _, cw = xbuf_ref.shape
    first = pl.program_id(0) % blocks_per_seq == 0

    def fill_halo(src_ref):
        for c in range(n_slab):
            xbuf_ref[c, pl.ds(0, HALO_ROWS, stride=2), :] = src_ref[:, c * cw:(c + 1) * cw].astype(F32)

    pl.when(first)(functools.partial(fill_halo, halo0_ref))
    pl.when(jnp.logical_not(first))(functools.partial(fill_halo, vprev_ref))
    for c in range(n_slab):
        xbuf_ref[c, pl.ds(2 * HALO_ROWS, tm, stride=2), :] = v_ref[:, c * cw:(c + 1) * cw].astype(F32)

    def chunk(c, carry):
        ls = pl.ds(pl.multiple_of(c * cw, cw), cw)
        for r0 in range(0, tm, rt):
            acc = jnp.broadcast_to(bdw_ref[:, ls], (rt, cw)).reshape(rt // V7X_SUBLANES, V7X_SUBLANES, cw)
            for m in range(CONV_WIDTH):
                xs = xbuf_ref[c, pl.ds(2 * (HALO_ROWS + r0 - m), rt, stride=2), :]
                acc = acc + xs.reshape(acc.shape) * wdw_ref[c, CONV_WIDTH - 1 - m][None]
            u_ref[r0:r0 + rt, ls] = acc.reshape(rt, cw)
        return carry

    lax.fori_loop(0, n_slab, chunk, 0)

    for rs in _row_groups(tm):
        u = u_ref[rs, :]
        mu = jnp.mean(u, axis=-1, keepdims=True)
        uc = u - mu
        var = jnp.mean(uc * uc, axis=-1, keepdims=True)
        y = uc * lax.rsqrt(var + LN_EPS) * lng_ref[...] + lnb_ref[...]
        y_ref[rs, :] = (y * _sigmoid(y)).astype(BF16)
    tr = min(tm, POINTWISE_ROWS)
    for r in range(0, tm, tr):
        o_ref[r:r + tr, :] = _mm(y_ref[r:r + tr, :], wout_ref[...]) + bout_ref[...]
        _post_norm_residual(o_ref.at[r:r + tr], h_ref.at[r:r + tr], gpost_ref)


def _conv_out(v, vprev_src, halo0, w_dw, b_dw, ln_g, ln_b, w_out, b_out, gpost, h, *, tm,
              rows_per_seq):
    rows, d = h.shape
    cw = V7X_LANES
    n_slab = d // cw
    body = functools.partial(_conv_out_body, tm=tm, rt=min(tm, CONV_ROW_CHUNK),
                             blocks_per_seq=rows_per_seq // tm)
    halo_blocks_per_tile = tm // HALO_ROWS
    w_slabs = jnp.broadcast_to(jnp.transpose(w_dw.reshape(CONV_WIDTH, n_slab, cw), (1, 0, 2))[:, :, None, :],
                               (n_slab, CONV_WIDTH, V7X_SUBLANES, cw))
    vec = lambda: pl.BlockSpec((1, d), lambda i: (0, 0))
    pipelined = (_nbytes((tm, d), BF16) + 2 * _nbytes((HALO_ROWS, d), BF16)
                 + _nbytes((HALO_ROWS, d), F32) + 2 * _nbytes((tm, d), F32))
    resident = (_nbytes((d, d), BF16) + 2 * _nbytes((tm + HALO_ROWS, d), F32)
                + 4 * _nbytes((tm, d), F32))
    return pl.pallas_call(
        body,
        grid=(rows // tm,),
        in_specs=[
            pl.BlockSpec((tm, d), lambda i: (i, 0)),
            pl.BlockSpec((HALO_ROWS, d), lambda i: (jnp.maximum(i * halo_blocks_per_tile - 1, 0), 0)),
            pl.BlockSpec((HALO_ROWS, d), lambda i: (0, 0)),
            _resident((n_slab, CONV_WIDTH, V7X_SUBLANES, cw)),
            vec(), vec(), vec(),
            _resident((d, d)),
            vec(), vec(),
            pl.BlockSpec((tm, d), lambda i: (i, 0)),
        ],
        out_specs=pl.BlockSpec((tm, d), lambda i: (i, 0)),
        out_shape=jax.ShapeDtypeStruct((rows, d), F32),
        scratch_shapes=[pltpu.VMEM((n_slab, 2 * (tm + HALO_ROWS), cw), F32), pltpu.VMEM((tm, d), F32),
                        pltpu.VMEM((tm, d), BF16)],
        compiler_params=pltpu.CompilerParams(
            dimension_semantics=("parallel",),
            vmem_limit_bytes=_vmem_limit(pipelined, resident)),
        name="conv_out",
    )(v, vprev_src, halo0, w_slabs, b_dw, ln_g, ln_b, w_out, b_out, gpost, h)


def _ffn_body(h_ref, gpre_ref, wg_ref, wu_ref, wd_ref, gpost_ref, o_ref, xn_ref, *, tc):
    f = pl.program_id(1)
    tf, d = wd_ref.shape

    last_f = pl.num_programs(1) - 1
    tm = o_ref.shape[0]

    def hidden_tile(first, last):
        xn = xn_ref[...]
        acts = []
        for c in range(0, tf, tc):
            g = _mm(xn, wg_ref[:, c:c + tc])
            u = _mm(xn, wu_ref[:, c:c + tc])
            acts.append((g * _sigmoid(g) * u).astype(BF16))
        a = jnp.concatenate(acts, axis=1)
        tr = min(tm, FFN_TAIL_ROWS) if last else tm
        for r in range(0, tm, tr):
            for c in range(0, d, tc):
                part = _mm(a[r:r + tr], wd_ref[:, c:c + tc])
                if first:
                    o_ref[r:r + tr, c:c + tc] = part
                else:
                    o_ref[r:r + tr, c:c + tc] += part
            if last:
                _post_norm_residual(o_ref.at[r:r + tr], h_ref.at[r:r + tr], gpost_ref)

    @pl.when(f == 0)
    def _():
        _pre_norm(h_ref, gpre_ref, xn_ref)
        hidden_tile(first=True, last=False)

    @pl.when((f > 0) & (f < last_f))
    def _():
        hidden_tile(first=False, last=False)

    @pl.when(f == last_f)
    def _():
        hidden_tile(first=False, last=True)


def _ffn(h, gpre, w_gate, w_up, w_down, gpost, *, layer, tm, tf):
    rows, d = h.shape
    ff = w_gate.shape[2]
    tc = min(tf, FFN_RESULT_COLS)
    assert ff // tf >= 2, "first and last hidden tile are separate grid steps"
    pipelined = 2 * _nbytes((tm, d), F32) + 3 * _nbytes((d, tf), BF16)
    resident = _nbytes((tm, d), BF16) + _nbytes((tm, tf), BF16) + 6 * _nbytes((tm, tc), F32)
    return pl.pallas_call(
        functools.partial(_ffn_body, tc=tc),
        grid=(rows // tm, ff // tf),
        in_specs=[
            pl.BlockSpec((tm, d), lambda i, f: (i, 0)),
            pl.BlockSpec((1, d), lambda i, f: (0, 0)),
            pl.BlockSpec((None, d, tf), lambda i, f: (layer, 0, f)),
            pl.BlockSpec((None, d, tf), lambda i, f: (layer, 0, f)),
            pl.BlockSpec((None, tf, d), lambda i, f: (layer, f, 0)),
            pl.BlockSpec((1, d), lambda i, f: (0, 0)),
        ],
        out_specs=pl.BlockSpec((tm, d), lambda i, f: (i, 0)),
        out_shape=jax.ShapeDtypeStruct((rows, d), F32),
        scratch_shapes=[pltpu.VMEM((tm, d), BF16)],
        compiler_params=pltpu.CompilerParams(
            dimension_semantics=("parallel", "arbitrary"),
            vmem_limit_bytes=_vmem_limit(pipelined, resident)),
        name="ffn",
    )(h, gpre, w_gate, w_up, w_down, gpost)


def _qkv_body(h_ref, gpre_ref, w_ref, b_ref, q_ref, k_ref, v_ref, xn_ref, *, tn, n_kv):
    d = h_ref.shape[1]
    _pre_norm(h_ref, gpre_ref, xn_ref)
    xn = xn_ref[...]
    for j in range(d // tn):
        cq = slice(j * tn, (j + 1) * tn)
        q_ref[:, cq] = ((_mm(xn, w_ref[:, cq]) + b_ref[:, cq]) * (HEAD_DIM ** -0.5)).astype(BF16)
    y = _mm(xn, w_ref[:, d:]) + b_ref[:, d:]
    ones = jnp.ones((y.shape[0], HEAD_DIM), BF16)
    for hh in range(n_kv):
        k_ref[hh] = y[:, hh * HEAD_DIM:(hh + 1) * HEAD_DIM].astype(BF16)
        vh = y[:, (n_kv + hh) * HEAD_DIM:(n_kv + hh + 1) * HEAD_DIM].astype(BF16)
        v_ref[hh] = jnp.concatenate([vh, ones, ones, vh], axis=1)


def _qkv(h, gpre, w_qkv, b_qkv, *, tm, tn, n_kv):
    rows, d = h.shape
    pipelined = (_nbytes((tm, d), F32) + _nbytes((tm, d), BF16)
                 + _nbytes((n_kv, tm, V7X_LANES), BF16) + _nbytes((n_kv, tm, V_EXT), BF16))
    resident = _nbytes(w_qkv.shape, BF16) + _nbytes((tm, d), BF16) + 4 * _nbytes((tm, tn), F32)
    return pl.pallas_call(
        functools.partial(_qkv_body, tn=tn, n_kv=n_kv),
        grid=(rows // tm,),
        in_specs=[
            pl.BlockSpec((tm, d), lambda i: (i, 0)),
            _resident((1, d)), _resident(w_qkv.shape), _resident(b_qkv.shape),
        ],
        out_specs=[
            pl.BlockSpec((tm, d), lambda i: (i, 0)),
            pl.BlockSpec((n_kv, tm, HEAD_DIM), lambda i: (0, i, 0)),
            pl.BlockSpec((n_kv, tm, V_EXT), lambda i: (0, i, 0)),
        ],
        out_shape=[jax.ShapeDtypeStruct((rows, d), BF16),
                   jax.ShapeDtypeStruct((n_kv, rows, HEAD_DIM), BF16),
                   jax.ShapeDtypeStruct((n_kv, rows, V_EXT), BF16)],
        scratch_shapes=[pltpu.VMEM((tm, d), BF16)],
        compiler_params=pltpu.CompilerParams(
            dimension_semantics=("parallel",),
            vmem_limit_bytes=_vmem_limit(pipelined, resident)),
        name="qkv",
    )(h, gpre, w_qkv, b_qkv)


def _attn_body(q_ref, kp_ref, kc_ref, vp_ref, vc_ref, km_ref, vm_ref, tbl_ref, o_ref):
    qw = GROUP * HEAD_DIM
    n_tail = SLOTS - BAND_SLOTS - N_META
    zpad = jnp.zeros((n_tail, HEAD_DIM), BF16)
    lane = lax.broadcasted_iota(jnp.int32, (n_tail, V_EXT), 1)
    vtail = jnp.where((lane >= HEAD_DIM) & (lane < V_EXT - HEAD_DIM), 1.0, 0.0).astype(BF16)
    units = [(hh, sb) for hh in range(kp_ref.shape[0]) for sb in range(N_SUB)]
    band = lambda p_ref, c_ref, hh, lo: jnp.concatenate(
        [p_ref[hh, lo:BLOCK], c_ref[hh, 0:lo + SUB]], axis=0)
    scores, vfs = [], []
    for hh, sb in units:
        lo = sb * SUB
        kf = jnp.concatenate([band(kp_ref, kc_ref, hh, lo), km_ref[hh], zpad], axis=0)
        vfs.append(jnp.concatenate([band(vp_ref, vc_ref, hh, lo), vm_ref[hh], vtail], axis=0))
        q = q_ref[lo:lo + SUB, hh * qw:(hh + 1) * qw]
        qs = jnp.concatenate([q[:, g * HEAD_DIM:(g + 1) * HEAD_DIM] for g in range(GROUP)], axis=0)
        s = lax.dot_general(qs, kf, (((1,), (1,)), ((), ())), preferred_element_type=F32)
        scores.append(s + tbl_ref[0, hh, sb])
    probs = [jnp.exp(s - jnp.max(s, axis=-1, keepdims=True)).astype(BF16) for s in scores]
    outs = []
    for p, vf in zip(probs, vfs):
        ol = _mm(p, vf)
        outs.append((ol[:, :V_EXT // 2] / ol[:, V_EXT // 2:])[:, :HEAD_DIM])
    for hh in range(kp_ref.shape[0]):
        o_ref[:, hh * qw:(hh + 1) * qw] = jnp.concatenate(
            [jnp.concatenate([o[g * SUB:(g + 1) * SUB] for o in outs[hh * N_SUB:(hh + 1) * N_SUB]],
                             axis=0) for g in range(GROUP)], axis=1).astype(o_ref.dtype)


def _attention(q, k, v, k_meta, v_meta, tbl, *, batch, seq, hps):
    rows, d = q.shape
    n_kv = k.shape[0]
    n_blk = seq // BLOCK
    qw = hps * GROUP * HEAD_DIM
    row_blk = lambda h, b, n: b * n_blk + n
    prev_blk = lambda h, b, n: b * n_blk + jnp.maximum(n - 1, 0)
    kv_spec = lambda fn, w: pl.BlockSpec((hps, BLOCK, w), lambda h, b, n: (h, fn(h, b, n), 0))
    meta_spec = lambda w: pl.BlockSpec((hps, N_META, w), lambda h, b, n: (h, 0, 0))
    pipelined = (2 * _nbytes((BLOCK, qw), BF16) + 2 * _nbytes((hps, BLOCK, V7X_LANES), BF16)
                 + 2 * _nbytes((hps, BLOCK, V_EXT), BF16)
                 + _nbytes((hps, GROUP * BLOCK, SLOTS), F32))
    resident = 6 * hps * _nbytes((GROUP * BLOCK, SLOTS), F32)
    return pl.pallas_call(
        _attn_body,
        grid=(n_kv // hps, batch, n_blk),
        in_specs=[
            pl.BlockSpec((BLOCK, qw), lambda h, b, n: (row_blk(h, b, n), h)),
            kv_spec(prev_blk, HEAD_DIM), kv_spec(row_blk, HEAD_DIM),
            kv_spec(prev_blk, V_EXT), kv_spec(row_blk, V_EXT),
            meta_spec(HEAD_DIM), meta_spec(V_EXT),
            pl.BlockSpec((1, hps, N_SUB, GROUP * SUB, SLOTS),
                         lambda h, b, n: (jnp.minimum(n, 1), h, 0, 0, 0)),
        ],
        out_specs=pl.BlockSpec((BLOCK, qw), lambda h, b, n: (row_blk(h, b, n), h)),
        out_shape=jax.ShapeDtypeStruct((rows, d), BF16),
        compiler_params=pltpu.CompilerParams(
            dimension_semantics=("parallel", "parallel", "parallel"),
            vmem_limit_bytes=_vmem_limit(pipelined, resident)),
        name="swa_attention",
    )(q, k, k, v, v, k_meta, v_meta, tbl)


def _oproj_body(a_ref, w_ref, b_ref, gpost_ref, h_ref, o_ref):
    o_ref[...] = _mm(a_ref[...], w_ref[...]) + b_ref[...]
    _post_norm_residual(o_ref, h_ref, gpost_ref)


def _oproj(a, w_o, b_o, gpost, h, *, tm):
    rows, d = h.shape
    vec = lambda: pl.BlockSpec((1, d), lambda i: (0, 0))
    pipelined = _nbytes((tm, d), BF16) + 2 * _nbytes((tm, d), F32)
    resident = _nbytes((d, d), BF16) + 3 * _nbytes((tm, d), F32)
    return pl.pallas_call(
        _oproj_body,
        grid=(rows // tm,),
        in_specs=[
            pl.BlockSpec((tm, d), lambda i: (i, 0)),
            _resident((d, d)),
            vec(), vec(),
            pl.BlockSpec((tm, d), lambda i: (i, 0)),
        ],
        out_specs=pl.BlockSpec((tm, d), lambda i: (i, 0)),
        out_shape=jax.ShapeDtypeStruct((rows, d), F32),
        compiler_params=pltpu.CompilerParams(
            dimension_semantics=("parallel",),
            vmem_limit_bytes=_vmem_limit(pipelined, resident)),
        name="attn_out_proj",
    )(a, w_o, b_o, gpost, h)


def _t5_bucket(dist):
    max_exact = N_BUCKETS // 2
    dd = jnp.maximum(dist, max_exact).astype(F32)
    large = max_exact + (jnp.log(dd / max_exact) / math.log(MAX_DISTANCE / max_exact)
                         * (N_BUCKETS - max_exact)).astype(jnp.int32)
    return jnp.where(dist < max_exact, dist, jnp.minimum(large, N_BUCKETS - 1))


def _slot_table(rel_bias, sinks, n_kv):
    n_heads = rel_bias.shape[1]
    rb = rel_bias.astype(F32)

    def lookup(bucket):
        onehot = (bucket[..., None] == jnp.arange(N_BUCKETS)).astype(F32)
        return jnp.einsum("...b,bh->h...", onehot, rb, precision=lax.Precision.HIGHEST)

    qpos = jnp.arange(BLOCK)[:, None]
    kpos = jnp.arange(2 * BLOCK)[None, :]
    mpos = jnp.arange(N_META)[None, :]
    d_band = BLOCK + qpos - kpos
    in_window = (d_band >= 0) & (d_band < WINDOW)
    bias_band = lookup(_t5_bucket(jnp.maximum(d_band, 0)))
    sink = jnp.broadcast_to(sinks.astype(F32)[:, None, None], (n_heads, SUB, 1))
    pad = jnp.full((n_heads, SUB, SLOTS - BAND_SLOTS - N_META - 1), NEG_INF, F32)
    tables = []
    for n in (0, 1):
        valid = in_window & (n * BLOCK + kpos >= BLOCK)
        band = jnp.where(valid[None], bias_band, NEG_INF)
        meta = lookup(_t5_bucket(N_META + n * BLOCK + qpos - mpos))
        subs = []
        for sb in range(N_SUB):
            lo = sb * SUB
            subs.append(jnp.concatenate([band[:, lo:lo + SUB, lo:lo + BAND_SLOTS],
                                         meta[:, lo:lo + SUB], sink, pad], axis=2))
        t = jnp.stack(subs, axis=1).reshape(n_kv, GROUP, N_SUB, SUB, SLOTS)
        tables.append(jnp.transpose(t, (0, 2, 1, 3, 4)).reshape(n_kv, N_SUB, GROUP * SUB, SLOTS))
    return jnp.stack(tables)


def kernel(x, meta_tokens, rel_bias, conv_w_in, conv_b_in, conv_w_dw, conv_b_dw, conv_ln_g, conv_ln_b,
           conv_w_out, conv_b_out, attn_w_qkv, attn_b_qkv, attn_sinks, attn_w_o, attn_b_o,
           norm_mix_pre, norm_mix_post, norm_ffn_pre, norm_ffn_post, ffn_w_gate, ffn_w_up, ffn_w_down):
    batch, seq, d = x.shape
    n_kv = (attn_w_qkv.shape[2] - d) // (2 * HEAD_DIM)
    rows = batch * seq
    row = lambda a: a.reshape(1, -1).astype(F32)
    bf = functools.partial(_to_bf16, kb=CAST_ROWS)

    tm_mm, tn_mm, tm_conv, tm_ffn, tf = 1024, 512, 512, 1024, 512
    tm_meta = N_META

    h0 = x.reshape(rows, d)
    hm0 = meta_tokens.astype(x.dtype)
    w_gate, w_up, w_down = bf(ffn_w_gate), bf(ffn_w_up), bf(ffn_w_down)
    ffn = lambda h, layer, tm: _ffn(h, row(norm_ffn_pre[layer]), w_gate, w_up, w_down,
                                    row(norm_ffn_post[layer]), layer=layer, tm=tm, tf=tf)

    w_in, b_in = bf(conv_w_in)[0], row(conv_b_in[0])
    conv_args = (conv_w_dw[0].astype(F32), row(conv_b_dw[0]), row(conv_ln_g[0]), row(conv_ln_b[0]),
                 bf(conv_w_out)[0], row(conv_b_out[0]), row(norm_mix_post[0]))
    gpre0 = row(norm_mix_pre[0])

    vm = _conv_in(hm0, gpre0, w_in, b_in, tm=tm_meta, tn=tn_mm)
    zero_halo = jnp.zeros((HALO_ROWS, d), BF16)
    hm1 = _conv_out(vm, zero_halo, zero_halo, *conv_args, hm0, tm=tm_meta, rows_per_seq=N_META)
    hm2 = ffn(hm1, 0, tm_meta)

    v = _conv_in(h0, gpre0, w_in, b_in, tm=tm_mm, tn=tn_mm)
    halo0 = jnp.concatenate([jnp.zeros((HALO_ROWS - N_META, d), BF16), vm], axis=0)
    h1 = _conv_out(v, v, halo0, *conv_args, h0, tm=tm_conv, rows_per_seq=seq)
    h2 = ffn(h1, 0, tm_ffn)

    w_qkv, b_qkv = bf(attn_w_qkv)[0], row(attn_b_qkv[0])
    gpre1 = row(norm_mix_pre[1])
    _, k_meta, v_meta = _qkv(hm2, gpre1, w_qkv, b_qkv, tm=tm_meta, tn=tn_mm, n_kv=n_kv)
    q, k, vv = _qkv(h2, gpre1, w_qkv, b_qkv, tm=tm_mm, tn=tn_mm, n_kv=n_kv)
    tbl = _slot_table(rel_bias, attn_sinks[0], n_kv)
    a = _attention(q, k, vv, k_meta, v_meta, tbl, batch=batch, seq=seq, hps=ATTN_KV_HEADS_PER_STEP)
    h3 = _oproj(a, bf(attn_w_o)[0], row(attn_b_o[0]), row(norm_mix_post[1]), h2, tm=tm_mm)
    h4 = ffn(h3, 1, tm_ffn)
    return h4.reshape(batch, seq, d)
```

```python
import functools
import math

import jax
import jax.numpy as jnp
from jax import lax
from jax.experimental import pallas as pl
from jax.experimental.pallas import tpu as pltpu

N_META = 16
CONV_WIDTH = 31
HEAD_DIM = 64
GROUP = 8
WINDOW = 128
BLOCK = 128
N_BUCKETS = 32
MAX_DISTANCE = 128
RMS_EPS = 1e-6
LN_EPS = 1e-5
NEG_INF = -1e30

V7X_LANES = 128
V7X_SUBLANES = 8
V7X_BF16_SUBLANE_ROWS = 16
V7X_VMEM_BYTES = 64 * 1024 * 1024

ATTN_KV_HEADS_PER_STEP = 4
CAST_ROWS = 256
FFN_RESULT_COLS = 256
FFN_TAIL_ROWS = 256
NORM_ROWS = 128
CONV_ROW_CHUNK = 256

V7X_VMEM_REQUEST_CAP = V7X_VMEM_BYTES - 6 * 1024 * 1024
VMEM_SPILL_FRACTION = 4
VMEM_PIPELINE_STATE_BYTES = 2 * 1024 * 1024

HALO_ROWS = 2 * V7X_BF16_SUBLANE_ROWS
SUB = BLOCK // 2
N_SUB = BLOCK // SUB
BAND_SLOTS = BLOCK + SUB
SLOTS = 2 * V7X_LANES
V_EXT = 4 * HEAD_DIM
BF16 = jnp.bfloat16
F32 = jnp.float32


def _vmem_limit(pipelined_bytes, resident_bytes):
    need = 2 * pipelined_bytes + resident_bytes
    need += need // VMEM_SPILL_FRACTION + VMEM_PIPELINE_STATE_BYTES
    return int(min(V7X_VMEM_REQUEST_CAP, need))


def _nbytes(shape, dtype):
    return math.prod(shape) * jnp.dtype(dtype).itemsize


def _rms(x, g):
    return x * lax.rsqrt(jnp.mean(x * x, axis=-1, keepdims=True) + RMS_EPS) * g


def _sigmoid(x):
    return 1.0 / (1.0 + jnp.exp(-x))


def _mm(a, b):
    return jnp.dot(a, b, preferred_element_type=F32)


def _row_groups(rows):
    step = min(rows, NORM_ROWS)
    return [slice(r, r + step) for r in range(0, rows, step)]


def _pre_norm(h_ref, g_ref, xn_ref):
    for rs in _row_groups(h_ref.shape[0]):
        xn_ref[rs, :] = _rms(h_ref[rs, :], g_ref[...]).astype(BF16)


def _post_norm_residual(o_ref, h_ref, g_ref):
    for rs in _row_groups(o_ref.shape[0]):
        o_ref[rs, :] = h_ref[rs, :] + _rms(o_ref[rs, :], g_ref[...])


def _cast_body(x_ref, o_ref):
    o_ref[...] = x_ref[...].astype(o_ref.dtype)


def _to_bf16(w, *, kb):
    n_l, k, n = w.shape
    blk = pl.BlockSpec((None, kb, n), lambda l, i: (l, i, 0))
    return pl.pallas_call(
        _cast_body,
        grid=(n_l, k // kb),
        in_specs=[blk],
        out_specs=blk,
        out_shape=jax.ShapeDtypeStruct(w.shape, BF16),
        compiler_params=pltpu.CompilerParams(
            dimension_semantics=("parallel", "parallel"),
            vmem_limit_bytes=_vmem_limit(_nbytes((kb, n), F32) + _nbytes((kb, n), BF16), 0)),
        name="cast_bf16",
    )(w)


def _resident(shape):
    return pl.BlockSpec(shape, lambda i: (0,) * len(shape), pipeline_mode=pl.Buffered(1))


def _conv_in_body(h_ref, gpre_ref, w_ref, b_ref, o_ref, xn_ref, *, tn):
    d = h_ref.shape[1]
    _pre_norm(h_ref, gpre_ref, xn_ref)
    xn = xn_ref[...]
    for j in range(d // tn):
        ca, cg = slice(j * tn, (j + 1) * tn), slice(d + j * tn, d + (j + 1) * tn)
        a = _mm(xn, w_ref[:, ca]) + b_ref[:, ca]
        g = _mm(xn, w_ref[:, cg]) + b_ref[:, cg]
        o_ref[:, ca] = (a * _sigmoid(g)).astype(o_ref.dtype)


def _conv_in(h, gpre, w_in, b_in, *, tm, tn):
    rows, d = h.shape
    pipelined = _nbytes((tm, d), F32) + _nbytes((tm, d), BF16)
    resident = _nbytes(w_in.shape, BF16) + _nbytes((tm, d), BF16) + 6 * _nbytes((tm, tn), F32)
    return pl.pallas_call(
        functools.partial(_conv_in_body, tn=tn),
        grid=(rows // tm,),
        in_specs=[
            pl.BlockSpec((tm, d), lambda i: (i, 0)),
            _resident((1, d)), _resident(w_in.shape), _resident(b_in.shape),
        ],
        out_specs=pl.BlockSpec((tm, d), lambda i: (i, 0)),
        out_shape=jax.ShapeDtypeStruct((rows, d), BF16),
        scratch_shapes=[pltpu.VMEM((tm, d), BF16)],
        compiler_params=pltpu.CompilerParams(
            dimension_semantics=("parallel",),
            vmem_limit_bytes=_vmem_limit(pipelined, resident)),
        name="conv_in",
    )(h, gpre, w_in, b_in)


def _conv_out_body(v_ref, vprev_ref, halo0_ref, wdw_ref, bdw_ref, lng_ref, lnb_ref, wout_ref,
                   bout_ref, gpost_ref, h_ref, o_ref, xbuf_ref, u_ref, y_ref, *, tm, rt, blocks_per_seq):
    n_slab, _, cw = xbuf_ref.shape
    first = pl.program_id(0) % blocks_per_seq == 0

    def fill_halo(src_ref):
        for c in range(n_slab):
            xbuf_ref[c, pl.ds(0, HALO_ROWS, stride=2), :] = src_ref[:, c * cw:(c + 1) * cw].astype(F32)

    pl.when(first)(functools.partial(fill_halo, halo0_ref))
    pl.when(jnp.logical_not(first))(functools.partial(fill_halo, vprev_ref))
    for c in range(n_slab):
        xbuf_ref[c, pl.ds(2 * HALO_ROWS, tm, stride=2), :] = v_ref[:, c * cw:(c + 1) * cw].astype(F32)

    def chunk(c, carry):
        ls = pl.ds(pl.multiple_of(c * cw, cw), cw)
        for r0 in range(0, tm, rt):
            acc = jnp.broadcast_to(bdw_ref[:, ls], (rt, cw)).reshape(rt // V7X_SUBLANES, V7X_SUBLANES, cw)
            for m in range(CONV_WIDTH):
                xs = xbuf_ref[c, pl.ds(2 * (HALO_ROWS + r0 - m), rt, stride=2), :]
                acc = acc + xs.reshape(acc.shape) * wdw_ref[c, CONV_WIDTH - 1 - m][None]
            u_ref[r0:r0 + rt, ls] = acc.reshape(rt, cw)
        return carry

    lax.fori_loop(0, n_slab, chunk, 0)

    for rs in _row_groups(tm):
        u = u_ref[rs, :]
        mu = jnp.mean(u, axis=-1, keepdims=True)
        uc = u - mu
        var = jnp.mean(uc * uc, axis=-1, keepdims=True)
        y = uc * lax.rsqrt(var + LN_EPS) * lng_ref[...] + lnb_ref[...]
        y_ref[rs, :] = (y * _sigmoid(y)).astype(BF16)
    o_ref[...] = _mm(y_ref[...], wout_ref[...]) + bout_ref[...]
    _post_norm_residual(o_ref, h_ref, gpost_ref)


def _conv_out(v, vprev_src, halo0, w_dw, b_dw, ln_g, ln_b, w_out, b_out, gpost, h, *, tm,
              rows_per_seq):
    rows, d = h.shape
    cw = V7X_LANES
    n_slab = d // cw
    body = functools.partial(_conv_out_body, tm=tm, rt=min(tm, CONV_ROW_CHUNK),
                             blocks_per_seq=rows_per_seq // tm)
    halo_blocks_per_tile = tm // HALO_ROWS
    w_slabs = jnp.broadcast_to(jnp.transpose(w_dw.reshape(CONV_WIDTH, n_slab, cw), (1, 0, 2))[:, :, None, :],
                               (n_slab, CONV_WIDTH, V7X_SUBLANES, cw))
    vec = lambda: pl.BlockSpec((1, d), lambda i: (0, 0))
    pipelined = (_nbytes((tm, d), BF16) + 2 * _nbytes((HALO_ROWS, d), BF16)
                 + _nbytes((HALO_ROWS, d), F32) + 2 * _nbytes((tm, d), F32))
    resident = (_nbytes((d, d), BF16) + 2 * _nbytes((tm + HALO_ROWS, d), F32)
                + 4 * _nbytes((tm, d), F32))
    return pl.pallas_call(
        body,
        grid=(rows // tm,),
        in_specs=[
            pl.BlockSpec((tm, d), lambda i: (i, 0)),
            pl.BlockSpec((HALO_ROWS, d), lambda i: (jnp.maximum(i * halo_blocks_per_tile - 1, 0), 0)),
            pl.BlockSpec((HALO_ROWS, d), lambda i: (0, 0)),
            _resident((n_slab, CONV_WIDTH, V7X_SUBLANES, cw)),
            vec(), vec(), vec(),
            _resident((d, d)),
            vec(), vec(),
            pl.BlockSpec((tm, d), lambda i: (i, 0)),
        ],
        out_specs=pl.BlockSpec((tm, d), lambda i: (i, 0)),
        out_shape=jax.ShapeDtypeStruct((rows, d), F32),
        scratch_shapes=[pltpu.VMEM((n_slab, 2 * (tm + HALO_ROWS), cw), F32), pltpu.VMEM((tm, d), F32),
                        pltpu.VMEM((tm, d), BF16)],
        compiler_params=pltpu.CompilerParams(
            dimension_semantics=("parallel",),
            vmem_limit_bytes=_vmem_limit(pipelined, resident)),
        name="conv_out",
    )(v, vprev_src, halo0, w_slabs, b_dw, ln_g, ln_b, w_out, b_out, gpost, h)


def _ffn_body(h_ref, gpre_ref, wg_ref, wu_ref, wd_ref, gpost_ref, o_ref, xn_ref, *, tc):
    f = pl.program_id(1)
    tf, d = wd_ref.shape

    last_f = pl.num_programs(1) - 1
    tm = o_ref.shape[0]

    def hidden_tile(first, last):
        xn = xn_ref[...]
        acts = []
        for c in range(0, tf, tc):
            g = _mm(xn, wg_ref[:, c:c + tc])
            u = _mm(xn, wu_ref[:, c:c + tc])
            acts.append((g * _sigmoid(g) * u).astype(BF16))
        a = jnp.concatenate(acts, axis=1)
        tr = min(tm, FFN_TAIL_ROWS) if last else tm
        for r in range(0, tm, tr):
            for c in range(0, d, tc):
                part = _mm(a[r:r + tr], wd_ref[:, c:c + tc])
                if first:
                    o_ref[r:r + tr, c:c + tc] = part
                else:
                    o_ref[r:r + tr, c:c + tc] += part
            if last:
                _post_norm_residual(o_ref.at[r:r + tr], h_ref.at[r:r + tr], gpost_ref)

    @pl.when(f == 0)
    def _():
        _pre_norm(h_ref, gpre_ref, xn_ref)
        hidden_tile(first=True, last=False)

    @pl.when((f > 0) & (f < last_f))
    def _():
        hidden_tile(first=False, last=False)

    @pl.when(f == last_f)
    def _():
        hidden_tile(first=False, last=True)


def _ffn(h, gpre, w_gate, w_up, w_down, gpost, *, layer, tm, tf):
    rows, d = h.shape
    ff = w_gate.shape[2]
    tc = min(tf, FFN_RESULT_COLS)
    assert ff // tf >= 2, "first and last hidden tile are separate grid steps"
    pipelined = 2 * _nbytes((tm, d), F32) + 3 * _nbytes((d, tf), BF16)
    resident = _nbytes((tm, d), BF16) + _nbytes((tm, tf), BF16) + 6 * _nbytes((tm, tc), F32)
    return pl.pallas_call(
        functools.partial(_ffn_body, tc=tc),
        grid=(rows // tm, ff // tf),
        in_specs=[
            pl.BlockSpec((tm, d), lambda i, f: (i, 0)),
            pl.BlockSpec((1, d), lambda i, f: (0, 0)),
            pl.BlockSpec((None, d, tf), lambda i, f: (layer, 0, f)),
            pl.BlockSpec((None, d, tf), lambda i, f: (layer, 0, f)),
            pl.BlockSpec((None, tf, d), lambda i, f: (layer, f, 0)),
            pl.BlockSpec((1, d), lambda i, f: (0, 0)),
        ],
        out_specs=pl.BlockSpec((tm, d), lambda i, f: (i, 0)),
        out_shape=jax.ShapeDtypeStruct((rows, d), F32),
        scratch_shapes=[pltpu.VMEM((tm, d), BF16)],
        compiler_params=pltpu.CompilerParams(
            dimension_semantics=("parallel", "arbitrary"),
            vmem_limit_bytes=_vmem_limit(pipelined, resident)),
        name="ffn",
    )(h, gpre, w_gate, w_up, w_down, gpost)


def _qkv_body(h_ref, gpre_ref, w_ref, b_ref, q_ref, k_ref, v_ref, xn_ref, *, tn, n_kv):
    d = h_ref.shape[1]
    _pre_norm(h_ref, gpre_ref, xn_ref)
    xn = xn_ref[...]
    for j in range(d // tn):
        cq = slice(j * tn, (j + 1) * tn)
        q_ref[:, cq] = ((_mm(xn, w_ref[:, cq]) + b_ref[:, cq]) * (HEAD_DIM ** -0.5)).astype(BF16)
    y = _mm(xn, w_ref[:, d:]) + b_ref[:, d:]
    ones = jnp.ones((y.shape[0], HEAD_DIM), BF16)
    for hh in range(n_kv):
        k_ref[hh] = y[:, hh * HEAD_DIM:(hh + 1) * HEAD_DIM].astype(BF16)
        vh = y[:, (n_kv + hh) * HEAD_DIM:(n_kv + hh + 1) * HEAD_DIM].astype(BF16)
        v_ref[hh] = jnp.concatenate([vh, ones, ones, vh], axis=1)


def _qkv(h, gpre, w_qkv, b_qkv, *, tm, tn, n_kv):
    rows, d = h.shape
    pipelined = (_nbytes((tm, d), F32) + _nbytes((tm, d), BF16)
                 + _nbytes((n_kv, tm, V7X_LANES), BF16) + _nbytes((n_kv, tm, V_EXT), BF16))
    resident = _nbytes(w_qkv.shape, BF16) + _nbytes((tm, d), BF16) + 4 * _nbytes((tm, tn), F32)
    return pl.pallas_call(
        functools.partial(_qkv_body, tn=tn, n_kv=n_kv),
        grid=(rows // tm,),
        in_specs=[
            pl.BlockSpec((tm, d), lambda i: (i, 0)),
            _resident((1, d)), _resident(w_qkv.shape), _resident(b_qkv.shape),
        ],
        out_specs=[
            pl.BlockSpec((tm, d), lambda i: (i, 0)),
            pl.BlockSpec((n_kv, tm, HEAD_DIM), lambda i: (0, i, 0)),
            pl.BlockSpec((n_kv, tm, V_EXT), lambda i: (0, i, 0)),
        ],
        out_shape=[jax.ShapeDtypeStruct((rows, d), BF16),
                   jax.ShapeDtypeStruct((n_kv, rows, HEAD_DIM), BF16),
                   jax.ShapeDtypeStruct((n_kv, rows, V_EXT), BF16)],
        scratch_shapes=[pltpu.VMEM((tm, d), BF16)],
        compiler_params=pltpu.CompilerParams(
            dimension_semantics=("parallel",),
            vmem_limit_bytes=_vmem_limit(pipelined, resident)),
        name="qkv",
    )(h, gpre, w_qkv, b_qkv)


def _attn_body(q_ref, kp_ref, kc_ref, vp_ref, vc_ref, km_ref, vm_ref, tbl_ref, o_ref):
    qw = GROUP * HEAD_DIM
    n_tail = SLOTS - BAND_SLOTS - N_META
    zpad = jnp.zeros((n_tail, HEAD_DIM), BF16)
    lane = lax.broadcasted_iota(jnp.int32, (n_tail, V_EXT), 1)
    vtail = jnp.where((lane >= HEAD_DIM) & (lane < V_EXT - HEAD_DIM), 1.0, 0.0).astype(BF16)
    units = [(hh, sb) for hh in range(kp_ref.shape[0]) for sb in range(N_SUB)]
    band = lambda p_ref, c_ref, hh, lo: jnp.concatenate(
        [p_ref[hh, lo:BLOCK], c_ref[hh, 0:lo + SUB]], axis=0)
    scores, vfs = [], []
    for hh, sb in units:
        lo = sb * SUB
        kf = jnp.concatenate([band(kp_ref, kc_ref, hh, lo), km_ref[hh], zpad], axis=0)
        vfs.append(jnp.concatenate([band(vp_ref, vc_ref, hh, lo), vm_ref[hh], vtail], axis=0))
        q = q_ref[lo:lo + SUB, hh * qw:(hh + 1) * qw]
        qs = jnp.concatenate([q[:, g * HEAD_DIM:(g + 1) * HEAD_DIM] for g in range(GROUP)], axis=0)
        s = lax.dot_general(qs, kf, (((1,), (1,)), ((), ())), preferred_element_type=F32)
        scores.append(s + tbl_ref[0, hh, sb])
    probs = [jnp.exp(s - jnp.max(s, axis=-1, keepdims=True)).astype(BF16) for s in scores]
    outs = []
    for p, vf in zip(probs, vfs):
        ol = _mm(p, vf)
        outs.append((ol[:, :V_EXT // 2] / ol[:, V_EXT // 2:])[:, :HEAD_DIM])
    for hh in range(kp_ref.shape[0]):
        o_ref[:, hh * qw:(hh + 1) * qw] = jnp.concatenate(
            [jnp.concatenate([o[g * SUB:(g + 1) * SUB] for o in outs[hh * N_SUB:(hh + 1) * N_SUB]],
                             axis=0) for g in range(GROUP)], axis=1).astype(o_ref.dtype)


def _attention(q, k, v, k_meta, v_meta, tbl, *, batch, seq, hps):
    rows, d = q.shape
    n_kv = k.shape[0]
    n_blk = seq // BLOCK
    qw = hps * GROUP * HEAD_DIM
    row_blk = lambda h, b, n: b * n_blk + n
    prev_blk = lambda h, b, n: b * n_blk + jnp.maximum(n - 1, 0)
    kv_spec = lambda fn, w: pl.BlockSpec((hps, BLOCK, w), lambda h, b, n: (h, fn(h, b, n), 0))
    meta_spec = lambda w: pl.BlockSpec((hps, N_META, w), lambda h, b, n: (h, 0, 0))
    pipelined = (2 * _nbytes((BLOCK, qw), BF16) + 2 * _nbytes((hps, BLOCK, V7X_LANES), BF16)
                 + 2 * _nbytes((hps, BLOCK, V_EXT), BF16)
                 + _nbytes((hps, GROUP * BLOCK, SLOTS), F32))
    resident = 6 * hps * _nbytes((GROUP * BLOCK, SLOTS), F32)
    return pl.pallas_call(
        _attn_body,
        grid=(n_kv // hps, batch, n_blk),
        in_specs=[
            pl.BlockSpec((BLOCK, qw), lambda h, b, n: (row_blk(h, b, n), h)),
            kv_spec(prev_blk, HEAD_DIM), kv_spec(row_blk, HEAD_DIM),
            kv_spec(prev_blk, V_EXT), kv_spec(row_blk, V_EXT),
            meta_spec(HEAD_DIM), meta_spec(V_EXT),
            pl.BlockSpec((1, hps, N_SUB, GROUP * SUB, SLOTS),
                         lambda h, b, n: (jnp.minimum(n, 1), h, 0, 0, 0)),
        ],
        out_specs=pl.BlockSpec((BLOCK, qw), lambda h, b, n: (row_blk(h, b, n), h)),
        out_shape=jax.ShapeDtypeStruct((rows, d), BF16),
        compiler_params=pltpu.CompilerParams(
            dimension_semantics=("parallel", "parallel", "parallel"),
            vmem_limit_bytes=_vmem_limit(pipelined, resident)),
        name="swa_attention",
    )(q, k, k, v, v, k_meta, v_meta, tbl)


def _oproj_body(a_ref, w_ref, b_ref, gpost_ref, h_ref, o_ref):
    o_ref[...] = _mm(a_ref[...], w_ref[...]) + b_ref[...]
    _post_norm_residual(o_ref, h_ref, gpost_ref)


def _oproj(a, w_o, b_o, gpost, h, *, tm):
    rows, d = h.shape
    vec = lambda: pl.BlockSpec((1, d), lambda i: (0, 0))
    pipelined = _nbytes((tm, d), BF16) + 2 * _nbytes((tm, d), F32)
    resident = _nbytes((d, d), BF16) + 3 * _nbytes((tm, d), F32)
    return pl.pallas_call(
        _oproj_body,
        grid=(rows // tm,),
        in_specs=[
            pl.BlockSpec((tm, d), lambda i: (i, 0)),
            _resident((d, d)),
            vec(), vec(),
            pl.BlockSpec((tm, d), lambda i: (i, 0)),
        ],
        out_specs=pl.BlockSpec((tm, d), lambda i: (i, 0)),
        out_shape=jax.ShapeDtypeStruct((rows, d), F32),
        compiler_params=pltpu.CompilerParams(
            dimension_semantics=("parallel",),
            vmem_limit_bytes=_vmem_limit(pipelined, resident)),
        name="attn_out_proj",
    )(a, w_o, b_o, gpost, h)


def _t5_bucket(dist):
    max_exact = N_BUCKETS // 2
    dd = jnp.maximum(dist, max_exact).astype(F32)
    large = max_exact + (jnp.log(dd / max_exact) / math.log(MAX_DISTANCE / max_exact)
                         * (N_BUCKETS - max_exact)).astype(jnp.int32)
    return jnp.where(dist < max_exact, dist, jnp.minimum(large, N_BUCKETS - 1))


def _slot_table(rel_bias, sinks, n_kv):
    n_heads = rel_bias.shape[1]
    rb = rel_bias.astype(F32)

    def lookup(bucket):
        onehot = (bucket[..., None] == jnp.arange(N_BUCKETS)).astype(F32)
        return jnp.einsum("...b,bh->h...", onehot, rb, precision=lax.Precision.HIGHEST)

    qpos = jnp.arange(BLOCK)[:, None]
    kpos = jnp.arange(2 * BLOCK)[None, :]
    mpos = jnp.arange(N_META)[None, :]
    d_band = BLOCK + qpos - kpos
    in_window = (d_band >= 0) & (d_band < WINDOW)
    bias_band = lookup(_t5_bucket(jnp.maximum(d_band, 0)))
    sink = jnp.broadcast_to(sinks.astype(F32)[:, None, None], (n_heads, SUB, 1))
    pad = jnp.full((n_heads, SUB, SLOTS - BAND_SLOTS - N_META - 1), NEG_INF, F32)
    tables = []
    for n in (0, 1):
        valid = in_window & (n * BLOCK + kpos >= BLOCK)
        band = jnp.where(valid[None], bias_band, NEG_INF)
        meta = lookup(_t5_bucket(N_META + n * BLOCK + qpos - mpos))
        subs = []
        for sb in range(N_SUB):
            lo = sb * SUB
            subs.append(jnp.concatenate([band[:, lo:lo + SUB, lo:lo + BAND_SLOTS],
                                         meta[:, lo:lo + SUB], sink, pad], axis=2))
        t = jnp.stack(subs, axis=1).reshape(n_kv, GROUP, N_SUB, SUB, SLOTS)
        tables.append(jnp.transpose(t, (0, 2, 1, 3, 4)).reshape(n_kv, N_SUB, GROUP * SUB, SLOTS))
    return jnp.stack(tables)


def kernel(x, meta_tokens, rel_bias, conv_w_in, conv_b_in, conv_w_dw, conv_b_dw, conv_ln_g, conv_ln_b,
           conv_w_out, conv_b_out, attn_w_qkv, attn_b_qkv, attn_sinks, attn_w_o, attn_b_o,
           norm_mix_pre, norm_mix_post, norm_ffn_pre, norm_ffn_post, ffn_w_gate, ffn_w_up, ffn_w_down):
    batch, seq, d = x.shape
    n_kv = (attn_w_qkv.shape[2] - d) // (2 * HEAD_DIM)
    rows = batch * seq
    row = lambda a: a.reshape(1, -1).astype(F32)
    bf = functools.partial(_to_bf16, kb=CAST_ROWS)

    tm_mm, tn_mm, tm_conv, tm_ffn, tf = 1024, 512, 512, 1024, 512
    tm_meta = N_META

    h0 = x.reshape(rows, d)
    hm0 = meta_tokens.astype(x.dtype)
    w_gate, w_up, w_down = bf(ffn_w_gate), bf(ffn_w_up), bf(ffn_w_down)
    ffn = lambda h, layer, tm: _ffn(h, row(norm_ffn_pre[layer]), w_gate, w_up, w_down,
                                    row(norm_ffn_post[layer]), layer=layer, tm=tm, tf=tf)

    w_in, b_in = bf(conv_w_in)[0], row(conv_b_in[0])
    conv_args = (conv_w_dw[0].astype(F32), row(conv_b_dw[0]), row(conv_ln_g[0]), row(conv_ln_b[0]),
                 bf(conv_w_out)[0], row(conv_b_out[0]), row(norm_mix_post[0]))
    gpre0 = row(norm_mix_pre[0])

    vm = _conv_in(hm0, gpre0, w_in, b_in, tm=tm_meta, tn=tn_mm)
    zero_halo = jnp.zeros((HALO_ROWS, d), BF16)
    hm1 = _conv_out(vm, zero_halo, zero_halo, *conv_args, hm0, tm=tm_meta, rows_per_seq=N_META)
    hm2 = ffn(hm1, 0, tm_meta)

    v = _conv_in(h0, gpre0, w_in, b_in, tm=tm_mm, tn=tn_mm)
    halo0 = jnp.concatenate([jnp.zeros((HALO_ROWS - N_META, d), BF16), vm], axis=0)
    h1 = _conv_out(v, v, halo0, *conv_args, h0, tm=tm_conv, rows_per_seq=seq)
    h2 = ffn(h1, 0, tm_ffn)

    w_qkv, b_qkv = bf(attn_w_qkv)[0], row(attn_b_qkv[0])
    gpre1 = row(norm_mix_pre[1])
    _, k_meta, v_meta = _qkv(hm2, gpre1, w_qkv, b_qkv, tm=tm_meta, tn=tn_mm, n_kv=n_kv)
    q, k, vv = _qkv(h2, gpre1, w_qkv, b_qkv, tm=tm_mm, tn=tn_mm, n_kv=n_kv)
    tbl = _slot_table(rel_bias, attn_sinks[0], n_kv)
    a = _attention(q, k, vv, k_meta, v_meta, tbl, batch=batch, seq=seq, hps=ATTN_KV_HEADS_PER_STEP)
    h3 = _oproj(a, bf(attn_w_o)[0], row(attn_b_o[0]), row(norm_mix_post[1]), h2, tm=tm_mm)
    h4 = ffn(h3, 1, tm_ffn)
    return h4.reshape(batch, seq, d)
```

```python
import functools
import math

import jax
import jax.numpy as jnp
from jax import lax
from jax.experimental import pallas as pl
from jax.experimental.pallas import tpu as pltpu

N_META = 16
CONV_WIDTH = 31
HEAD_DIM = 64
GROUP = 8
WINDOW = 128
BLOCK = 128
N_BUCKETS = 32
MAX_DISTANCE = 128
RMS_EPS = 1e-6
LN_EPS = 1e-5
NEG_INF = -1e30

V7X_LANES = 128
V7X_SUBLANES = 8
V7X_BF16_SUBLANE_ROWS = 16
V7X_VMEM_BYTES = 64 * 1024 * 1024

ATTN_KV_HEADS_PER_STEP = 4
CAST_ROWS = 256
FFN_RESULT_COLS = 256
FFN_TAIL_ROWS = 256
NORM_ROWS = 128
CONV_ROW_CHUNK = 256

V7X_VMEM_REQUEST_CAP = V7X_VMEM_BYTES - 6 * 1024 * 1024
VMEM_SPILL_FRACTION = 4
VMEM_PIPELINE_STATE_BYTES = 2 * 1024 * 1024

HALO_ROWS = 2 * V7X_BF16_SUBLANE_ROWS
SUB = BLOCK // 2
N_SUB = BLOCK // SUB
BAND_SLOTS = BLOCK + SUB
SLOTS = 2 * V7X_LANES
V_EXT = 4 * HEAD_DIM
BF16 = jnp.bfloat16
F32 = jnp.float32


def _vmem_limit(pipelined_bytes, resident_bytes):
    need = 2 * pipelined_bytes + resident_bytes
    need += need // VMEM_SPILL_FRACTION + VMEM_PIPELINE_STATE_BYTES
    return int(min(V7X_VMEM_REQUEST_CAP, need))


def _nbytes(shape, dtype):
    return math.prod(shape) * jnp.dtype(dtype).itemsize


def _rms(x, g):
    return x * lax.rsqrt(jnp.mean(x * x, axis=-1, keepdims=True) + RMS_EPS) * g


def _sigmoid(x):
    return 1.0 / (1.0 + jnp.exp(-x))


def _mm(a, b):
    return jnp.dot(a, b, preferred_element_type=F32)


def _row_groups(rows):
    step = min(rows, NORM_ROWS)
    return [slice(r, r + step) for r in range(0, rows, step)]


def _pre_norm(h_ref, g_ref, xn_ref):
    for rs in _row_groups(h_ref.shape[0]):
        xn_ref[rs, :] = _rms(h_ref[rs, :], g_ref[...]).astype(BF16)


def _post_norm_residual(o_ref, h_ref, g_ref):
    for rs in _row_groups(o_ref.shape[0]):
        o_ref[rs, :] = h_ref[rs, :] + _rms(o_ref[rs, :], g_ref[...])


def _cast_body(x_ref, o_ref):
    o_ref[...] = x_ref[...].astype(o_ref.dtype)


def _to_bf16(w, *, kb, layer=None):
    n_l, k, n = w.shape
    if layer is None:
        grid = (n_l, k // kb)
        in_spec = out_spec = pl.BlockSpec((None, kb, n), lambda l, i: (l, i, 0))
        out_shape = w.shape
    else:
        grid = (1, k // kb)
        in_spec = pl.BlockSpec((None, kb, n), lambda l, i: (layer, i, 0))
        out_spec = pl.BlockSpec((kb, n), lambda l, i: (i, 0))
        out_shape = (k, n)
    return pl.pallas_call(
        _cast_body,
        grid=grid,
        in_specs=[in_spec],
        out_specs=out_spec,
        out_shape=jax.ShapeDtypeStruct(out_shape, BF16),
        compiler_params=pltpu.CompilerParams(
            dimension_semantics=("parallel", "parallel"),
            vmem_limit_bytes=_vmem_limit(_nbytes((kb, n), F32) + _nbytes((kb, n), BF16), 0)),
        name="cast_bf16",
    )(w)


def _resident(shape):
    return pl.BlockSpec(shape, lambda i: (0,) * len(shape), pipeline_mode=pl.Buffered(1))


def _conv_in_body(h_ref, gpre_ref, w_ref, b_ref, o_ref, xn_ref, *, tn):
    d = h_ref.shape[1]
    _pre_norm(h_ref, gpre_ref, xn_ref)
    xn = xn_ref[...]
    for j in range(d // tn):
        ca, cg = slice(j * tn, (j + 1) * tn), slice(d + j * tn, d + (j + 1) * tn)
        a = _mm(xn, w_ref[:, ca]) + b_ref[:, ca]
        g = _mm(xn, w_ref[:, cg]) + b_ref[:, cg]
        o_ref[:, ca] = (a * _sigmoid(g)).astype(o_ref.dtype)


def _conv_in(h, gpre, w_in, b_in, *, tm, tn):
    rows, d = h.shape
    pipelined = _nbytes((tm, d), F32) + _nbytes((tm, d), BF16)
    resident = _nbytes(w_in.shape, BF16) + _nbytes((tm, d), BF16) + 6 * _nbytes((tm, tn), F32)
    return pl.pallas_call(
        functools.partial(_conv_in_body, tn=tn),
        grid=(rows // tm,),
        in_specs=[
            pl.BlockSpec((tm, d), lambda i: (i, 0)),
            _resident((1, d)), _resident(w_in.shape), _resident(b_in.shape),
        ],
        out_specs=pl.BlockSpec((tm, d), lambda i: (i, 0)),
        out_shape=jax.ShapeDtypeStruct((rows, d), BF16),
        scratch_shapes=[pltpu.VMEM((tm, d), BF16)],
        compiler_params=pltpu.CompilerParams(
            dimension_semantics=("parallel",),
            vmem_limit_bytes=_vmem_limit(pipelined, resident)),
        name="conv_in",
    )(h, gpre, w_in, b_in)


def _conv_out_body(v_ref, vprev_ref, halo0_ref, wdw_ref, bdw_ref, lng_ref, lnb_ref, wout_ref,
                   bout_ref, gpost_ref, h_ref, o_ref, xbuf_ref, u_ref, y_ref, *, tm, rt, blocks_per_seq):
    n_slab, _, cw = xbuf_ref.shape
    first = pl.program_id(0) % blocks_per_seq == 0

    def fill_halo(src_ref):
        for c in range(n_slab):
            xbuf_ref[c, pl.ds(0, HALO_ROWS, stride=2), :] = src_ref[:, c * cw:(c + 1) * cw].astype(F32)

    pl.when(first)(functools.partial(fill_halo, halo0_ref))
    pl.when(jnp.logical_not(first))(functools.partial(fill_halo, vprev_ref))
    for c in range(n_slab):
        xbuf_ref[c, pl.ds(2 * HALO_ROWS, tm, stride=2), :] = v_ref[:, c * cw:(c + 1) * cw].astype(F32)

    def chunk(c, carry):
        ls = pl.ds(pl.multiple_of(c * cw, cw), cw)
        for r0 in range(0, tm, rt):
            acc = jnp.broadcast_to(bdw_ref[:, ls], (rt, cw)).reshape(rt // V7X_SUBLANES, V7X_SUBLANES, cw)
            for m in range(CONV_WIDTH):
                xs = xbuf_ref[c, pl.ds(2 * (HALO_ROWS + r0 - m), rt, stride=2), :]
                acc = acc + xs.reshape(acc.shape) * wdw_ref[c, CONV_WIDTH - 1 - m][None]
            u_ref[r0:r0 + rt, ls] = acc.reshape(rt, cw)
        return carry

    lax.fori_loop(0, n_slab, chunk, 0)

    for rs in _row_groups(tm):
        u = u_ref[rs, :]
        mu = jnp.mean(u, axis=-1, keepdims=True)
        uc = u - mu
        var = jnp.mean(uc * uc, axis=-1, keepdims=True)
        y = uc * lax.rsqrt(var + LN_EPS) * lng_ref[...] + lnb_ref[...]
        y_ref[rs, :] = (y * _sigmoid(y)).astype(BF16)
    o_ref[...] = _mm(y_ref[...], wout_ref[...]) + bout_ref[...]
    _post_norm_residual(o_ref, h_ref, gpost_ref)


def _conv_out(v, vprev_src, halo0, w_dw, b_dw, ln_g, ln_b, w_out, b_out, gpost, h, *, tm,
              rows_per_seq):
    rows, d = h.shape
    cw = V7X_LANES
    n_slab = d // cw
    body = functools.partial(_conv_out_body, tm=tm, rt=min(tm, CONV_ROW_CHUNK),
                             blocks_per_seq=rows_per_seq // tm)
    halo_blocks_per_tile = tm // HALO_ROWS
    w_slabs = jnp.broadcast_to(jnp.transpose(w_dw.reshape(CONV_WIDTH, n_slab, cw), (1, 0, 2))[:, :, None, :],
                               (n_slab, CONV_WIDTH, V7X_SUBLANES, cw))
    vec = lambda: pl.BlockSpec((1, d), lambda i: (0, 0))
    pipelined = (_nbytes((tm, d), BF16) + 2 * _nbytes((HALO_ROWS, d), BF16)
                 + _nbytes((HALO_ROWS, d), F32) + 2 * _nbytes((tm, d), F32))
    resident = (_nbytes((d, d), BF16) + 2 * _nbytes((tm + HALO_ROWS, d), F32)
                + 4 * _nbytes((tm, d), F32))
    return pl.pallas_call(
        body,
        grid=(rows // tm,),
        in_specs=[
            pl.BlockSpec((tm, d), lambda i: (i, 0)),
            pl.BlockSpec((HALO_ROWS, d), lambda i: (jnp.maximum(i * halo_blocks_per_tile - 1, 0), 0)),
            pl.BlockSpec((HALO_ROWS, d), lambda i: (0, 0)),
            _resident((n_slab, CONV_WIDTH, V7X_SUBLANES, cw)),
            vec(), vec(), vec(),
            _resident((d, d)),
            vec(), vec(),
            pl.BlockSpec((tm, d), lambda i: (i, 0)),
        ],
        out_specs=pl.BlockSpec((tm, d), lambda i: (i, 0)),
        out_shape=jax.ShapeDtypeStruct((rows, d), F32),
        scratch_shapes=[pltpu.VMEM((n_slab, 2 * (tm + HALO_ROWS), cw), F32), pltpu.VMEM((tm, d), F32),
                        pltpu.VMEM((tm, d), BF16)],
        compiler_params=pltpu.CompilerParams(
            dimension_semantics=("parallel",),
            vmem_limit_bytes=_vmem_limit(pipelined, resident)),
        name="conv_out",
    )(v, vprev_src, halo0, w_slabs, b_dw, ln_g, ln_b, w_out, b_out, gpost, h)


def _ffn_body(h_ref, gpre_ref, wg_ref, wu_ref, wd_ref, gpost_ref, o_ref, xn_ref, *, tc):
    f = pl.program_id(1)
    tf, d = wd_ref.shape

    last_f = pl.num_programs(1) - 1
    tm = o_ref.shape[0]

    def hidden_tile(first, last):
        xn = xn_ref[...]
        acts = []
        for c in range(0, tf, tc):
            g = _mm(xn, wg_ref[:, c:c + tc])
            u = _mm(xn, wu_ref[:, c:c + tc])
            acts.append((g * _sigmoid(g) * u).astype(BF16))
        a = jnp.concatenate(acts, axis=1)
        tr = min(tm, FFN_TAIL_ROWS) if last else tm
        for r in range(0, tm, tr):
            for c in range(0, d, tc):
                part = _mm(a[r:r + tr], wd_ref[:, c:c + tc])
                if first:
                    o_ref[r:r + tr, c:c + tc] = part
                else:
                    o_ref[r:r + tr, c:c + tc] += part
            if last:
                _post_norm_residual(o_ref.at[r:r + tr], h_ref.at[r:r + tr], gpost_ref)

    @pl.when(f == 0)
    def _():
        _pre_norm(h_ref, gpre_ref, xn_ref)
        hidden_tile(first=True, last=False)

    @pl.when((f > 0) & (f < last_f))
    def _():
        hidden_tile(first=False, last=False)

    @pl.when(f == last_f)
    def _():
        hidden_tile(first=False, last=True)


def _ffn(h, gpre, w_gate, w_up, w_down, gpost, *, layer, tm, tf):
    rows, d = h.shape
    ff = w_gate.shape[2]
    tc = min(tf, FFN_RESULT_COLS)
    assert ff // tf >= 2, "first and last hidden tile are separate grid steps"
    pipelined = 2 * _nbytes((tm, d), F32) + 3 * _nbytes((d, tf), BF16)
    resident = _nbytes((tm, d), BF16) + _nbytes((tm, tf), BF16) + 6 * _nbytes((tm, tc), F32)
    return pl.pallas_call(
        functools.partial(_ffn_body, tc=tc),
        grid=(rows // tm, ff // tf),
        in_specs=[
            pl.BlockSpec((tm, d), lambda i, f: (i, 0)),
            pl.BlockSpec((1, d), lambda i, f: (0, 0)),
            pl.BlockSpec((None, d, tf), lambda i, f: (layer, 0, f)),
            pl.BlockSpec((None, d, tf), lambda i, f: (layer, 0, f)),
            pl.BlockSpec((None, tf, d), lambda i, f: (layer, f, 0)),
            pl.BlockSpec((1, d), lambda i, f: (0, 0)),
        ],
        out_specs=pl.BlockSpec((tm, d), lambda i, f: (i, 0)),
        out_shape=jax.ShapeDtypeStruct((rows, d), F32),
        scratch_shapes=[pltpu.VMEM((tm, d), BF16)],
        compiler_params=pltpu.CompilerParams(
            dimension_semantics=("parallel", "arbitrary"),
            vmem_limit_bytes=_vmem_limit(pipelined, resident)),
        name="ffn",
    )(h, gpre, w_gate, w_up, w_down, gpost)


def _qkv_body(h_ref, gpre_ref, w_ref, b_ref, q_ref, k_ref, v_ref, xn_ref, *, tn, n_kv):
    d = h_ref.shape[1]
    _pre_norm(h_ref, gpre_ref, xn_ref)
    xn = xn_ref[...]
    for j in range(d // tn):
        cq = slice(j * tn, (j + 1) * tn)
        q_ref[:, cq] = ((_mm(xn, w_ref[:, cq]) + b_ref[:, cq]) * (HEAD_DIM ** -0.5)).astype(BF16)
    y = _mm(xn, w_ref[:, d:]) + b_ref[:, d:]
    ones = jnp.ones((y.shape[0], HEAD_DIM), BF16)
    for hh in range(n_kv):
        k_ref[hh] = y[:, hh * HEAD_DIM:(hh + 1) * HEAD_DIM].astype(BF16)
        vh = y[:, (n_kv + hh) * HEAD_DIM:(n_kv + hh + 1) * HEAD_DIM].astype(BF16)
        v_ref[hh] = jnp.concatenate([vh, ones, ones, vh], axis=1)


def _qkv(h, gpre, w_qkv, b_qkv, *, tm, tn, n_kv):
    rows, d = h.shape
    pipelined = (_nbytes((tm, d), F32) + _nbytes((tm, d), BF16)
                 + _nbytes((n_kv, tm, V7X_LANES), BF16) + _nbytes((n_kv, tm, V_EXT), BF16))
    resident = _nbytes(w_qkv.shape, BF16) + _nbytes((tm, d), BF16) + 4 * _nbytes((tm, tn), F32)
    return pl.pallas_call(
        functools.partial(_qkv_body, tn=tn, n_kv=n_kv),
        grid=(rows // tm,),
        in_specs=[
            pl.BlockSpec((tm, d), lambda i: (i, 0)),
            _resident((1, d)), _resident(w_qkv.shape), _resident(b_qkv.shape),
        ],
        out_specs=[
            pl.BlockSpec((tm, d), lambda i: (i, 0)),
            pl.BlockSpec((n_kv, tm, HEAD_DIM), lambda i: (0, i, 0)),
            pl.BlockSpec((n_kv, tm, V_EXT), lambda i: (0, i, 0)),
        ],
        out_shape=[jax.ShapeDtypeStruct((rows, d), BF16),
                   jax.ShapeDtypeStruct((n_kv, rows, HEAD_DIM), BF16),
                   jax.ShapeDtypeStruct((n_kv, rows, V_EXT), BF16)],
        scratch_shapes=[pltpu.VMEM((tm, d), BF16)],
        compiler_params=pltpu.CompilerParams(
            dimension_semantics=("parallel",),
            vmem_limit_bytes=_vmem_limit(pipelined, resident)),
        name="qkv",
    )(h, gpre, w_qkv, b_qkv)


def _attn_body(q_ref, kp_ref, kc_ref, vp_ref, vc_ref, km_ref, vm_ref, tbl_ref, o_ref):
    qw = GROUP * HEAD_DIM
    n_tail = SLOTS - BAND_SLOTS - N_META
    zpad = jnp.zeros((n_tail, HEAD_DIM), BF16)
    lane = lax.broadcasted_iota(jnp.int32, (n_tail, V_EXT), 1)
    vtail = jnp.where((lane >= HEAD_DIM) & (lane < V_EXT - HEAD_DIM), 1.0, 0.0).astype(BF16)
    units = [(hh, sb) for hh in range(kp_ref.shape[0]) for sb in range(N_SUB)]
    band = lambda p_ref, c_ref, hh, lo: jnp.concatenate(
        [p_ref[hh, lo:BLOCK], c_ref[hh, 0:lo + SUB]], axis=0)
    scores, vfs = [], []
    for hh, sb in units:
        lo = sb * SUB
        kf = jnp.concatenate([band(kp_ref, kc_ref, hh, lo), km_ref[hh], zpad], axis=0)
        vfs.append(jnp.concatenate([band(vp_ref, vc_ref, hh, lo), vm_ref[hh], vtail], axis=0))
        q = q_ref[lo:lo + SUB, hh * qw:(hh + 1) * qw]
        qs = jnp.concatenate([q[:, g * HEAD_DIM:(g + 1) * HEAD_DIM] for g in range(GROUP)], axis=0)
        s = lax.dot_general(qs, kf, (((1,), (1,)), ((), ())), preferred_element_type=F32)
        scores.append(s + tbl_ref[0, hh, sb])
    probs = [jnp.exp(s - jnp.max(s, axis=-1, keepdims=True)).astype(BF16) for s in scores]
    outs = []
    for p, vf in zip(probs, vfs):
        ol = _mm(p, vf)
        outs.append((ol[:, :V_EXT // 2] / ol[:, V_EXT // 2:])[:, :HEAD_DIM])
    for hh in range(kp_ref.shape[0]):
        o_ref[:, hh * qw:(hh + 1) * qw] = jnp.concatenate(
            [jnp.concatenate([o[g * SUB:(g + 1) * SUB] for o in outs[hh * N_SUB:(hh + 1) * N_SUB]],
                             axis=0) for g in range(GROUP)], axis=1).astype(o_ref.dtype)


def _attention(q, k, v, k_meta, v_meta, tbl, *, batch, seq, hps):
    rows, d = q.shape
    n_kv = k.shape[0]
    n_blk = seq // BLOCK
    qw = hps * GROUP * HEAD_DIM
    row_blk = lambda h, b, n: b * n_blk + n
    prev_blk = lambda h, b, n: b * n_blk + jnp.maximum(n - 1, 0)
    kv_spec = lambda fn, w: pl.BlockSpec((hps, BLOCK, w), lambda h, b, n: (h, fn(h, b, n), 0))
    meta_spec = lambda w: pl.BlockSpec((hps, N_META, w), lambda h, b, n: (h, 0, 0))
    pipelined = (2 * _nbytes((BLOCK, qw), BF16) + 2 * _nbytes((hps, BLOCK, V7X_LANES), BF16)
                 + 2 * _nbytes((hps, BLOCK, V_EXT), BF16)
                 + _nbytes((hps, GROUP * BLOCK, SLOTS), F32))
    resident = 6 * hps * _nbytes((GROUP * BLOCK, SLOTS), F32)
    return pl.pallas_call(
        _attn_body,
        grid=(n_kv // hps, batch, n_blk),
        in_specs=[
            pl.BlockSpec((BLOCK, qw), lambda h, b, n: (row_blk(h, b, n), h)),
            kv_spec(prev_blk, HEAD_DIM), kv_spec(row_blk, HEAD_DIM),
            kv_spec(prev_blk, V_EXT), kv_spec(row_blk, V_EXT),
            meta_spec(HEAD_DIM), meta_spec(V_EXT),
            pl.BlockSpec((1, hps, N_SUB, GROUP * SUB, SLOTS),
                         lambda h, b, n: (jnp.minimum(n, 1), h, 0, 0, 0)),
        ],
        out_specs=pl.BlockSpec((BLOCK, qw), lambda h, b, n: (row_blk(h, b, n), h)),
        out_shape=jax.ShapeDtypeStruct((rows, d), BF16),
        compiler_params=pltpu.CompilerParams(
            dimension_semantics=("parallel", "parallel", "parallel"),
            vmem_limit_bytes=_vmem_limit(pipelined, resident)),
        name="swa_attention",
    )(q, k, k, v, v, k_meta, v_meta, tbl)


def _oproj_body(a_ref, w_ref, b_ref, gpost_ref, h_ref, o_ref):
    o_ref[...] = _mm(a_ref[...], w_ref[...]) + b_ref[...]
    _post_norm_residual(o_ref, h_ref, gpost_ref)


def _oproj(a, w_o, b_o, gpost, h, *, tm):
    rows, d = h.shape
    vec = lambda: pl.BlockSpec((1, d), lambda i: (0, 0))
    pipelined = _nbytes((tm, d), BF16) + 2 * _nbytes((tm, d), F32)
    resident = _nbytes((d, d), BF16) + 3 * _nbytes((tm, d), F32)
    return pl.pallas_call(
        _oproj_body,
        grid=(rows // tm,),
        in_specs=[
            pl.BlockSpec((tm, d), lambda i: (i, 0)),
            _resident((d, d)),
            vec(), vec(),
            pl.BlockSpec((tm, d), lambda i: (i, 0)),
        ],
        out_specs=pl.BlockSpec((tm, d), lambda i: (i, 0)),
        out_shape=jax.ShapeDtypeStruct((rows, d), F32),
        compiler_params=pltpu.CompilerParams(
            dimension_semantics=("parallel",),
            vmem_limit_bytes=_vmem_limit(pipelined, resident)),
        name="attn_out_proj",
    )(a, w_o, b_o, gpost, h)


def _t5_bucket(dist):
    max_exact = N_BUCKETS // 2
    dd = jnp.maximum(dist, max_exact).astype(F32)
    large = max_exact + (jnp.log(dd / max_exact) / math.log(MAX_DISTANCE / max_exact)
                         * (N_BUCKETS - max_exact)).astype(jnp.int32)
    return jnp.where(dist < max_exact, dist, jnp.minimum(large, N_BUCKETS - 1))


def _slot_table(rel_bias, sinks, n_kv):
    n_heads = rel_bias.shape[1]
    rb = rel_bias.astype(F32)

    def lookup(bucket):
        onehot = (bucket[..., None] == jnp.arange(N_BUCKETS)).astype(F32)
        return jnp.einsum("...b,bh->h...", onehot, rb, precision=lax.Precision.HIGHEST)

    qpos = jnp.arange(BLOCK)[:, None]
    kpos = jnp.arange(2 * BLOCK)[None, :]
    mpos = jnp.arange(N_META)[None, :]
    d_band = BLOCK + qpos - kpos
    in_window = (d_band >= 0) & (d_band < WINDOW)
    bias_band = lookup(_t5_bucket(jnp.maximum(d_band, 0)))
    sink = jnp.broadcast_to(sinks.astype(F32)[:, None, None], (n_heads, SUB, 1))
    pad = jnp.full((n_heads, SUB, SLOTS - BAND_SLOTS - N_META - 1), NEG_INF, F32)
    tables = []
    for n in (0, 1):
        valid = in_window & (n * BLOCK + kpos >= BLOCK)
        band = jnp.where(valid[None], bias_band, NEG_INF)
        meta = lookup(_t5_bucket(N_META + n * BLOCK + qpos - mpos))
        subs = []
        for sb in range(N_SUB):
            lo = sb * SUB
            subs.append(jnp.concatenate([band[:, lo:lo + SUB, lo:lo + BAND_SLOTS],
                                         meta[:, lo:lo + SUB], sink, pad], axis=2))
        t = jnp.stack(subs, axis=1).reshape(n_kv, GROUP, N_SUB, SUB, SLOTS)
        tables.append(jnp.transpose(t, (0, 2, 1, 3, 4)).reshape(n_kv, N_SUB, GROUP * SUB, SLOTS))
    return jnp.stack(tables)


def kernel(x, meta_tokens, rel_bias, conv_w_in, conv_b_in, conv_w_dw, conv_b_dw, conv_ln_g, conv_ln_b,
           conv_w_out, conv_b_out, attn_w_qkv, attn_b_qkv, attn_sinks, attn_w_o, attn_b_o,
           norm_mix_pre, norm_mix_post, norm_ffn_pre, norm_ffn_post, ffn_w_gate, ffn_w_up, ffn_w_down):
    batch, seq, d = x.shape
    n_kv = (attn_w_qkv.shape[2] - d) // (2 * HEAD_DIM)
    rows = batch * seq
    row = lambda a: a.reshape(1, -1).astype(F32)
    bf = functools.partial(_to_bf16, kb=CAST_ROWS)

    tm_mm, tn_mm, tm_conv, tm_ffn, tf = 1024, 512, 512, 1024, 512
    tm_meta = N_META

    h0 = x.reshape(rows, d)
    hm0 = meta_tokens.astype(x.dtype)
    w_gate, w_up, w_down = bf(ffn_w_gate), bf(ffn_w_up), bf(ffn_w_down)
    ffn = lambda h, layer, tm: _ffn(h, row(norm_ffn_pre[layer]), w_gate, w_up, w_down,
                                    row(norm_ffn_post[layer]), layer=layer, tm=tm, tf=tf)

    w_in, b_in = bf(conv_w_in, layer=0), row(conv_b_in[0])
    conv_args = (conv_w_dw[0].astype(F32), row(conv_b_dw[0]), row(conv_ln_g[0]), row(conv_ln_b[0]),
                 bf(conv_w_out, layer=0), row(conv_b_out[0]), row(norm_mix_post[0]))
    gpre0 = row(norm_mix_pre[0])

    vm = _conv_in(hm0, gpre0, w_in, b_in, tm=tm_meta, tn=tn_mm)
    zero_halo = jnp.zeros((HALO_ROWS, d), BF16)
    hm1 = _conv_out(vm, zero_halo, zero_halo, *conv_args, hm0, tm=tm_meta, rows_per_seq=N_META)
    hm2 = ffn(hm1, 0, tm_meta)

    v = _conv_in(h0, gpre0, w_in, b_in, tm=tm_mm, tn=tn_mm)
    halo0 = jnp.concatenate([jnp.zeros((HALO_ROWS - N_META, d), BF16), vm], axis=0)
    h1 = _conv_out(v, v, halo0, *conv_args, h0, tm=tm_conv, rows_per_seq=seq)
    h2 = ffn(h1, 0, tm_ffn)

    w_qkv, b_qkv = bf(attn_w_qkv, layer=0), row(attn_b_qkv[0])
    gpre1 = row(norm_mix_pre[1])
    _, k_meta, v_meta = _qkv(hm2, gpre1, w_qkv, b_qkv, tm=tm_meta, tn=tn_mm, n_kv=n_kv)
    q, k, vv = _qkv(h2, gpre1, w_qkv, b_qkv, tm=tm_mm, tn=tn_mm, n_kv=n_kv)
    tbl = _slot_table(rel_bias, attn_sinks[0], n_kv)
    a = _attention(q, k, vv, k_meta, v_meta, tbl, batch=batch, seq=seq, hps=ATTN_KV_HEADS_PER_STEP)
    h3 = _oproj(a, bf(attn_w_o, layer=0), row(attn_b_o[0]), row(norm_mix_post[1]), h2, tm=tm_mm)
    h4 = ffn(h3, 1, tm_ffn)
    return h4.reshape(batch, seq, d)
```

```python
import functools
import math

import jax
import jax.numpy as jnp
from jax import lax
from jax.experimental import pallas as pl
from jax.experimental.pallas import tpu as pltpu

N_META = 16
CONV_WIDTH = 31
HEAD_DIM = 64
GROUP = 8
WINDOW = 128
BLOCK = 128
N_BUCKETS = 32
MAX_DISTANCE = 128
RMS_EPS = 1e-6
LN_EPS = 1e-5
NEG_INF = -1e30

V7X_LANES = 128
V7X_SUBLANES = 8
V7X_BF16_SUBLANE_ROWS = 16
V7X_VMEM_BYTES = 64 * 1024 * 1024

ATTN_KV_HEADS_PER_STEP = 4
CAST_ROWS = 256
FFN_RESULT_COLS = 256
FFN_TAIL_ROWS = 256
NORM_ROWS = 128
CONV_ROW_CHUNK = 256

V7X_VMEM_REQUEST_CAP = V7X_VMEM_BYTES - 6 * 1024 * 1024
VMEM_SPILL_FRACTION = 4
VMEM_PIPELINE_STATE_BYTES = 2 * 1024 * 1024

HALO_ROWS = 2 * V7X_BF16_SUBLANE_ROWS
SUB = BLOCK // 2
N_SUB = BLOCK // SUB
BAND_SLOTS = BLOCK + SUB
SLOTS = 2 * V7X_LANES
V_EXT = 4 * HEAD_DIM
BF16 = jnp.bfloat16
F32 = jnp.float32


def _vmem_limit(pipelined_bytes, resident_bytes):
    need = 2 * pipelined_bytes + resident_bytes
    need += need // VMEM_SPILL_FRACTION + VMEM_PIPELINE_STATE_BYTES
    return int(min(V7X_VMEM_REQUEST_CAP, need))


def _nbytes(shape, dtype):
    return math.prod(shape) * jnp.dtype(dtype).itemsize


def _rms(x, g):
    return x * lax.rsqrt(jnp.mean(x * x, axis=-1, keepdims=True) + RMS_EPS) * g


def _sigmoid(x):
    return 1.0 / (1.0 + jnp.exp(-x))


def _mm(a, b):
    return jnp.dot(a, b, preferred_element_type=F32)


def _row_groups(rows):
    step = min(rows, NORM_ROWS)
    return [slice(r, r + step) for r in range(0, rows, step)]


def _pre_norm(h_ref, g_ref, xn_ref):
    for rs in _row_groups(h_ref.shape[0]):
        xn_ref[rs, :] = _rms(h_ref[rs, :], g_ref[...]).astype(BF16)


def _post_norm_residual(o_ref, h_ref, g_ref):
    for rs in _row_groups(o_ref.shape[0]):
        o_ref[rs, :] = h_ref[rs, :] + _rms(o_ref[rs, :], g_ref[...])


def _cast_body(x_ref, o_ref):
    o_ref[...] = x_ref[...].astype(o_ref.dtype)


def _to_bf16(w, *, kb):
    n_l, k, n = w.shape
    blk = pl.BlockSpec((None, kb, n), lambda l, i: (l, i, 0))
    return pl.pallas_call(
        _cast_body,
        grid=(n_l, k // kb),
        in_specs=[blk],
        out_specs=blk,
        out_shape=jax.ShapeDtypeStruct(w.shape, BF16),
        compiler_params=pltpu.CompilerParams(
            dimension_semantics=("parallel", "parallel"),
            vmem_limit_bytes=_vmem_limit(_nbytes((kb, n), F32) + _nbytes((kb, n), BF16), 0)),
        name="cast_bf16",
    )(w)


def _resident(shape):
    return pl.BlockSpec(shape, lambda i: (0,) * len(shape), pipeline_mode=pl.Buffered(1))


def _conv_in_body(h_ref, gpre_ref, w_ref, b_ref, o_ref, xn_ref, *, tn):
    d = h_ref.shape[1]
    _pre_norm(h_ref, gpre_ref, xn_ref)
    xn = xn_ref[...]
    for j in range(d // tn):
        ca, cg = slice(j * tn, (j + 1) * tn), slice(d + j * tn, d + (j + 1) * tn)
        a = _mm(xn, w_ref[:, ca]) + b_ref[:, ca]
        g = _mm(xn, w_ref[:, cg]) + b_ref[:, cg]
        o_ref[:, ca] = (a * _sigmoid(g)).astype(o_ref.dtype)


def _conv_in(h, gpre, w_in, b_in, *, tm, tn):
    rows, d = h.shape
    pipelined = _nbytes((tm, d), F32) + _nbytes((tm, d), BF16)
    resident = _nbytes(w_in.shape, BF16) + _nbytes((tm, d), BF16) + 6 * _nbytes((tm, tn), F32)
    return pl.pallas_call(
        functools.partial(_conv_in_body, tn=tn),
        grid=(rows // tm,),
        in_specs=[
            pl.BlockSpec((tm, d), lambda i: (i, 0)),
            _resident((1, d)), _resident(w_in.shape), _resident(b_in.shape),
        ],
        out_specs=pl.BlockSpec((tm, d), lambda i: (i, 0)),
        out_shape=jax.ShapeDtypeStruct((rows, d), BF16),
        scratch_shapes=[pltpu.VMEM((tm, d), BF16)],
        compiler_params=pltpu.CompilerParams(
            dimension_semantics=("parallel",),
            vmem_limit_bytes=_vmem_limit(pipelined, resident)),
        name="conv_in",
    )(h, gpre, w_in, b_in)


def _conv_out_body(v_ref, vprev_ref, halo0_ref, wdw_ref, bdw_ref, lng_ref, lnb_ref, wout_ref,
                   bout_ref, gpost_ref, h_ref, o_ref, xbuf_ref, u_ref, y_ref, *, tm, rt, blocks_per_seq):
    n_slab, _, cw = xbuf_ref.shape
    first = pl.program_id(0) % blocks_per_seq == 0

    def fill_halo(src_ref):
        for c in range(n_slab):
            xbuf_ref[c, pl.ds(0, HALO_ROWS, stride=2), :] = src_ref[:, c * cw:(c + 1) * cw].astype(F32)

    pl.when(first)(functools.partial(fill_halo, halo0_ref))
    pl.when(jnp.logical_not(first))(functools.partial(fill_halo, vprev_ref))
    for c in range(n_slab):
        xbuf_ref[c, pl.ds(2 * HALO_ROWS, tm, stride=2), :] = v_ref[:, c * cw:(c + 1) * cw].astype(F32)

    def chunk(c, carry):
        ls = pl.ds(pl.multiple_of(c * cw, cw), cw)
        for r0 in range(0, tm, rt):
            acc = jnp.broadcast_to(bdw_ref[:, ls], (rt, cw)).reshape(rt // V7X_SUBLANES, V7X_SUBLANES, cw)
            for m in range(CONV_WIDTH):
                xs = xbuf_ref[c, pl.ds(2 * (HALO_ROWS + r0 - m), rt, stride=2), :]
                acc = acc + xs.reshape(acc.shape) * wdw_ref[c, CONV_WIDTH - 1 - m][None]
            u_ref[r0:r0 + rt, ls] = acc.reshape(rt, cw)
        return carry

    lax.fori_loop(0, n_slab, chunk, 0)

    for rs in _row_groups(tm):
        u = u_ref[rs, :]
        mu = jnp.mean(u, axis=-1, keepdims=True)
        uc = u - mu
        var = jnp.mean(uc * uc, axis=-1, keepdims=True)
        y = uc * lax.rsqrt(var + LN_EPS) * lng_ref[...] + lnb_ref[...]
        y_ref[rs, :] = (y * _sigmoid(y)).astype(BF16)
    o_ref[...] = _mm(y_ref[...], wout_ref[...]) + bout_ref[...]
    _post_norm_residual(o_ref, h_ref, gpost_ref)


def _conv_out(v, vprev_src, halo0, w_dw, b_dw, ln_g, ln_b, w_out, b_out, gpost, h, *, tm,
              rows_per_seq):
    rows, d = h.shape
    cw = V7X_LANES
    n_slab = d // cw
    body = functools.partial(_conv_out_body, tm=tm, rt=min(tm, CONV_ROW_CHUNK),
                             blocks_per_seq=rows_per_seq // tm)
    halo_blocks_per_tile = tm // HALO_ROWS
    w_slabs = jnp.broadcast_to(jnp.transpose(w_dw.reshape(CONV_WIDTH, n_slab, cw), (1, 0, 2))[:, :, None, :],
                               (n_slab, CONV_WIDTH, V7X_SUBLANES, cw))
    vec = lambda: pl.BlockSpec((1, d), lambda i: (0, 0))
    pipelined = (_nbytes((tm, d), BF16) + 2 * _nbytes((HALO_ROWS, d), BF16)
                 + _nbytes((HALO_ROWS, d), F32) + 2 * _nbytes((tm, d), F32))
    resident = (_nbytes((d, d), BF16) + 2 * _nbytes((tm + HALO_ROWS, d), F32)
                + 4 * _nbytes((tm, d), F32))
    return pl.pallas_call(
        body,
        grid=(rows // tm,),
        in_specs=[
            pl.BlockSpec((tm, d), lambda i: (i, 0)),
            pl.BlockSpec((HALO_ROWS, d), lambda i: (jnp.maximum(i * halo_blocks_per_tile - 1, 0), 0)),
            pl.BlockSpec((HALO_ROWS, d), lambda i: (0, 0)),
            _resident((n_slab, CONV_WIDTH, V7X_SUBLANES, cw)),
            vec(), vec(), vec(),
            _resident((d, d)),
            vec(), vec(),
            pl.BlockSpec((tm, d), lambda i: (i, 0)),
        ],
        out_specs=pl.BlockSpec((tm, d), lambda i: (i, 0)),
        out_shape=jax.ShapeDtypeStruct((rows, d), F32),
        scratch_shapes=[pltpu.VMEM((n_slab, 2 * (tm + HALO_ROWS), cw), F32), pltpu.VMEM((tm, d), F32),
                        pltpu.VMEM((tm, d), BF16)],
        compiler_params=pltpu.CompilerParams(
            dimension_semantics=("parallel",),
            vmem_limit_bytes=_vmem_limit(pipelined, resident)),
        name="conv_out",
    )(v, vprev_src, halo0, w_slabs, b_dw, ln_g, ln_b, w_out, b_out, gpost, h)


def _ffn_body(h_ref, gpre_ref, wg_ref, wu_ref, wd_ref, gpost_ref, o_ref, xn_ref, *, tc):
    f = pl.program_id(1)
    tf, d = wd_ref.shape

    last_f = pl.num_programs(1) - 1
    tm = o_ref.shape[0]

    def hidden_tile(first, last):
        xn = xn_ref[...]
        acts = []
        for c in range(0, tf, tc):
            g = _mm(xn, wg_ref[:, c:c + tc])
            u = _mm(xn, wu_ref[:, c:c + tc])
            acts.append((g * _sigmoid(g) * u).astype(BF16))
        a = jnp.concatenate(acts, axis=1)
        tr = min(tm, FFN_TAIL_ROWS) if last else tm
        for r in range(0, tm, tr):
            for c in range(0, d, tc):
                part = _mm(a[r:r + tr], wd_ref[:, c:c + tc])
                if first:
                    o_ref[r:r + tr, c:c + tc] = part
                else:
                    o_ref[r:r + tr, c:c + tc] += part
            if last:
                _post_norm_residual(o_ref.at[r:r + tr], h_ref.at[r:r + tr], gpost_ref)

    @pl.when(f == 0)
    def _():
        _pre_norm(h_ref, gpre_ref, xn_ref)
        hidden_tile(first=True, last=False)

    @pl.when((f > 0) & (f < last_f))
    def _():
        hidden_tile(first=False, last=False)

    @pl.when(f == last_f)
    def _():
        hidden_tile(first=False, last=True)


def _ffn(h, gpre, w_gate, w_up, w_down, gpost, *, layer, tm, tf):
    rows, d = h.shape
    ff = w_gate.shape[2]
    tc = min(tf, FFN_RESULT_COLS)
    assert ff // tf >= 2, "first and last hidden tile are separate grid steps"
    pipelined = 2 * _nbytes((tm, d), F32) + 3 * _nbytes((d, tf), BF16)
    resident = _nbytes((tm, d), BF16) + _nbytes((tm, tf), BF16) + 6 * _nbytes((tm, tc), F32)
    return pl.pallas_call(
        functools.partial(_ffn_body, tc=tc),
        grid=(rows // tm, ff // tf),
        in_specs=[
            pl.BlockSpec((tm, d), lambda i, f: (i, 0)),
            pl.BlockSpec((1, d), lambda i, f: (0, 0)),
            pl.BlockSpec((None, d, tf), lambda i, f: (layer, 0, f)),
            pl.BlockSpec((None, d, tf), lambda i, f: (layer, 0, f)),
            pl.BlockSpec((None, tf, d), lambda i, f: (layer, f, 0)),
            pl.BlockSpec((1, d), lambda i, f: (0, 0)),
        ],
        out_specs=pl.BlockSpec((tm, d), lambda i, f: (i, 0)),
        out_shape=jax.ShapeDtypeStruct((rows, d), F32),
        scratch_shapes=[pltpu.VMEM((tm, d), BF16)],
        compiler_params=pltpu.CompilerParams(
            dimension_semantics=("parallel", "arbitrary"),
            vmem_limit_bytes=_vmem_limit(pipelined, resident)),
        name="ffn",
    )(h, gpre, w_gate, w_up, w_down, gpost)


def _qkv_body(h_ref, gpre_ref, w_ref, b_ref, q_ref, k_ref, v_ref, xn_ref, *, tn, n_kv):
    d = h_ref.shape[1]
    _pre_norm(h_ref, gpre_ref, xn_ref)
    xn = xn_ref[...]
    for j in range(d // tn):
        cq = slice(j * tn, (j + 1) * tn)
        q_ref[:, cq] = ((_mm(xn, w_ref[:, cq]) + b_ref[:, cq]) * (HEAD_DIM ** -0.5)).astype(BF16)
    y = _mm(xn, w_ref[:, d:]) + b_ref[:, d:]
    ones = jnp.ones((y.shape[0], HEAD_DIM), BF16)
    for hh in range(n_kv):
        k_ref[hh] = y[:, hh * HEAD_DIM:(hh + 1) * HEAD_DIM].astype(BF16)
        vh = y[:, (n_kv + hh) * HEAD_DIM:(n_kv + hh + 1) * HEAD_DIM].astype(BF16)
        v_ref[hh] = jnp.concatenate([vh, ones, ones, vh], axis=1)


def _qkv(h, gpre, w_qkv, b_qkv, *, tm, tn, n_kv):
    rows, d = h.shape
    pipelined = (_nbytes((tm, d), F32) + _nbytes((tm, d), BF16)
                 + _nbytes((n_kv, tm, V7X_LANES), BF16) + _nbytes((n_kv, tm, V_EXT), BF16))
    resident = _nbytes(w_qkv.shape, BF16) + _nbytes((tm, d), BF16) + 4 * _nbytes((tm, tn), F32)
    return pl.pallas_call(
        functools.partial(_qkv_body, tn=tn, n_kv=n_kv),
        grid=(rows // tm,),
        in_specs=[
            pl.BlockSpec((tm, d), lambda i: (i, 0)),
            _resident((1, d)), _resident(w_qkv.shape), _resident(b_qkv.shape),
        ],
        out_specs=[
            pl.BlockSpec((tm, d), lambda i: (i, 0)),
            pl.BlockSpec((n_kv, tm, HEAD_DIM), lambda i: (0, i, 0)),
            pl.BlockSpec((n_kv, tm, V_EXT), lambda i: (0, i, 0)),
        ],
        out_shape=[jax.ShapeDtypeStruct((rows, d), BF16),
                   jax.ShapeDtypeStruct((n_kv, rows, HEAD_DIM), BF16),
                   jax.ShapeDtypeStruct((n_kv, rows, V_EXT), BF16)],
        scratch_shapes=[pltpu.VMEM((tm, d), BF16)],
        compiler_params=pltpu.CompilerParams(
            dimension_semantics=("parallel",),
            vmem_limit_bytes=_vmem_limit(pipelined, resident)),
        name="qkv",
    )(h, gpre, w_qkv, b_qkv)


def _attn_body(q_ref, kp_ref, kc_ref, vp_ref, vc_ref, km_ref, vm_ref, tbl_ref, o_ref):
    qw = GROUP * HEAD_DIM
    n_tail = SLOTS - BAND_SLOTS - N_META
    zpad = jnp.zeros((n_tail, HEAD_DIM), BF16)
    lane = lax.broadcasted_iota(jnp.int32, (n_tail, V_EXT), 1)
    vtail = jnp.where((lane >= HEAD_DIM) & (lane < V_EXT - HEAD_DIM), 1.0, 0.0).astype(BF16)
    units = [(hh, sb) for hh in range(kp_ref.shape[0]) for sb in range(N_SUB)]
    band = lambda p_ref, c_ref, hh, lo: jnp.concatenate(
        [p_ref[hh, lo:BLOCK], c_ref[hh, 0:lo + SUB]], axis=0)
    scores, vfs = [], []
    for hh, sb in units:
        lo = sb * SUB
        kf = jnp.concatenate([band(kp_ref, kc_ref, hh, lo), km_ref[hh], zpad], axis=0)
        vfs.append(jnp.concatenate([band(vp_ref, vc_ref, hh, lo), vm_ref[hh], vtail], axis=0))
        q = q_ref[lo:lo + SUB, hh * qw:(hh + 1) * qw]
        qs = jnp.concatenate([q[:, g * HEAD_DIM:(g + 1) * HEAD_DIM] for g in range(GROUP)], axis=0)
        s = lax.dot_general(qs, kf, (((1,), (1,)), ((), ())), preferred_element_type=F32)
        scores.append(s + tbl_ref[0, hh, sb])
    probs = [jnp.exp(s - jnp.max(s, axis=-1, keepdims=True)).astype(BF16) for s in scores]
    outs = []
    for p, vf in zip(probs, vfs):
        ol = _mm(p, vf)
        outs.append((ol[:, :V_EXT // 2] / ol[:, V_EXT // 2:])[:, :HEAD_DIM])
    for hh in range(kp_ref.shape[0]):
        o_ref[:, hh * qw:(hh + 1) * qw] = jnp.concatenate(
            [jnp.concatenate([o[g * SUB:(g + 1) * SUB] for o in outs[hh * N_SUB:(hh + 1) * N_SUB]],
                             axis=0) for g in range(GROUP)], axis=1).astype(o_ref.dtype)


def _attention(q, k, v, k_meta, v_meta, tbl, *, batch, seq, hps):
    rows, d = q.shape
    n_kv = k.shape[0]
    n_blk = seq // BLOCK
    qw = hps * GROUP * HEAD_DIM
    row_blk = lambda h, b, n: b * n_blk + n
    prev_blk = lambda h, b, n: b * n_blk + jnp.maximum(n - 1, 0)
    kv_spec = lambda fn, w: pl.BlockSpec((hps, BLOCK, w), lambda h, b, n: (h, fn(h, b, n), 0))
    meta_spec = lambda w: pl.BlockSpec((hps, N_META, w), lambda h, b, n: (h, 0, 0))
    pipelined = (2 * _nbytes((BLOCK, qw), BF16) + 2 * _nbytes((hps, BLOCK, V7X_LANES), BF16)
                 + 2 * _nbytes((hps, BLOCK, V_EXT), BF16)
                 + _nbytes((hps, GROUP * BLOCK, SLOTS), F32))
    resident = 6 * hps * _nbytes((GROUP * BLOCK, SLOTS), F32)
    return pl.pallas_call(
        _attn_body,
        grid=(n_kv // hps, batch, n_blk),
        in_specs=[
            pl.BlockSpec((BLOCK, qw), lambda h, b, n: (row_blk(h, b, n), h)),
            kv_spec(prev_blk, HEAD_DIM), kv_spec(row_blk, HEAD_DIM),
            kv_spec(prev_blk, V_EXT), kv_spec(row_blk, V_EXT),
            meta_spec(HEAD_DIM), meta_spec(V_EXT),
            pl.BlockSpec((1, hps, N_SUB, GROUP * SUB, SLOTS),
                         lambda h, b, n: (jnp.minimum(n, 1), h, 0, 0, 0)),
        ],
        out_specs=pl.BlockSpec((BLOCK, qw), lambda h, b, n: (row_blk(h, b, n), h)),
        out_shape=jax.ShapeDtypeStruct((rows, d), BF16),
        compiler_params=pltpu.CompilerParams(
            dimension_semantics=("parallel", "parallel", "parallel"),
            vmem_limit_bytes=_vmem_limit(pipelined, resident)),
        name="swa_attention",
    )(q, k, k, v, v, k_meta, v_meta, tbl)


def _oproj_body(a_ref, w_ref, b_ref, gpost_ref, h_ref, o_ref):
    o_ref[...] = _mm(a_ref[...], w_ref[...]) + b_ref[...]
    _post_norm_residual(o_ref, h_ref, gpost_ref)


def _oproj(a, w_o, b_o, gpost, h, *, tm):
    rows, d = h.shape
    vec = lambda: pl.BlockSpec((1, d), lambda i: (0, 0))
    pipelined = _nbytes((tm, d), BF16) + 2 * _nbytes((tm, d), F32)
    resident = _nbytes((d, d), BF16) + 3 * _nbytes((tm, d), F32)
    return pl.pallas_call(
        _oproj_body,
        grid=(rows // tm,),
        in_specs=[
            pl.BlockSpec((tm, d), lambda i: (i, 0)),
            _resident((d, d)),
            vec(), vec(),
            pl.BlockSpec((tm, d), lambda i: (i, 0)),
        ],
        out_specs=pl.BlockSpec((tm, d), lambda i: (i, 0)),
        out_shape=jax.ShapeDtypeStruct((rows, d), F32),
        compiler_params=pltpu.CompilerParams(
            dimension_semantics=("parallel",),
            vmem_limit_bytes=_vmem_limit(pipelined, resident)),
        name="attn_out_proj",
    )(a, w_o, b_o, gpost, h)


def _t5_bucket(dist):
    max_exact = N_BUCKETS // 2
    dd = jnp.maximum(dist, max_exact).astype(F32)
    large = max_exact + (jnp.log(dd / max_exact) / math.log(MAX_DISTANCE / max_exact)
                         * (N_BUCKETS - max_exact)).astype(jnp.int32)
    return jnp.where(dist < max_exact, dist, jnp.minimum(large, N_BUCKETS - 1))


def _slot_table(rel_bias, sinks, n_kv):
    n_heads = rel_bias.shape[1]
    rb = rel_bias.astype(F32)

    def lookup(bucket):
        onehot = (bucket[..., None] == jnp.arange(N_BUCKETS)).astype(F32)
        return jnp.einsum("...b,bh->h...", onehot, rb, precision=lax.Precision.HIGHEST)

    qpos = jnp.arange(BLOCK)[:, None]
    kpos = jnp.arange(2 * BLOCK)[None, :]
    mpos = jnp.arange(N_META)[None, :]
    d_band = BLOCK + qpos - kpos
    in_window = (d_band >= 0) & (d_band < WINDOW)
    bias_band = lookup(_t5_bucket(jnp.maximum(d_band, 0)))
    sink = jnp.broadcast_to(sinks.astype(F32)[:, None, None], (n_heads, SUB, 1))
    pad = jnp.full((n_heads, SUB, SLOTS - BAND_SLOTS - N_META - 1), NEG_INF, F32)
    tables = []
    for n in (0, 1):
        valid = in_window & (n * BLOCK + kpos >= BLOCK)
        band = jnp.where(valid[None], bias_band, NEG_INF)
        meta = lookup(_t5_bucket(N_META + n * BLOCK + qpos - mpos))
        subs = []
        for sb in range(N_SUB):
            lo = sb * SUB
            subs.append(jnp.concatenate([band[:, lo:lo + SUB, lo:lo + BAND_SLOTS],
                                         meta[:, lo:lo + SUB], sink, pad], axis=2))
        t = jnp.stack(subs, axis=1).reshape(n_kv, GROUP, N_SUB, SUB, SLOTS)
        tables.append(jnp.transpose(t, (0, 2, 1, 3, 4)).reshape(n_kv, N_SUB, GROUP * SUB, SLOTS))
    return jnp.stack(tables)


def kernel(x, meta_tokens, rel_bias, conv_w_in, conv_b_in, conv_w_dw, conv_b_dw, conv_ln_g, conv_ln_b,
           conv_w_out, conv_b_out, attn_w_qkv, attn_b_qkv, attn_sinks, attn_w_o, attn_b_o,
           norm_mix_pre, norm_mix_post, norm_ffn_pre, norm_ffn_post, ffn_w_gate, ffn_w_up, ffn_w_down):
    batch, seq, d = x.shape
    n_kv = (attn_w_qkv.shape[2] - d) // (2 * HEAD_DIM)
    rows = batch * seq
    row = lambda a: a.reshape(1, -1).astype(F32)
    bf = functools.partial(_to_bf16, kb=CAST_ROWS)

    tm_mm, tn_mm, tm_conv, tm_ffn, tf = 1024, 256, 512, 1024, 512
    tm_meta = N_META

    h0 = x.reshape(rows, d)
    hm0 = meta_tokens.astype(x.dtype)
    w_gate, w_up, w_down = bf(ffn_w_gate), bf(ffn_w_up), bf(ffn_w_down)
    ffn = lambda h, layer, tm: _ffn(h, row(norm_ffn_pre[layer]), w_gate, w_up, w_down,
                                    row(norm_ffn_post[layer]), layer=layer, tm=tm, tf=tf)

    w_in, b_in = bf(conv_w_in)[0], row(conv_b_in[0])
    conv_args = (conv_w_dw[0].astype(F32), row(conv_b_dw[0]), row(conv_ln_g[0]), row(conv_ln_b[0]),
                 bf(conv_w_out)[0], row(conv_b_out[0]), row(norm_mix_post[0]))
    gpre0 = row(norm_mix_pre[0])

    vm = _conv_in(hm0, gpre0, w_in, b_in, tm=tm_meta, tn=tn_mm)
    zero_halo = jnp.zeros((HALO_ROWS, d), BF16)
    hm1 = _conv_out(vm, zero_halo, zero_halo, *conv_args, hm0, tm=tm_meta, rows_per_seq=N_META)
    hm2 = ffn(hm1, 0, tm_meta)

    v = _conv_in(h0, gpre0, w_in, b_in, tm=tm_mm, tn=tn_mm)
    halo0 = jnp.concatenate([jnp.zeros((HALO_ROWS - N_META, d), BF16), vm], axis=0)
    h1 = _conv_out(v, v, halo0, *conv_args, h0, tm=tm_conv, rows_per_seq=seq)
    h2 = ffn(h1, 0, tm_ffn)

    w_qkv, b_qkv = bf(attn_w_qkv)[0], row(attn_b_qkv[0])
    gpre1 = row(norm_mix_pre[1])
    _, k_meta, v_meta = _qkv(hm2, gpre1, w_qkv, b_qkv, tm=tm_meta, tn=tn_mm, n_kv=n_kv)
    q, k, vv = _qkv(h2, gpre1, w_qkv, b_qkv, tm=tm_mm, tn=tn_mm, n_kv=n_kv)
    tbl = _slot_table(rel_bias, attn_sinks[0], n_kv)
    a = _attention(q, k, vv, k_meta, v_meta, tbl, batch=batch, seq=seq, hps=ATTN_KV_HEADS_PER_STEP)
    h3 = _oproj(a, bf(attn_w_o)[0], row(attn_b_o[0]), row(norm_mix_post[1]), h2, tm=tm_mm)
    h4 = ffn(h3, 1, tm_ffn)
    return h4.reshape(batch, seq, d)
```

```python
import functools
import math

import jax
import jax.numpy as jnp
from jax import lax
from jax.experimental import pallas as pl
from jax.experimental.pallas import tpu as pltpu

N_META = 16
CONV_WIDTH = 31
HEAD_DIM = 64
GROUP = 8
WINDOW = 128
BLOCK = 128
N_BUCKETS = 32
MAX_DISTANCE = 128
RMS_EPS = 1e-6
LN_EPS = 1e-5
NEG_INF = -1e30

V7X_LANES = 128
V7X_SUBLANES = 8
V7X_BF16_SUBLANE_ROWS = 16
V7X_VMEM_BYTES = 64 * 1024 * 1024

ATTN_KV_HEADS_PER_STEP = 4
CAST_ROWS = 256
FFN_RESULT_COLS = 256
FFN_TAIL_ROWS = 256
NORM_ROWS = 128
CONV_ROW_CHUNK = 256

V7X_VMEM_REQUEST_CAP = V7X_VMEM_BYTES - 6 * 1024 * 1024
VMEM_SPILL_FRACTION = 4
VMEM_PIPELINE_STATE_BYTES = 2 * 1024 * 1024

HALO_ROWS = 2 * V7X_BF16_SUBLANE_ROWS
SUB = BLOCK // 2
N_SUB = BLOCK // SUB
BAND_SLOTS = BLOCK + SUB
SLOTS = 2 * V7X_LANES
V_EXT = 4 * HEAD_DIM
BF16 = jnp.bfloat16
F32 = jnp.float32


def _vmem_limit(pipelined_bytes, resident_bytes):
    need = 2 * pipelined_bytes + resident_bytes
    need += need // VMEM_SPILL_FRACTION + VMEM_PIPELINE_STATE_BYTES
    return int(min(V7X_VMEM_REQUEST_CAP, need))


def _nbytes(shape, dtype):
    return math.prod(shape) * jnp.dtype(dtype).itemsize


def _rms(x, g):
    return x * lax.rsqrt(jnp.mean(x * x, axis=-1, keepdims=True) + RMS_EPS) * g


def _sigmoid(x):
    return 1.0 / (1.0 + jnp.exp(-x))


def _mm(a, b):
    return jnp.dot(a, b, preferred_element_type=F32)


def _row_groups(rows):
    step = min(rows, NORM_ROWS)
    return [slice(r, r + step) for r in range(0, rows, step)]


def _pre_norm(h_ref, g_ref, xn_ref):
    for rs in _row_groups(h_ref.shape[0]):
        xn_ref[rs, :] = _rms(h_ref[rs, :], g_ref[...]).astype(BF16)


def _post_norm_residual(o_ref, h_ref, g_ref):
    for rs in _row_groups(o_ref.shape[0]):
        o_ref[rs, :] = h_ref[rs, :] + _rms(o_ref[rs, :], g_ref[...])


def _cast_body(x_ref, o_ref):
    o_ref[...] = x_ref[...].astype(o_ref.dtype)


def _to_bf16(w, *, kb):
    n_l, k, n = w.shape
    blk = pl.BlockSpec((None, kb, n), lambda l, i: (l, i, 0))
    return pl.pallas_call(
        _cast_body,
        grid=(n_l, k // kb),
        in_specs=[blk],
        out_specs=blk,
        out_shape=jax.ShapeDtypeStruct(w.shape, BF16),
        compiler_params=pltpu.CompilerParams(
            dimension_semantics=("parallel", "parallel"),
            vmem_limit_bytes=_vmem_limit(_nbytes((kb, n), F32) + _nbytes((kb, n), BF16), 0)),
        name="cast_bf16",
    )(w)


def _resident(shape):
    return pl.BlockSpec(shape, lambda i: (0,) * len(shape), pipeline_mode=pl.Buffered(1))


def _conv_in_body(h_ref, gpre_ref, w_ref, b_ref, o_ref, xn_ref, *, tn):
    d = h_ref.shape[1]
    _pre_norm(h_ref, gpre_ref, xn_ref)
    xn = xn_ref[...]
    for j in range(d // tn):
        ca, cg = slice(j * tn, (j + 1) * tn), slice(d + j * tn, d + (j + 1) * tn)
        a = _mm(xn, w_ref[:, ca]) + b_ref[:, ca]
        g = _mm(xn, w_ref[:, cg]) + b_ref[:, cg]
        o_ref[:, ca] = (a * _sigmoid(g)).astype(o_ref.dtype)


def _conv_in(h, gpre, w_in, b_in, *, tm, tn):
    rows, d = h.shape
    pipelined = _nbytes((tm, d), F32) + _nbytes((tm, d), BF16)
    resident = _nbytes(w_in.shape, BF16) + _nbytes((tm, d), BF16) + 6 * _nbytes((tm, tn), F32)
    return pl.pallas_call(
        functools.partial(_conv_in_body, tn=tn),
        grid=(rows // tm,),
        in_specs=[
            pl.BlockSpec((tm, d), lambda i: (i, 0)),
            _resident((1, d)), _resident(w_in.shape), _resident(b_in.shape),
        ],
        out_specs=pl.BlockSpec((tm, d), lambda i: (i, 0)),
        out_shape=jax.ShapeDtypeStruct((rows, d), BF16),
        scratch_shapes=[pltpu.VMEM((tm, d), BF16)],
        compiler_params=pltpu.CompilerParams(
            dimension_semantics=("parallel",),
            vmem_limit_bytes=_vmem_limit(pipelined, resident)),
        name="conv_in",
    )(h, gpre, w_in, b_in)


def _conv_out_body(v_ref, vprev_ref, halo0_ref, wdw_ref, bdw_ref, lng_ref, lnb_ref, wout_ref,
                   bout_ref, gpost_ref, h_ref, o_ref, xbuf_ref, u_ref, y_ref, *, tm, rt, blocks_per_seq):
    n_slab, _, cw = xbuf_ref.shape
    first = pl.program_id(0) % blocks_per_seq == 0

    def fill_halo(src_ref):
        for c in range(n_slab):
            xbuf_ref[c, pl.ds(0, HALO_ROWS, stride=2), :] = src_ref[:, c * cw:(c + 1) * cw].astype(F32)

    pl.when(first)(functools.partial(fill_halo, halo0_ref))
    pl.when(jnp.logical_not(first))(functools.partial(fill_halo, vprev_ref))
    for c in range(n_slab):
        xbuf_ref[c, pl.ds(2 * HALO_ROWS, tm, stride=2), :] = v_ref[:, c * cw:(c + 1) * cw].astype(F32)

    def chunk(c, carry):
        ls = pl.ds(pl.multiple_of(c * cw, cw), cw)
        for r0 in range(0, tm, rt):
            acc = jnp.broadcast_to(bdw_ref[:, ls], (rt, cw)).reshape(rt // V7X_SUBLANES, V7X_SUBLANES, cw)
            for m in range(CONV_WIDTH):
                xs = xbuf_ref[c, pl.ds(2 * (HALO_ROWS + r0 - m), rt, stride=2), :]
                acc = acc + xs.reshape(acc.shape) * wdw_ref[c, CONV_WIDTH - 1 - m][None]
            u_ref[r0:r0 + rt, ls] = acc.reshape(rt, cw)
        return carry

    lax.fori_loop(0, n_slab, chunk, 0)

    for rs in _row_groups(tm):
        u = u_ref[rs, :]
        mu = jnp.mean(u, axis=-1, keepdims=True)
        uc = u - mu
        var = jnp.mean(uc * uc, axis=-1, keepdims=True)
        y = uc * lax.rsqrt(var + LN_EPS) * lng_ref[...] + lnb_ref[...]
        y_ref[rs, :] = (y * _sigmoid(y)).astype(BF16)
    o_ref[...] = _mm(y_ref[...], wout_ref[...]) + bout_ref[...]
    _post_norm_residual(o_ref, h_ref, gpost_ref)


def _conv_out(v, vprev_src, halo0, w_dw, b_dw, ln_g, ln_b, w_out, b_out, gpost, h, *, tm,
              rows_per_seq):
    rows, d = h.shape
    cw = V7X_LANES
    n_slab = d // cw
    body = functools.partial(_conv_out_body, tm=tm, rt=min(tm, CONV_ROW_CHUNK),
                             blocks_per_seq=rows_per_seq // tm)
    halo_blocks_per_tile = tm // HALO_ROWS
    w_slabs = jnp.broadcast_to(jnp.transpose(w_dw.reshape(CONV_WIDTH, n_slab, cw), (1, 0, 2))[:, :, None, :],
                               (n_slab, CONV_WIDTH, V7X_SUBLANES, cw))
    vec = lambda: pl.BlockSpec((1, d), lambda i: (0, 0))
    pipelined = (_nbytes((tm, d), BF16) + 2 * _nbytes((HALO_ROWS, d), BF16)
                 + _nbytes((HALO_ROWS, d), F32) + 2 * _nbytes((tm, d), F32))
    resident = (_nbytes((d, d), BF16) + 2 * _nbytes((tm + HALO_ROWS, d), F32)
                + 4 * _nbytes((tm, d), F32))
    return pl.pallas_call(
        body,
        grid=(rows // tm,),
        in_specs=[
            pl.BlockSpec((tm, d), lambda i: (i, 0)),
            pl.BlockSpec((HALO_ROWS, d), lambda i: (jnp.maximum(i * halo_blocks_per_tile - 1, 0), 0)),
            pl.BlockSpec((HALO_ROWS, d), lambda i: (0, 0)),
            _resident((n_slab, CONV_WIDTH, V7X_SUBLANES, cw)),
            vec(), vec(), vec(),
            _resident((d, d)),
            vec(), vec(),
            pl.BlockSpec((tm, d), lambda i: (i, 0)),
        ],
        out_specs=pl.BlockSpec((tm, d), lambda i: (i, 0)),
        out_shape=jax.ShapeDtypeStruct((rows, d), F32),
        scratch_shapes=[pltpu.VMEM((n_slab, 2 * (tm + HALO_ROWS), cw), F32), pltpu.VMEM((tm, d), F32),
                        pltpu.VMEM((tm, d), BF16)],
        compiler_params=pltpu.CompilerParams(
            dimension_semantics=("parallel",),
            vmem_limit_bytes=_vmem_limit(pipelined, resident)),
        name="conv_out",
    )(v, vprev_src, halo0, w_slabs, b_dw, ln_g, ln_b, w_out, b_out, gpost, h)


def _ffn_body(h_ref, gpre_ref, wg_ref, wu_ref, wd_ref, gpost_ref, o_ref, xn_ref, *, tc):
    f = pl.program_id(1)
    tf, d = wd_ref.shape

    last_f = pl.num_programs(1) - 1
    tm = o_ref.shape[0]

    def hidden_tile(first, last):
        xn = xn_ref[...]
        acts = []
        for c in range(0, tf, tc):
            g = _mm(xn, wg_ref[:, c:c + tc])
            u = _mm(xn, wu_ref[:, c:c + tc])
            acts.append((g * _sigmoid(g) * u).astype(BF16))
        a = jnp.concatenate(acts, axis=1)
        tr = min(tm, FFN_TAIL_ROWS) if last else tm
        for r in range(0, tm, tr):
            for c in range(0, d, tc):
                part = _mm(a[r:r + tr], wd_ref[:, c:c + tc])
                if first:
                    o_ref[r:r + tr, c:c + tc] = part
                else:
                    o_ref[r:r + tr, c:c + tc] += part
            if last:
                _post_norm_residual(o_ref.at[r:r + tr], h_ref.at[r:r + tr], gpost_ref)

    @pl.when(f == 0)
    def _():
        _pre_norm(h_ref, gpre_ref, xn_ref)
        hidden_tile(first=True, last=False)

    @pl.when((f > 0) & (f < last_f))
    def _():
        hidden_tile(first=False, last=False)

    @pl.when(f == last_f)
    def _():
        hidden_tile(first=False, last=True)


def _ffn(h, gpre, w_gate, w_up, w_down, gpost, *, layer, tm, tf):
    rows, d = h.shape
    ff = w_gate.shape[2]
    tc = min(tf, FFN_RESULT_COLS)
    assert ff // tf >= 2, "first and last hidden tile are separate grid steps"
    pipelined = 2 * _nbytes((tm, d), F32) + 3 * _nbytes((d, tf), BF16)
    resident = _nbytes((tm, d), BF16) + _nbytes((tm, tf), BF16) + 6 * _nbytes((tm, tc), F32)
    return pl.pallas_call(
        functools.partial(_ffn_body, tc=tc),
        grid=(rows // tm, ff // tf),
        in_specs=[
            pl.BlockSpec((tm, d), lambda i, f: (i, 0)),
            pl.BlockSpec((1, d), lambda i, f: (0, 0)),
            pl.BlockSpec((None, d, tf), lambda i, f: (layer, 0, f)),
            pl.BlockSpec((None, d, tf), lambda i, f: (layer, 0, f)),
            pl.BlockSpec((None, tf, d), lambda i, f: (layer, f, 0)),
            pl.BlockSpec((1, d), lambda i, f: (0, 0)),
        ],
        out_specs=pl.BlockSpec((tm, d), lambda i, f: (i, 0)),
        out_shape=jax.ShapeDtypeStruct((rows, d), F32),
        scratch_shapes=[pltpu.VMEM((tm, d), BF16)],
        compiler_params=pltpu.CompilerParams(
            dimension_semantics=("parallel", "arbitrary"),
            vmem_limit_bytes=_vmem_limit(pipelined, resident)),
        name="ffn",
    )(h, gpre, w_gate, w_up, w_down, gpost)


def _qkv_body(h_ref, gpre_ref, w_ref, b_ref, q_ref, k_ref, v_ref, xn_ref, *, tn, n_kv):
    d = h_ref.shape[1]
    _pre_norm(h_ref, gpre_ref, xn_ref)
    xn = xn_ref[...]
    y = _mm(xn, w_ref[:, d:]) + b_ref[:, d:]
    ones = jnp.ones((y.shape[0], HEAD_DIM), BF16)
    for hh in range(n_kv):
        k_ref[hh] = y[:, hh * HEAD_DIM:(hh + 1) * HEAD_DIM].astype(BF16)
        vh = y[:, (n_kv + hh) * HEAD_DIM:(n_kv + hh + 1) * HEAD_DIM].astype(BF16)
        v_ref[hh] = jnp.concatenate([vh, ones, ones, vh], axis=1)
    for j in range(d // tn):
        cq = slice(j * tn, (j + 1) * tn)
        q_ref[:, cq] = ((_mm(xn, w_ref[:, cq]) + b_ref[:, cq]) * (HEAD_DIM ** -0.5)).astype(BF16)


def _qkv(h, gpre, w_qkv, b_qkv, *, tm, tn, n_kv):
    rows, d = h.shape
    pipelined = (_nbytes((tm, d), F32) + _nbytes((tm, d), BF16)
                 + _nbytes((n_kv, tm, V7X_LANES), BF16) + _nbytes((n_kv, tm, V_EXT), BF16))
    resident = _nbytes(w_qkv.shape, BF16) + _nbytes((tm, d), BF16) + 4 * _nbytes((tm, tn), F32)
    return pl.pallas_call(
        functools.partial(_qkv_body, tn=tn, n_kv=n_kv),
        grid=(rows // tm,),
        in_specs=[
            pl.BlockSpec((tm, d), lambda i: (i, 0)),
            _resident((1, d)), _resident(w_qkv.shape), _resident(b_qkv.shape),
        ],
        out_specs=[
            pl.BlockSpec((tm, d), lambda i: (i, 0)),
            pl.BlockSpec((n_kv, tm, HEAD_DIM), lambda i: (0, i, 0)),
            pl.BlockSpec((n_kv, tm, V_EXT), lambda i: (0, i, 0)),
        ],
        out_shape=[jax.ShapeDtypeStruct((rows, d), BF16),
                   jax.ShapeDtypeStruct((n_kv, rows, HEAD_DIM), BF16),
                   jax.ShapeDtypeStruct((n_kv, rows, V_EXT), BF16)],
        scratch_shapes=[pltpu.VMEM((tm, d), BF16)],
        compiler_params=pltpu.CompilerParams(
            dimension_semantics=("parallel",),
            vmem_limit_bytes=_vmem_limit(pipelined, resident)),
        name="qkv",
    )(h, gpre, w_qkv, b_qkv)


def _attn_body(q_ref, kp_ref, kc_ref, vp_ref, vc_ref, km_ref, vm_ref, tbl_ref, o_ref):
    qw = GROUP * HEAD_DIM
    n_tail = SLOTS - BAND_SLOTS - N_META
    zpad = jnp.zeros((n_tail, HEAD_DIM), BF16)
    lane = lax.broadcasted_iota(jnp.int32, (n_tail, V_EXT), 1)
    vtail = jnp.where((lane >= HEAD_DIM) & (lane < V_EXT - HEAD_DIM), 1.0, 0.0).astype(BF16)
    units = [(hh, sb) for hh in range(kp_ref.shape[0]) for sb in range(N_SUB)]
    band = lambda p_ref, c_ref, hh, lo: jnp.concatenate(
        [p_ref[hh, lo:BLOCK], c_ref[hh, 0:lo + SUB]], axis=0)
    scores, vfs = [], []
    for hh, sb in units:
        lo = sb * SUB
        kf = jnp.concatenate([band(kp_ref, kc_ref, hh, lo), km_ref[hh], zpad], axis=0)
        vfs.append(jnp.concatenate([band(vp_ref, vc_ref, hh, lo), vm_ref[hh], vtail], axis=0))
        q = q_ref[lo:lo + SUB, hh * qw:(hh + 1) * qw]
        qs = jnp.concatenate([q[:, g * HEAD_DIM:(g + 1) * HEAD_DIM] for g in range(GROUP)], axis=0)
        s = lax.dot_general(qs, kf, (((1,), (1,)), ((), ())), preferred_element_type=F32)
        scores.append(s + tbl_ref[0, hh, sb])
    probs = [jnp.exp(s - jnp.max(s, axis=-1, keepdims=True)).astype(BF16) for s in scores]
    outs = []
    for p, vf in zip(probs, vfs):
        ol = _mm(p, vf)
        outs.append((ol[:, :V_EXT // 2] / ol[:, V_EXT // 2:])[:, :HEAD_DIM])
    for hh in range(kp_ref.shape[0]):
        o_ref[:, hh * qw:(hh + 1) * qw] = jnp.concatenate(
            [jnp.concatenate([o[g * SUB:(g + 1) * SUB] for o in outs[hh * N_SUB:(hh + 1) * N_SUB]],
                             axis=0) for g in range(GROUP)], axis=1).astype(o_ref.dtype)


def _attention(q, k, v, k_meta, v_meta, tbl, *, batch, seq, hps):
    rows, d = q.shape
    n_kv = k.shape[0]
    n_blk = seq // BLOCK
    qw = hps * GROUP * HEAD_DIM
    row_blk = lambda h, b, n: b * n_blk + n
    prev_blk = lambda h, b, n: b * n_blk + jnp.maximum(n - 1, 0)
    kv_spec = lambda fn, w: pl.BlockSpec((hps, BLOCK, w), lambda h, b, n: (h, fn(h, b, n), 0))
    meta_spec = lambda w: pl.BlockSpec((hps, N_META, w), lambda h, b, n: (h, 0, 0))
    pipelined = (2 * _nbytes((BLOCK, qw), BF16) + 2 * _nbytes((hps, BLOCK, V7X_LANES), BF16)
                 + 2 * _nbytes((hps, BLOCK, V_EXT), BF16)
                 + _nbytes((hps, GROUP * BLOCK, SLOTS), F32))
    resident = 6 * hps * _nbytes((GROUP * BLOCK, SLOTS), F32)
    return pl.pallas_call(
        _attn_body,
        grid=(n_kv // hps, batch, n_blk),
        in_specs=[
            pl.BlockSpec((BLOCK, qw), lambda h, b, n: (row_blk(h, b, n), h)),
            kv_spec(prev_blk, HEAD_DIM), kv_spec(row_blk, HEAD_DIM),
            kv_spec(prev_blk, V_EXT), kv_spec(row_blk, V_EXT),
            meta_spec(HEAD_DIM), meta_spec(V_EXT),
            pl.BlockSpec((1, hps, N_SUB, GROUP * SUB, SLOTS),
                         lambda h, b, n: (jnp.minimum(n, 1), h, 0, 0, 0)),
        ],
        out_specs=pl.BlockSpec((BLOCK, qw), lambda h, b, n: (row_blk(h, b, n), h)),
        out_shape=jax.ShapeDtypeStruct((rows, d), BF16),
        compiler_params=pltpu.CompilerParams(
            dimension_semantics=("parallel", "parallel", "parallel"),
            vmem_limit_bytes=_vmem_limit(pipelined, resident)),
        name="swa_attention",
    )(q, k, k, v, v, k_meta, v_meta, tbl)


def _oproj_body(a_ref, w_ref, b_ref, gpost_ref, h_ref, o_ref):
    o_ref[...] = _mm(a_ref[...], w_ref[...]) + b_ref[...]
    _post_norm_residual(o_ref, h_ref, gpost_ref)


def _oproj(a, w_o, b_o, gpost, h, *, tm):
    rows, d = h.shape
    vec = lambda: pl.BlockSpec((1, d), lambda i: (0, 0))
    pipelined = _nbytes((tm, d), BF16) + 2 * _nbytes((tm, d), F32)
    resident = _nbytes((d, d), BF16) + 3 * _nbytes((tm, d), F32)
    return pl.pallas_call(
        _oproj_body,
        grid=(rows // tm,),
        in_specs=[
            pl.BlockSpec((tm, d), lambda i: (i, 0)),
            _resident((d, d)),
            vec(), vec(),
            pl.BlockSpec((tm, d), lambda i: (i, 0)),
        ],
        out_specs=pl.BlockSpec((tm, d), lambda i: (i, 0)),
        out_shape=jax.ShapeDtypeStruct((rows, d), F32),
        compiler_params=pltpu.CompilerParams(
            dimension_semantics=("parallel",),
            vmem_limit_bytes=_vmem_limit(pipelined, resident)),
        name="attn_out_proj",
    )(a, w_o, b_o, gpost, h)


def _t5_bucket(dist):
    max_exact = N_BUCKETS // 2
    dd = jnp.maximum(dist, max_exact).astype(F32)
    large = max_exact + (jnp.log(dd / max_exact) / math.log(MAX_DISTANCE / max_exact)
                         * (N_BUCKETS - max_exact)).astype(jnp.int32)
    return jnp.where(dist < max_exact, dist, jnp.minimum(large, N_BUCKETS - 1))


def _slot_table(rel_bias, sinks, n_kv):
    n_heads = rel_bias.shape[1]
    rb = rel_bias.astype(F32)

    def lookup(bucket):
        onehot = (bucket[..., None] == jnp.arange(N_BUCKETS)).astype(F32)
        return jnp.einsum("...b,bh->h...", onehot, rb, precision=lax.Precision.HIGHEST)

    qpos = jnp.arange(BLOCK)[:, None]
    kpos = jnp.arange(2 * BLOCK)[None, :]
    mpos = jnp.arange(N_META)[None, :]
    d_band = BLOCK + qpos - kpos
    in_window = (d_band >= 0) & (d_band < WINDOW)
    bias_band = lookup(_t5_bucket(jnp.maximum(d_band, 0)))
    sink = jnp.broadcast_to(sinks.astype(F32)[:, None, None], (n_heads, SUB, 1))
    pad = jnp.full((n_heads, SUB, SLOTS - BAND_SLOTS - N_META - 1), NEG_INF, F32)
    tables = []
    for n in (0, 1):
        valid = in_window & (n * BLOCK + kpos >= BLOCK)
        band = jnp.where(valid[None], bias_band, NEG_INF)
        meta = lookup(_t5_bucket(N_META + n * BLOCK + qpos - mpos))
        subs = []
        for sb in range(N_SUB):
            lo = sb * SUB
            subs.append(jnp.concatenate([band[:, lo:lo + SUB, lo:lo + BAND_SLOTS],
                                         meta[:, lo:lo + SUB], sink, pad], axis=2))
        t = jnp.stack(subs, axis=1).reshape(n_kv, GROUP, N_SUB, SUB, SLOTS)
        tables.append(jnp.transpose(t, (0, 2, 1, 3, 4)).reshape(n_kv, N_SUB, GROUP * SUB, SLOTS))
    return jnp.stack(tables)


def kernel(x, meta_tokens, rel_bias, conv_w_in, conv_b_in, conv_w_dw, conv_b_dw, conv_ln_g, conv_ln_b,
           conv_w_out, conv_b_out, attn_w_qkv, attn_b_qkv, attn_sinks, attn_w_o, attn_b_o,
           norm_mix_pre, norm_mix_post, norm_ffn_pre, norm_ffn_post, ffn_w_gate, ffn_w_up, ffn_w_down):
    batch, seq, d = x.shape
    n_kv = (attn_w_qkv.shape[2] - d) // (2 * HEAD_DIM)
    rows = batch * seq
    row = lambda a: a.reshape(1, -1).astype(F32)
    bf = functools.partial(_to_bf16, kb=CAST_ROWS)

    tm_mm, tn_mm, tm_conv, tm_ffn, tf = 1024, 256, 512, 1024, 512
    tm_meta = N_META

    h0 = x.reshape(rows, d)
    hm0 = meta_tokens.astype(x.dtype)
    w_gate, w_up, w_down = bf(ffn_w_gate), bf(ffn_w_up), bf(ffn_w_down)
    ffn = lambda h, layer, tm: _ffn(h, row(norm_ffn_pre[layer]), w_gate, w_up, w_down,
                                    row(norm_ffn_post[layer]), layer=layer, tm=tm, tf=tf)

    w_in, b_in = bf(conv_w_in)[0], row(conv_b_in[0])
    conv_args = (conv_w_dw[0].astype(F32), row(conv_b_dw[0]), row(conv_ln_g[0]), row(conv_ln_b[0]),
                 bf(conv_w_out)[0], row(conv_b_out[0]), row(norm_mix_post[0]))
    gpre0 = row(norm_mix_pre[0])

    vm = _conv_in(hm0, gpre0, w_in, b_in, tm=tm_meta, tn=tn_mm)
    zero_halo = jnp.zeros((HALO_ROWS, d), BF16)
    hm1 = _conv_out(vm, zero_halo, zero_halo, *conv_args, hm0, tm=tm_meta, rows_per_seq=N_META)
    hm2 = ffn(hm1, 0, tm_meta)

    v = _conv_in(h0, gpre0, w_in, b_in, tm=tm_mm, tn=tn_mm)
    halo0 = jnp.concatenate([jnp.zeros((HALO_ROWS - N_META, d), BF16), vm], axis=0)
    h1 = _conv_out(v, v, halo0, *conv_args, h0, tm=tm_conv, rows_per_seq=seq)
    h2 = ffn(h1, 0, tm_ffn)

    w_qkv, b_qkv = bf(attn_w_qkv)[0], row(attn_b_qkv[0])
    gpre1 = row(norm_mix_pre[1])
    _, k_meta, v_meta = _qkv(hm2, gpre1, w_qkv, b_qkv, tm=tm_meta, tn=tn_mm, n_kv=n_kv)
    q, k, vv = _qkv(h2, gpre1, w_qkv, b_qkv, tm=tm_mm, tn=tn_mm, n_kv=n_kv)
    tbl = _slot_table(rel_bias, attn_sinks[0], n_kv)
    a = _attention(q, k, vv, k_meta, v_meta, tbl, batch=batch, seq=seq, hps=ATTN_KV_HEADS_PER_STEP)
    h3 = _oproj(a, bf(attn_w_o)[0], row(attn_b_o[0]), row(norm_mix_post[1]), h2, tm=tm_mm)
    h4 = ffn(h3, 1, tm_ffn)
    return h4.reshape(batch, seq, d)
```

```python
import functools
import math

import jax
import jax.numpy as jnp
from jax import lax
from jax.experimental import pallas as pl
from jax.experimental.pallas import tpu as pltpu

N_META = 16
CONV_WIDTH = 31
HEAD_DIM = 64
GROUP = 8
WINDOW = 128
BLOCK = 128
N_BUCKETS = 32
MAX_DISTANCE = 128
RMS_EPS = 1e-6
LN_EPS = 1e-5
NEG_INF = -1e30

V7X_LANES = 128
V7X_SUBLANES = 8
V7X_BF16_SUBLANE_ROWS = 16
V7X_VMEM_BYTES = 64 * 1024 * 1024

ATTN_KV_HEADS_PER_STEP = 4
CAST_ROWS = 256
FFN_RESULT_COLS = 256
FFN_TAIL_ROWS = 256
NORM_ROWS = 128
CONV_ROW_CHUNK = 256

V7X_VMEM_REQUEST_CAP = V7X_VMEM_BYTES - 6 * 1024 * 1024
VMEM_SPILL_FRACTION = 4
VMEM_PIPELINE_STATE_BYTES = 2 * 1024 * 1024

HALO_ROWS = 2 * V7X_BF16_SUBLANE_ROWS
SUB = BLOCK // 2
N_SUB = BLOCK // SUB
BAND_SLOTS = BLOCK + SUB
SLOTS = 2 * V7X_LANES
V_EXT = 4 * HEAD_DIM
BF16 = jnp.bfloat16
F32 = jnp.float32


def _vmem_limit(pipelined_bytes, resident_bytes):
    need = 2 * pipelined_bytes + resident_bytes
    need += need // VMEM_SPILL_FRACTION + VMEM_PIPELINE_STATE_BYTES
    return int(min(V7X_VMEM_REQUEST_CAP, need))


def _nbytes(shape, dtype):
    return math.prod(shape) * jnp.dtype(dtype).itemsize


def _rms(x, g):
    return x * lax.rsqrt(jnp.mean(x * x, axis=-1, keepdims=True) + RMS_EPS) * g


def _sigmoid(x):
    return 1.0 / (1.0 + jnp.exp(-x))


def _mm(a, b):
    return jnp.dot(a, b, preferred_element_type=F32)


def _row_groups(rows):
    step = min(rows, NORM_ROWS)
    return [slice(r, r + step) for r in range(0, rows, step)]


def _pre_norm(h_ref, g_ref, xn_ref):
    for rs in _row_groups(h_ref.shape[0]):
        xn_ref[rs, :] = _rms(h_ref[rs, :], g_ref[...]).astype(BF16)


def _post_norm_residual(o_ref, h_ref, g_ref):
    for rs in _row_groups(o_ref.shape[0]):
        o_ref[rs, :] = h_ref[rs, :] + _rms(o_ref[rs, :], g_ref[...])


def _cast_body(x_ref, o_ref):
    o_ref[...] = x_ref[...].astype(o_ref.dtype)


def _to_bf16(w, *, kb):
    n_l, k, n = w.shape
    blk = pl.BlockSpec((None, kb, n), lambda l, i: (l, i, 0))
    return pl.pallas_call(
        _cast_body,
        grid=(n_l, k // kb),
        in_specs=[blk],
        out_specs=blk,
        out_shape=jax.ShapeDtypeStruct(w.shape, BF16),
        compiler_params=pltpu.CompilerParams(
            dimension_semantics=("parallel", "parallel"),
            vmem_limit_bytes=_vmem_limit(_nbytes((kb, n), F32) + _nbytes((kb, n), BF16), 0)),
        name="cast_bf16",
    )(w)


def _resident(shape):
    return pl.BlockSpec(shape, lambda i: (0,) * len(shape), pipeline_mode=pl.Buffered(1))


def _conv_in_body(h_ref, gpre_ref, w_ref, b_ref, o_ref, xn_ref, *, tn):
    d = h_ref.shape[1]
    _pre_norm(h_ref, gpre_ref, xn_ref)
    xn = xn_ref[...]
    for j in range(d // tn):
        ca, cg = slice(j * tn, (j + 1) * tn), slice(d + j * tn, d + (j + 1) * tn)
        a = _mm(xn, w_ref[:, ca]) + b_ref[:, ca]
        g = _mm(xn, w_ref[:, cg]) + b_ref[:, cg]
        o_ref[:, ca] = (a * _sigmoid(g)).astype(o_ref.dtype)


def _conv_in(h, gpre, w_in, b_in, *, tm, tn):
    rows, d = h.shape
    pipelined = _nbytes((tm, d), F32) + _nbytes((tm, d), BF16)
    resident = _nbytes(w_in.shape, BF16) + _nbytes((tm, d), BF16) + 6 * _nbytes((tm, tn), F32)
    return pl.pallas_call(
        functools.partial(_conv_in_body, tn=tn),
        grid=(rows // tm,),
        in_specs=[
            pl.BlockSpec((tm, d), lambda i: (i, 0)),
            _resident((1, d)), _resident(w_in.shape), _resident(b_in.shape),
        ],
        out_specs=pl.BlockSpec((tm, d), lambda i: (i, 0)),
        out_shape=jax.ShapeDtypeStruct((rows, d), BF16),
        scratch_shapes=[pltpu.VMEM((tm, d), BF16)],
        compiler_params=pltpu.CompilerParams(
            dimension_semantics=("parallel",),
            vmem_limit_bytes=_vmem_limit(pipelined, resident)),
        name="conv_in",
    )(h, gpre, w_in, b_in)


def _conv_out_body(v_ref, vprev_ref, halo0_ref, wdw_ref, bdw_ref, lng_ref, lnb_ref, wout_ref,
                   bout_ref, gpost_ref, h_ref, o_ref, xbuf_ref, u_ref, y_ref, *, tm, rt, blocks_per_seq):
    n_slab, _, cw = xbuf_ref.shape
    first = pl.program_id(0) % blocks_per_seq == 0

    def fill_halo(src_ref):
        for c in range(n_slab):
            xbuf_ref[c, pl.ds(0, HALO_ROWS, stride=2), :] = src_ref[:, c * cw:(c + 1) * cw].astype(F32)

    pl.when(first)(functools.partial(fill_halo, halo0_ref))
    pl.when(jnp.logical_not(first))(functools.partial(fill_halo, vprev_ref))
    for c in range(n_slab):
        xbuf_ref[c, pl.ds(2 * HALO_ROWS, tm, stride=2), :] = v_ref[:, c * cw:(c + 1) * cw].astype(F32)

    def chunk(c, carry):
        ls = pl.ds(pl.multiple_of(c * cw, cw), cw)
        starts = list(range(0, tm, rt))
        bias = jnp.broadcast_to(bdw_ref[:, ls], (rt, cw)).reshape(rt // V7X_SUBLANES, V7X_SUBLANES, cw)
        accs = [bias for _ in starts]
        for m in range(CONV_WIDTH):
            w = wdw_ref[c, CONV_WIDTH - 1 - m][None]
            for k, r0 in enumerate(starts):
                xs = xbuf_ref[c, pl.ds(2 * (HALO_ROWS + r0 - m), rt, stride=2), :]
                accs[k] = accs[k] + xs.reshape(bias.shape) * w
        for k, r0 in enumerate(starts):
            u_ref[r0:r0 + rt, ls] = accs[k].reshape(rt, cw)
        return carry

    lax.fori_loop(0, n_slab, chunk, 0)

    for rs in _row_groups(tm):
        u = u_ref[rs, :]
        mu = jnp.mean(u, axis=-1, keepdims=True)
        uc = u - mu
        var = jnp.mean(uc * uc, axis=-1, keepdims=True)
        y = uc * lax.rsqrt(var + LN_EPS) * lng_ref[...] + lnb_ref[...]
        y_ref[rs, :] = (y * _sigmoid(y)).astype(BF16)
    o_ref[...] = _mm(y_ref[...], wout_ref[...]) + bout_ref[...]
    _post_norm_residual(o_ref, h_ref, gpost_ref)


def _conv_out(v, vprev_src, halo0, w_dw, b_dw, ln_g, ln_b, w_out, b_out, gpost, h, *, tm,
              rows_per_seq):
    rows, d = h.shape
    cw = V7X_LANES
    n_slab = d // cw
    body = functools.partial(_conv_out_body, tm=tm, rt=min(tm, CONV_ROW_CHUNK),
                             blocks_per_seq=rows_per_seq // tm)
    halo_blocks_per_tile = tm // HALO_ROWS
    w_slabs = jnp.broadcast_to(jnp.transpose(w_dw.reshape(CONV_WIDTH, n_slab, cw), (1, 0, 2))[:, :, None, :],
                               (n_slab, CONV_WIDTH, V7X_SUBLANES, cw))
    vec = lambda: pl.BlockSpec((1, d), lambda i: (0, 0))
    pipelined = (_nbytes((tm, d), BF16) + 2 * _nbytes((HALO_ROWS, d), BF16)
                 + _nbytes((HALO_ROWS, d), F32) + 2 * _nbytes((tm, d), F32))
    resident = (_nbytes((d, d), BF16) + 2 * _nbytes((tm + HALO_ROWS, d), F32)
                + 4 * _nbytes((tm, d), F32))
    return pl.pallas_call(
        body,
        grid=(rows // tm,),
        in_specs=[
            pl.BlockSpec((tm, d), lambda i: (i, 0)),
            pl.BlockSpec((HALO_ROWS, d), lambda i: (jnp.maximum(i * halo_blocks_per_tile - 1, 0), 0)),
            pl.BlockSpec((HALO_ROWS, d), lambda i: (0, 0)),
            _resident((n_slab, CONV_WIDTH, V7X_SUBLANES, cw)),
            vec(), vec(), vec(),
            _resident((d, d)),
            vec(), vec(),
            pl.BlockSpec((tm, d), lambda i: (i, 0)),
        ],
        out_specs=pl.BlockSpec((tm, d), lambda i: (i, 0)),
        out_shape=jax.ShapeDtypeStruct((rows, d), F32),
        scratch_shapes=[pltpu.VMEM((n_slab, 2 * (tm + HALO_ROWS), cw), F32), pltpu.VMEM((tm, d), F32),
                        pltpu.VMEM((tm, d), BF16)],
        compiler_params=pltpu.CompilerParams(
            dimension_semantics=("parallel",),
            vmem_limit_bytes=_vmem_limit(pipelined, resident)),
        name="conv_out",
    )(v, vprev_src, halo0, w_slabs, b_dw, ln_g, ln_b, w_out, b_out, gpost, h)


def _ffn_body(h_ref, gpre_ref, wg_ref, wu_ref, wd_ref, gpost_ref, o_ref, xn_ref, *, tc):
    f = pl.program_id(1)
    tf, d = wd_ref.shape

    last_f = pl.num_programs(1) - 1
    tm = o_ref.shape[0]

    def hidden_tile(first, last):
        xn = xn_ref[...]
        acts = []
        for c in range(0, tf, tc):
            g = _mm(xn, wg_ref[:, c:c + tc])
            u = _mm(xn, wu_ref[:, c:c + tc])
            acts.append((g * _sigmoid(g) * u).astype(BF16))
        a = jnp.concatenate(acts, axis=1)
        tr = min(tm, FFN_TAIL_ROWS) if last else tm
        for r in range(0, tm, tr):
            for c in range(0, d, tc):
                part = _mm(a[r:r + tr], wd_ref[:, c:c + tc])
                if first:
                    o_ref[r:r + tr, c:c + tc] = part
                else:
                    o_ref[r:r + tr, c:c + tc] += part
            if last:
                _post_norm_residual(o_ref.at[r:r + tr], h_ref.at[r:r + tr], gpost_ref)

    @pl.when(f == 0)
    def _():
        _pre_norm(h_ref, gpre_ref, xn_ref)
        hidden_tile(first=True, last=False)

    @pl.when((f > 0) & (f < last_f))
    def _():
        hidden_tile(first=False, last=False)

    @pl.when(f == last_f)
    def _():
        hidden_tile(first=False, last=True)


def _ffn(h, gpre, w_gate, w_up, w_down, gpost, *, layer, tm, tf):
    rows, d = h.shape
    ff = w_gate.shape[2]
    tc = min(tf, FFN_RESULT_COLS)
    assert ff // tf >= 2, "first and last hidden tile are separate grid steps"
    pipelined = 2 * _nbytes((tm, d), F32) + 3 * _nbytes((d, tf), BF16)
    resident = _nbytes((tm, d), BF16) + _nbytes((tm, tf), BF16) + 6 * _nbytes((tm, tc), F32)
    return pl.pallas_call(
        functools.partial(_ffn_body, tc=tc),
        grid=(rows // tm, ff // tf),
        in_specs=[
            pl.BlockSpec((tm, d), lambda i, f: (i, 0)),
            pl.BlockSpec((1, d), lambda i, f: (0, 0)),
            pl.BlockSpec((None, d, tf), lambda i, f: (layer, 0, f)),
            pl.BlockSpec((None, d, tf), lambda i, f: (layer, 0, f)),
            pl.BlockSpec((None, tf, d), lambda i, f: (layer, f, 0)),
            pl.BlockSpec((1, d), lambda i, f: (0, 0)),
        ],
        out_specs=pl.BlockSpec((tm, d), lambda i, f: (i, 0)),
        out_shape=jax.ShapeDtypeStruct((rows, d), F32),
        scratch_shapes=[pltpu.VMEM((tm, d), BF16)],
        compiler_params=pltpu.CompilerParams(
            dimension_semantics=("parallel", "arbitrary"),
            vmem_limit_bytes=_vmem_limit(pipelined, resident)),
        name="ffn",
    )(h, gpre, w_gate, w_up, w_down, gpost)


def _qkv_body(h_ref, gpre_ref, w_ref, b_ref, q_ref, k_ref, v_ref, xn_ref, *, tn, n_kv):
    d = h_ref.shape[1]
    _pre_norm(h_ref, gpre_ref, xn_ref)
    xn = xn_ref[...]
    y = _mm(xn, w_ref[:, d:]) + b_ref[:, d:]
    ones = jnp.ones((y.shape[0], HEAD_DIM), BF16)
    for hh in range(n_kv):
        k_ref[hh] = y[:, hh * HEAD_DIM:(hh + 1) * HEAD_DIM].astype(BF16)
        vh = y[:, (n_kv + hh) * HEAD_DIM:(n_kv + hh + 1) * HEAD_DIM].astype(BF16)
        v_ref[hh] = jnp.concatenate([vh, ones, ones, vh], axis=1)
    for j in range(d // tn):
        cq = slice(j * tn, (j + 1) * tn)
        q_ref[:, cq] = ((_mm(xn, w_ref[:, cq]) + b_ref[:, cq]) * (HEAD_DIM ** -0.5)).astype(BF16)


def _qkv(h, gpre, w_qkv, b_qkv, *, tm, tn, n_kv):
    rows, d = h.shape
    pipelined = (_nbytes((tm, d), F32) + _nbytes((tm, d), BF16)
                 + _nbytes((n_kv, tm, V7X_LANES), BF16) + _nbytes((n_kv, tm, V_EXT), BF16))
    resident = _nbytes(w_qkv.shape, BF16) + _nbytes((tm, d), BF16) + 4 * _nbytes((tm, tn), F32)
    return pl.pallas_call(
        functools.partial(_qkv_body, tn=tn, n_kv=n_kv),
        grid=(rows // tm,),
        in_specs=[
            pl.BlockSpec((tm, d), lambda i: (i, 0)),
            _resident((1, d)), _resident(w_qkv.shape), _resident(b_qkv.shape),
        ],
        out_specs=[
            pl.BlockSpec((tm, d), lambda i: (i, 0)),
            pl.BlockSpec((n_kv, tm, HEAD_DIM), lambda i: (0, i, 0)),
            pl.BlockSpec((n_kv, tm, V_EXT), lambda i: (0, i, 0)),
        ],
        out_shape=[jax.ShapeDtypeStruct((rows, d), BF16),
                   jax.ShapeDtypeStruct((n_kv, rows, HEAD_DIM), BF16),
                   jax.ShapeDtypeStruct((n_kv, rows, V_EXT), BF16)],
        scratch_shapes=[pltpu.VMEM((tm, d), BF16)],
        compiler_params=pltpu.CompilerParams(
            dimension_semantics=("parallel",),
            vmem_limit_bytes=_vmem_limit(pipelined, resident)),
        name="qkv",
    )(h, gpre, w_qkv, b_qkv)


def _attn_body(q_ref, kp_ref, kc_ref, vp_ref, vc_ref, km_ref, vm_ref, tbl_ref, o_ref):
    qw = GROUP * HEAD_DIM
    n_tail = SLOTS - BAND_SLOTS - N_META
    zpad = jnp.zeros((n_tail, HEAD_DIM), BF16)
    lane = lax.broadcasted_iota(jnp.int32, (n_tail, V_EXT), 1)
    vtail = jnp.where((lane >= HEAD_DIM) & (lane < V_EXT - HEAD_DIM), 1.0, 0.0).astype(BF16)
    units = [(hh, sb) for hh in range(kp_ref.shape[0]) for sb in range(N_SUB)]
    band = lambda p_ref, c_ref, hh, lo: jnp.concatenate(
        [p_ref[hh, lo:BLOCK], c_ref[hh, 0:lo + SUB]], axis=0)
    scores, vfs = [], []
    for hh, sb in units:
        lo = sb * SUB
        kf = jnp.concatenate([band(kp_ref, kc_ref, hh, lo), km_ref[hh], zpad], axis=0)
        vfs.append(jnp.concatenate([band(vp_ref, vc_ref, hh, lo), vm_ref[hh], vtail], axis=0))
        q = q_ref[lo:lo + SUB, hh * qw:(hh + 1) * qw]
        qs = jnp.concatenate([q[:, g * HEAD_DIM:(g + 1) * HEAD_DIM] for g in range(GROUP)], axis=0)
        s = lax.dot_general(qs, kf, (((1,), (1,)), ((), ())), preferred_element_type=F32)
        scores.append(s + tbl_ref[0, hh, sb])
    probs = [jnp.exp(s - jnp.max(s, axis=-1, keepdims=True)).astype(BF16) for s in scores]
    outs = []
    for p, vf in zip(probs, vfs):
        ol = _mm(p, vf)
        outs.append((ol[:, :V_EXT // 2] / ol[:, V_EXT // 2:])[:, :HEAD_DIM])
    for hh in range(kp_ref.shape[0]):
        o_ref[:, hh * qw:(hh + 1) * qw] = jnp.concatenate(
            [jnp.concatenate([o[g * SUB:(g + 1) * SUB] for o in outs[hh * N_SUB:(hh + 1) * N_SUB]],
                             axis=0) for g in range(GROUP)], axis=1).astype(o_ref.dtype)


def _attention(q, k, v, k_meta, v_meta, tbl, *, batch, seq, hps):
    rows, d = q.shape
    n_kv = k.shape[0]
    n_blk = seq // BLOCK
    qw = hps * GROUP * HEAD_DIM
    row_blk = lambda h, b, n: b * n_blk + n
    prev_blk = lambda h, b, n: b * n_blk + jnp.maximum(n - 1, 0)
    kv_spec = lambda fn, w: pl.BlockSpec((hps, BLOCK, w), lambda h, b, n: (h, fn(h, b, n), 0))
    meta_spec = lambda w: pl.BlockSpec((hps, N_META, w), lambda h, b, n: (h, 0, 0))
    pipelined = (2 * _nbytes((BLOCK, qw), BF16) + 2 * _nbytes((hps, BLOCK, V7X_LANES), BF16)
                 + 2 * _nbytes((hps, BLOCK, V_EXT), BF16)
                 + _nbytes((hps, GROUP * BLOCK, SLOTS), F32))
    resident = 6 * hps * _nbytes((GROUP * BLOCK, SLOTS), F32)
    return pl.pallas_call(
        _attn_body,
        grid=(n_kv // hps, batch, n_blk),
        in_specs=[
            pl.BlockSpec((BLOCK, qw), lambda h, b, n: (row_blk(h, b, n), h)),
            kv_spec(prev_blk, HEAD_DIM), kv_spec(row_blk, HEAD_DIM),
            kv_spec(prev_blk, V_EXT), kv_spec(row_blk, V_EXT),
            meta_spec(HEAD_DIM), meta_spec(V_EXT),
            pl.BlockSpec((1, hps, N_SUB, GROUP * SUB, SLOTS),
                         lambda h, b, n: (jnp.minimum(n, 1), h, 0, 0, 0)),
        ],
        out_specs=pl.BlockSpec((BLOCK, qw), lambda h, b, n: (row_blk(h, b, n), h)),
        out_shape=jax.ShapeDtypeStruct((rows, d), BF16),
        compiler_params=pltpu.CompilerParams(
            dimension_semantics=("parallel", "parallel", "parallel"),
            vmem_limit_bytes=_vmem_limit(pipelined, resident)),
        name="swa_attention",
    )(q, k, k, v, v, k_meta, v_meta, tbl)


def _oproj_body(a_ref, w_ref, b_ref, gpost_ref, h_ref, o_ref):
    o_ref[...] = _mm(a_ref[...], w_ref[...]) + b_ref[...]
    _post_norm_residual(o_ref, h_ref, gpost_ref)


def _oproj(a, w_o, b_o, gpost, h, *, tm):
    rows, d = h.shape
    vec = lambda: pl.BlockSpec((1, d), lambda i: (0, 0))
    pipelined = _nbytes((tm, d), BF16) + 2 * _nbytes((tm, d), F32)
    resident = _nbytes((d, d), BF16) + 3 * _nbytes((tm, d), F32)
    return pl.pallas_call(
        _oproj_body,
        grid=(rows // tm,),
        in_specs=[
            pl.BlockSpec((tm, d), lambda i: (i, 0)),
            _resident((d, d)),
            vec(), vec(),
            pl.BlockSpec((tm, d), lambda i: (i, 0)),
        ],
        out_specs=pl.BlockSpec((tm, d), lambda i: (i, 0)),
        out_shape=jax.ShapeDtypeStruct((rows, d), F32),
        compiler_params=pltpu.CompilerParams(
            dimension_semantics=("parallel",),
            vmem_limit_bytes=_vmem_limit(pipelined, resident)),
        name="attn_out_proj",
    )(a, w_o, b_o, gpost, h)


def _t5_bucket(dist):
    max_exact = N_BUCKETS // 2
    dd = jnp.maximum(dist, max_exact).astype(F32)
    large = max_exact + (jnp.log(dd / max_exact) / math.log(MAX_DISTANCE / max_exact)
                         * (N_BUCKETS - max_exact)).astype(jnp.int32)
    return jnp.where(dist < max_exact, dist, jnp.minimum(large, N_BUCKETS - 1))


def _slot_table(rel_bias, sinks, n_kv):
    n_heads = rel_bias.shape[1]
    rb = rel_bias.astype(F32)

    def lookup(bucket):
        onehot = (bucket[..., None] == jnp.arange(N_BUCKETS)).astype(F32)
        return jnp.einsum("...b,bh->h...", onehot, rb, precision=lax.Precision.HIGHEST)

    qpos = jnp.arange(BLOCK)[:, None]
    kpos = jnp.arange(2 * BLOCK)[None, :]
    mpos = jnp.arange(N_META)[None, :]
    d_band = BLOCK + qpos - kpos
    in_window = (d_band >= 0) & (d_band < WINDOW)
    bias_band = lookup(_t5_bucket(jnp.maximum(d_band, 0)))
    sink = jnp.broadcast_to(sinks.astype(F32)[:, None, None], (n_heads, SUB, 1))
    pad = jnp.full((n_heads, SUB, SLOTS - BAND_SLOTS - N_META - 1), NEG_INF, F32)
    tables = []
    for n in (0, 1):
        valid = in_window & (n * BLOCK + kpos >= BLOCK)
        band = jnp.where(valid[None], bias_band, NEG_INF)
        meta = lookup(_t5_bucket(N_META + n * BLOCK + qpos - mpos))
        subs = []
        for sb in range(N_SUB):
            lo = sb * SUB
            subs.append(jnp.concatenate([band[:, lo:lo + SUB, lo:lo + BAND_SLOTS],
                                         meta[:, lo:lo + SUB], sink, pad], axis=2))
        t = jnp.stack(subs, axis=1).reshape(n_kv, GROUP, N_SUB, SUB, SLOTS)
        tables.append(jnp.transpose(t, (0, 2, 1, 3, 4)).reshape(n_kv, N_SUB, GROUP * SUB, SLOTS))
    return jnp.stack(tables)


def kernel(x, meta_tokens, rel_bias, conv_w_in, conv_b_in, conv_w_dw, conv_b_dw, conv_ln_g, conv_ln_b,
           conv_w_out, conv_b_out, attn_w_qkv, attn_b_qkv, attn_sinks, attn_w_o, attn_b_o,
           norm_mix_pre, norm_mix_post, norm_ffn_pre, norm_ffn_post, ffn_w_gate, ffn_w_up, ffn_w_down):
    batch, seq, d = x.shape
    n_kv = (attn_w_qkv.shape[2] - d) // (2 * HEAD_DIM)
    rows = batch * seq
    row = lambda a: a.reshape(1, -1).astype(F32)
    bf = functools.partial(_to_bf16, kb=CAST_ROWS)

    tm_mm, tn_mm, tm_conv, tm_ffn, tf = 1024, 256, 512, 1024, 512
    tm_meta = N_META

    h0 = x.reshape(rows, d)
    hm0 = meta_tokens.astype(x.dtype)
    w_gate, w_up, w_down = bf(ffn_w_gate), bf(ffn_w_up), bf(ffn_w_down)
    ffn = lambda h, layer, tm: _ffn(h, row(norm_ffn_pre[layer]), w_gate, w_up, w_down,
                                    row(norm_ffn_post[layer]), layer=layer, tm=tm, tf=tf)

    w_in, b_in = bf(conv_w_in)[0], row(conv_b_in[0])
    conv_args = (conv_w_dw[0].astype(F32), row(conv_b_dw[0]), row(conv_ln_g[0]), row(conv_ln_b[0]),
                 bf(conv_w_out)[0], row(conv_b_out[0]), row(norm_mix_post[0]))
    gpre0 = row(norm_mix_pre[0])

    vm = _conv_in(hm0, gpre0, w_in, b_in, tm=tm_meta, tn=tn_mm)
    zero_halo = jnp.zeros((HALO_ROWS, d), BF16)
    hm1 = _conv_out(vm, zero_halo, zero_halo, *conv_args, hm0, tm=tm_meta, rows_per_seq=N_META)
    hm2 = ffn(hm1, 0, tm_meta)

    v = _conv_in(h0, gpre0, w_in, b_in, tm=tm_mm, tn=tn_mm)
    halo0 = jnp.concatenate([jnp.zeros((HALO_ROWS - N_META, d), BF16), vm], axis=0)
    h1 = _conv_out(v, v, halo0, *conv_args, h0, tm=tm_conv, rows_per_seq=seq)
    h2 = ffn(h1, 0, tm_ffn)

    w_qkv, b_qkv = bf(attn_w_qkv)[0], row(attn_b_qkv[0])
    gpre1 = row(norm_mix_pre[1])
    _, k_meta, v_meta = _qkv(hm2, gpre1, w_qkv, b_qkv, tm=tm_meta, tn=tn_mm, n_kv=n_kv)
    q, k, vv = _qkv(h2, gpre1, w_qkv, b_qkv, tm=tm_mm, tn=tn_mm, n_kv=n_kv)
    tbl = _slot_table(rel_bias, attn_sinks[0], n_kv)
    a = _attention(q, k, vv, k_meta, v_meta, tbl, batch=batch, seq=seq, hps=ATTN_KV_HEADS_PER_STEP)
    h3 = _oproj(a, bf(attn_w_o)[0], row(attn_b_o[0]), row(norm_mix_post[1]), h2, tm=tm_mm)
    h4 = ffn(h3, 1, tm_ffn)
    return h4.reshape(batch, seq, d)
```
